```python
import math
import jax, jax.numpy as jnp
from jax import lax
import numpy as np

D_MODEL = 2048
BATCH = 2
SEQ = 16384
DEPTH = 4

CTX_LEN = 256
GRID_W = 64

DIFF_HEADS = 8
DIFF_QK_DIM = 64
DIFF_V_DIM = 2 * DIFF_QK_DIM
DIFF_SCALE = DIFF_QK_DIM ** -0.5

SWA_HEADS = 8
SWA_KV_HEADS = 2
SWA_GROUP = SWA_HEADS // SWA_KV_HEADS
SWA_HEAD_DIM = 128
SWA_SCALE = SWA_HEAD_DIM ** -0.5
WINDOW = 128
SWA_BLOCK = WINDOW

Q_BLOCK = 128

DIFF_Q_W = DIFF_HEADS * 2 * DIFF_QK_DIM
DIFF_K_W = DIFF_HEADS * 2 * DIFF_QK_DIM
DIFF_V_W = DIFF_HEADS * DIFF_V_DIM
SWA_Q_W = SWA_HEADS * SWA_HEAD_DIM
SWA_K_W = SWA_KV_HEADS * SWA_HEAD_DIM
SWA_V_W = SWA_KV_HEADS * SWA_HEAD_DIM
IN_WIDTHS = (DIFF_Q_W, DIFF_K_W, DIFF_V_W, SWA_Q_W, SWA_K_W, SWA_V_W)
IN_W = 4608
MIX_W = DIFF_HEADS * DIFF_V_DIM + SWA_HEADS * SWA_HEAD_DIM

D_FF = 5632
CONV_W = 3
ROPE_THETA = 10000.0
EPS = 1e-6
NEG_INF = -1e30

kernel_name = 'hybrid_diffattn_swa_convffn_dit'


def _rms(x, g):
    xf = x.astype(jnp.float32)
    y = xf * lax.rsqrt(jnp.mean(xf * xf, axis=-1, keepdims=True) + EPS)
    return (y * g.astype(jnp.float32)).astype(x.dtype)


def _modulate(h, shift, scale):
    return h * (1 + scale) + shift


def _ada(cvec, w, b):
    return jnp.split(jax.nn.silu(cvec) @ w + b, 6, axis=-1)


def _split_in(p):
    cuts = [int(i) for i in np.cumsum(IN_WIDTHS)[:-1]]
    return jnp.split(p, cuts, axis=-1)


def _rope_axis(x, pos):
    half = x.shape[-1] // 2
    inv = ROPE_THETA ** (-jnp.arange(half, dtype=jnp.float32) / half)
    ang = pos.astype(jnp.float32)[:, None] * inv[None, :]
    ang = ang.reshape((pos.shape[0],) + (1,) * (x.ndim - 3) + (half,))
    cos, sin = jnp.cos(ang), jnp.sin(ang)
    xf = x.astype(jnp.float32)
    x1, x2 = xf[..., :half], xf[..., half:]
    return jnp.concatenate([x1 * cos - x2 * sin, x1 * sin + x2 * cos], axis=-1).astype(x.dtype)


def _rope_2d(x, row, col):
    d = x.shape[-1]
    return jnp.concatenate([_rope_axis(x[..., : d // 2], row), _rope_axis(x[..., d // 2:], col)], axis=-1)


def _diff_attend(q, k, v, lam):
    s = jnp.einsum('bqhcd,bkhcd->bhcqk', q, k).astype(jnp.float32)
    p = jax.nn.softmax(s, axis=-1)
    pd = p[:, :, 0] - lam * p[:, :, 1]
    return jnp.einsum('bhqk,bkhe->bqhe', pd.astype(v.dtype), v)


def _diff_latent(q, k_all, v_all, lam):
    B, S = q.shape[:2]
    nb = S // Q_BLOCK
    qb = jnp.moveaxis(q.reshape((B, nb, Q_BLOCK) + q.shape[2:]), 1, 0)
    o = lax.map(lambda qblk: _diff_attend(qblk, k_all, v_all, lam), qb)
    return jnp.moveaxis(o, 0, 1).reshape((B, S) + o.shape[3:])


def _swa_context(q, k, v, sink):
    B, C = q.shape[:2]
    s = jnp.einsum('bqhgd,bkhd->bhgqk', q, k).astype(jnp.float32)
    sk = jnp.broadcast_to(sink.astype(jnp.float32).reshape(SWA_KV_HEADS, SWA_GROUP)[None, :, :, None, None],
                          s.shape[:-1] + (1,))
    p = jax.nn.softmax(jnp.concatenate([s, sk], axis=-1), axis=-1)[..., :-1]
    o = jnp.einsum('bhgqk,bkhd->bqhgd', p.astype(v.dtype), v)
    return o.reshape(B, C, SWA_HEADS * SWA_HEAD_DIM)


def _swa_latent(q, k, v, kc, vc, sink):
    B, S = q.shape[:2]
    C = kc.shape[1]
    nb = S // SWA_BLOCK
    qb = q.reshape(B, nb, SWA_BLOCK, SWA_KV_HEADS, SWA_GROUP, SWA_HEAD_DIM)

    def band(t):
        tp = jnp.pad(t, ((0, 0), (SWA_BLOCK, SWA_BLOCK), (0, 0), (0, 0)))
        tp = tp.reshape(B, nb + 2, SWA_BLOCK, SWA_KV_HEADS, SWA_HEAD_DIM)
        return jnp.concatenate([tp[:, :-2], tp[:, 1:-1], tp[:, 2:]], axis=2)

    kb, vb = band(k), band(v)
    n = jnp.arange(nb)[:, None, None]
    a = jnp.arange(SWA_BLOCK)[None, :, None]
    bidx = jnp.arange(3 * SWA_BLOCK)[None, None, :]
    qpos = n * SWA_BLOCK + a
    kpos = n * SWA_BLOCK - SWA_BLOCK + bidx
    valid = (jnp.abs(qpos - kpos) <= WINDOW) & (kpos >= 0) & (kpos < S)

    s_band = jnp.einsum('bnqhgd,bnkhd->bhgnqk', qb, kb).astype(jnp.float32)
    s_band = jnp.where(valid, s_band, NEG_INF)
    s_ctx = jnp.einsum('bnqhgd,bkhd->bhgnqk', qb, kc).astype(jnp.float32)
    s_sink = jnp.broadcast_to(
        sink.astype(jnp.float32).reshape(SWA_KV_HEADS, SWA_GROUP)[None, :, :, None, None, None],
        s_band.shape[:-1] + (1,))
    p = jax.nn.softmax(jnp.concatenate([s_band, s_ctx, s_sink], axis=-1), axis=-1)
    pb = p[..., : 3 * SWA_BLOCK].astype(v.dtype)
    pc = p[..., 3 * SWA_BLOCK: 3 * SWA_BLOCK + C].astype(v.dtype)
    o = (jnp.einsum('bhgnqk,bnkhd->bnqhgd', pb, vb)
         + jnp.einsum('bhgnqk,bkhd->bnqhgd', pc, vc))
    return o.reshape(B, S, SWA_HEADS * SWA_HEAD_DIM)


def _conv_ffn(h, wg, wu, cw, cb, wd):
    g = h @ wg
    gp = jnp.pad(g, ((0, 0), (1, 1), (0, 0)))
    g = gp[:, :-2] * cw[0] + gp[:, 1:-1] * cw[1] + gp[:, 2:] * cw[2] + cb
    return (jax.nn.silu(g) * (h @ wu)) @ wd


def setup_inputs(seed: int = 0) -> dict:
    key = jax.random.key(seed)
    ks = jax.random.split(key, 24)

    def nrm(k, shape, std):
        return std * jax.random.normal(k, shape, jnp.float32)

    D = D_MODEL
    return {
        'x': nrm(ks[0], (BATCH, SEQ, D), 1.0),
        'c': nrm(ks[1], (BATCH, D), 1.0),
        'ctx': nrm(ks[2], (BATCH, CTX_LEN, D), 1.0),
        'c_ctx': nrm(ks[3], (D,), 1.0),
        'w_ada': nrm(ks[4], (DEPTH, D, 6 * D), 0.5 * D ** -0.5),
        'b_ada': nrm(ks[5], (DEPTH, 6 * D), 0.01),
        'g_pre_mix': 1.0 + nrm(ks[6], (DEPTH, D), 0.02),
        'g_post_mix': 1.0 + nrm(ks[7], (DEPTH, D), 0.02),
        'w_in': nrm(ks[8], (DEPTH, D, IN_W), D ** -0.5),
        'diff_lambda': nrm(ks[9], (DEPTH, 4, DIFF_QK_DIM), 0.1),
        'diff_subln': 1.0 + nrm(ks[10], (DEPTH, DIFF_V_DIM), 0.02),
        'swa_sink': nrm(ks[11], (DEPTH, SWA_HEADS), 0.5),
        'w_out': nrm(ks[12], (DEPTH, MIX_W, D), MIX_W ** -0.5),
        'g_pre_ffn': 1.0 + nrm(ks[13], (DEPTH, D), 0.02),
        'g_post_ffn': 1.0 + nrm(ks[14], (DEPTH, D), 0.02),
        'w_ffn_gate': nrm(ks[15], (DEPTH, D, D_FF), D ** -0.5),
        'w_ffn_up': nrm(ks[16], (DEPTH, D, D_FF), D ** -0.5),
        'ffn_conv_w': nrm(ks[17], (DEPTH, CONV_W, D_FF), CONV_W ** -0.5),
        'ffn_conv_b': nrm(ks[18], (DEPTH, D_FF), 0.01),
        'w_ffn_down': nrm(ks[19], (DEPTH, D_FF, D), D_FF ** -0.5),
    }


def reference(x, c, ctx, c_ctx, w_ada, b_ada, g_pre_mix, g_post_mix, w_in, diff_lambda, diff_subln,
              swa_sink, w_out, g_pre_ffn, g_post_ffn, w_ffn_gate, w_ffn_up, ffn_conv_w, ffn_conv_b,
              w_ffn_down):
    B, S, _ = x.shape
    C = ctx.shape[1]
    rows = S // GRID_W
    row = jnp.repeat(jnp.arange(rows, dtype=jnp.int32), GRID_W)
    col = jnp.tile(jnp.arange(GRID_W, dtype=jnp.int32), rows)

    xl, xc = x, ctx
    for l in range(DEPTH):
        last = l == DEPTH - 1
        lam_init = 0.8 - 0.6 * math.exp(-0.3 * l)
        sh_a, sc_a, gt_a, sh_f, sc_f, gt_f = [m[:, None, :] for m in _ada(c, w_ada[l], b_ada[l])]
        csh_a, csc_a, cgt_a, csh_f, csc_f, cgt_f = _ada(c_ctx, w_ada[l], b_ada[l])

        hl = _modulate(_rms(xl, g_pre_mix[l]), sh_a, sc_a)
        hc = _modulate(_rms(xc, g_pre_mix[l]), csh_a, csc_a)
        dq, dk, dv, sq, sk, sv = _split_in(hl @ w_in[l])
        dqc, dkc, dvc, sqc, skc, svc = _split_in(hc @ w_in[l])

        lq1, lk1, lq2, lk2 = [diff_lambda[l, i].astype(jnp.float32) for i in range(4)]
        lam = jnp.exp(jnp.sum(lq1 * lk1)) - jnp.exp(jnp.sum(lq2 * lk2)) + lam_init

        qd = _rope_2d(dq.reshape(B, S, DIFF_HEADS, 2, DIFF_QK_DIM), row, col) * DIFF_SCALE
        kd = _rope_2d(dk.reshape(B, S, DIFF_HEADS, 2, DIFF_QK_DIM), row, col)
        vd = dv.reshape(B, S, DIFF_HEADS, DIFF_V_DIM)
        kdc = dkc.reshape(B, C, DIFF_HEADS, 2, DIFF_QK_DIM)
        vdc = dvc.reshape(B, C, DIFF_HEADS, DIFF_V_DIM)
        od = _diff_latent(qd, jnp.concatenate([kd, kdc], axis=1), jnp.concatenate([vd, vdc], axis=1), lam)
        od = (_rms(od, diff_subln[l]) * (1 - lam_init)).reshape(B, S, DIFF_V_W)

        qs = _rope_2d(sq.reshape(B, S, SWA_HEADS, SWA_HEAD_DIM), row, col) * SWA_SCALE
        qs = qs.reshape(B, S, SWA_KV_HEADS, SWA_GROUP, SWA_HEAD_DIM)
        ks_ = _rope_2d(sk.reshape(B, S, SWA_KV_HEADS, SWA_HEAD_DIM), row, col)
        vs = sv.reshape(B, S, SWA_KV_HEADS, SWA_HEAD_DIM)
        ksc = skc.reshape(B, C, SWA_KV_HEADS, SWA_HEAD_DIM)
        vsc = svc.reshape(B, C, SWA_KV_HEADS, SWA_HEAD_DIM)
        osw = _swa_latent(qs, ks_, vs, ksc, vsc, swa_sink[l])

        mix_l = jnp.concatenate([od, osw], axis=-1) @ w_out[l]
        xl = xl + gt_a * _rms(mix_l, g_post_mix[l])

        if not last:
            qdc = dqc.reshape(B, C, DIFF_HEADS, 2, DIFF_QK_DIM) * DIFF_SCALE
            odc = _diff_attend(qdc, kdc, vdc, lam)
            odc = (_rms(odc, diff_subln[l]) * (1 - lam_init)).reshape(B, C, DIFF_V_W)
            qsc = sqc.reshape(B, C, SWA_KV_HEADS, SWA_GROUP, SWA_HEAD_DIM) * SWA_SCALE
            oswc = _swa_context(qsc, ksc, vsc, swa_sink[l])
            mix_c = jnp.concatenate([odc, oswc], axis=-1) @ w_out[l]
            xc = xc + cgt_a * _rms(mix_c, g_post_mix[l])

        hl = _modulate(_rms(xl, g_pre_ffn[l]), sh_f, sc_f)
        fl = _conv_ffn(hl, w_ffn_gate[l], w_ffn_up[l], ffn_conv_w[l], ffn_conv_b[l], w_ffn_down[l])
        xl = xl + gt_f * _rms(fl, g_post_ffn[l])
        if not last:
            hc = _modulate(_rms(xc, g_pre_ffn[l]), csh_f, csc_f)
            fc = _conv_ffn(hc, w_ffn_gate[l], w_ffn_up[l], ffn_conv_w[l], ffn_conv_b[l], w_ffn_down[l])
            xc = xc + cgt_f * _rms(fc, g_post_ffn[l])
    return xl
```

```python
import functools
import math

import jax
import jax.numpy as jnp
import numpy as np
from jax import lax
from jax.experimental import pallas as pl
from jax.experimental.pallas import tpu as pltpu

BF = jnp.bfloat16
F32 = jnp.float32

GRID_W = 64
DIFF_HEADS = 8
DIFF_QK_DIM = 64
DIFF_V_DIM = 128
SWA_HEADS = 8
SWA_KV_HEADS = 2
SWA_GROUP = SWA_HEADS // SWA_KV_HEADS
SWA_HEAD_DIM = 128
WINDOW = 128
ROPE_THETA = 10000.0
EPS = 1e-6
NEG_INF = -1e30
LOG2E = 1.4426950408889634
DIFF_QSCALE = DIFF_QK_DIM ** -0.5 * LOG2E
SWA_QSCALE = SWA_HEAD_DIM ** -0.5 * LOG2E

SLAB = 128
DQ0, DK0, DV0, SQ0, SK0, SV0 = 0, 8, 16, 24, 32, 34
N_SLABS = 36
IN_W = N_SLABS * SLAB

VMEM_LIMIT = 56 * 1024 * 1024
NORM_CHUNK = 64


def _params(sem, vmem=VMEM_LIMIT):
    return pltpu.CompilerParams(dimension_semantics=sem, vmem_limit_bytes=vmem)


def _resident(shape, index_map):
    return pl.BlockSpec(shape, index_map, pipeline_mode=pl.Buffered(1))


def _sigmoid(v):
    return 1.0 / (1.0 + jnp.exp(-v))


def _rms_mod(xf, g, sh, sc):
    ms = jnp.mean(xf * xf, axis=-1, keepdims=True)
    return (xf * lax.rsqrt(ms + EPS) * g) * (1.0 + sc) + sh


def _rope(x, cos, sin, shift):
    lane = lax.broadcasted_iota(jnp.int32, x.shape, 1)
    fwd = pltpu.roll(x, SLAB - shift, axis=1)
    bwd = pltpu.roll(x, shift, axis=1)
    partner = jnp.where((lane & (2 * shift - 1)) < shift, fwd, bwd)
    return x * cos + partner * sin


def _ada_kernel(c_ref, w_ref, b_ref, o_ref):
    cv = c_ref[...]
    s = (cv * _sigmoid(cv)).astype(BF)
    o_ref[...] = jnp.dot(s, w_ref[...].astype(BF), preferred_element_type=F32) + b_ref[...]


def _ada(cvec, w_ada, b_ada):
    depth, d, n = w_ada.shape
    rows = cvec.shape[0]
    bn = 1024
    return pl.pallas_call(
        _ada_kernel,
        grid=(depth, n // bn),
        in_specs=[
            pl.BlockSpec((rows, d), lambda l, j: (0, 0)),
            pl.BlockSpec((None, d, bn), lambda l, j: (l, 0, j)),
            pl.BlockSpec((None, 1, bn), lambda l, j: (l, 0, j)),
        ],
        out_specs=pl.BlockSpec((None, rows, bn), lambda l, j: (l, 0, j)),
        out_shape=jax.ShapeDtypeStruct((depth, rows, n), F32),
        compiler_params=_params(("arbitrary", "arbitrary")),
        name="ada",
    )(cvec, w_ada, b_ada.reshape(depth, 1, n))


def _mod_spec(layer, k, row_fn, nargs):
    if nargs == 1:
        return lambda d: pl.BlockSpec((None, None, None, 1, d), lambda i: (layer, row_fn(i), k, 0, 0))
    return lambda d: pl.BlockSpec((None, None, None, 1, d), lambda i, j: (layer, row_fn(i), k, 0, 0))


def _inproj_kernel(x_ref, sh_ref, sc_ref, g_ref, w_ref, cd_ref, sd_ref, cs_ref, ss_ref, o_ref, h_scr, *, bm):
    g, sh, sc = g_ref[...], sh_ref[...], sc_ref[...]

    def norm_rows(r, carry):
        rows = pl.ds(pl.multiple_of(r * NORM_CHUNK, NORM_CHUNK), NORM_CHUNK)
        h_scr[rows, :] = _rms_mod(x_ref[rows, :], g, sh, sc).astype(BF)
        return carry

    lax.fori_loop(0, bm // NORM_CHUNK, norm_rows, 0)

    group = 4
    for grp in range(N_SLABS // group):
        acc = jnp.dot(h_scr[...], w_ref[:, grp * group * SLAB:(grp + 1) * group * SLAB],
                      preferred_element_type=F32)
        for t in range(group):
            slab = grp * group + t
            a = acc[:, t * SLAB:(t + 1) * SLAB]
            if slab < DV0:
                a = _rope(a, cd_ref[...], sd_ref[...], DIFF_QK_DIM // 4)
                if slab < DK0:
                    a = a * DIFF_QSCALE
            elif SQ0 <= slab < SV0:
                a = _rope(a, cs_ref[...], ss_ref[...], SWA_HEAD_DIM // 4)
                if slab < SK0:
                    a = a * SWA_QSCALE
            o_ref[:, slab * SLAB:(slab + 1) * SLAB] = a.astype(BF)


def _inproj(x, mods, layer, row_fn, g_pre, w, tables, seq, bm):
    rows, d = x.shape
    nt = seq // bm
    mod = lambda k: _mod_spec(layer, k, row_fn, 1)(d)
    tab = pl.BlockSpec((bm, SLAB), lambda i: (i % nt, 0))
    return pl.pallas_call(
        functools.partial(_inproj_kernel, bm=bm),
        grid=(rows // bm,),
        in_specs=[
            pl.BlockSpec((bm, d), lambda i: (i, 0)),
            mod(0), mod(1),
            pl.BlockSpec((1, d), lambda i: (0, 0)),
            _resident((d, IN_W), lambda i: (0, 0)),
            tab, tab, tab, tab,
        ],
        out_specs=pl.BlockSpec((bm, IN_W), lambda i: (i, 0)),
        out_shape=jax.ShapeDtypeStruct((rows, IN_W), BF),
        scratch_shapes=[pltpu.VMEM((bm, d), BF)],
        compiler_params=_params(("arbitrary",)),
        name="inproj",
    )(x, mods, mods, g_pre, w, *tables)


def _diff_kernel(*refs, n_lat, bk, lam_init):
    if n_lat:
        lam_ref, sub_ref, q_ref, kl_ref, vl_ref, kc_ref, vc_ref, o_ref = refs
    else:
        lam_ref, sub_ref, q_ref, kc_ref, vc_ref, o_ref = refs
    lp = lam_ref[...]
    lam = (jnp.exp(jnp.sum(lp[0:1] * lp[1:2], axis=-1, keepdims=True))
           - jnp.exp(jnp.sum(lp[2:3] * lp[3:4], axis=-1, keepdims=True)) + lam_init)

    q = q_ref[...]
    bq = q.shape[0]
    lane = lax.broadcasted_iota(jnp.int32, q.shape, 1)
    zero = jnp.zeros_like(q)
    q1 = jnp.where(lane < DIFF_QK_DIM, q, zero)
    q2 = jnp.where(lane >= DIFF_QK_DIM, q, zero)
    nt = (((1,), (1,)), ((), ()))

    def one(qz, kc, vc, m, l, a):
        s = lax.dot_general(qz, kc, nt, preferred_element_type=F32)
        mn = jnp.maximum(m, jnp.max(s, axis=-1, keepdims=True))
        alpha = jnp.exp2(m - mn)
        p = jnp.exp2(s - mn)
        l = alpha * l + jnp.sum(p, axis=-1, keepdims=True)
        a = alpha * a + jnp.dot(p.astype(BF), vc, preferred_element_type=F32)
        return mn, l, a

    def step(kc, vc, carry):
        m1, l1, a1, m2, l2, a2 = carry
        m1, l1, a1 = one(q1, kc, vc, m1, l1, a1)
        m2, l2, a2 = one(q2, kc, vc, m2, l2, a2)
        return m1, l1, a1, m2, l2, a2

    col = lambda v: jnp.full((bq, 1), v, F32)
    acc0 = jnp.zeros((bq, DIFF_V_DIM), F32)
    carry = (col(NEG_INF), col(0.0), acc0, col(NEG_INF), col(0.0), acc0)
    if n_lat:
        def lat_step(c, carry):
            rows = pl.ds(pl.multiple_of(c * bk, bk), bk)
            return step(kl_ref[rows, :], vl_ref[rows, :], carry)
        carry = lax.fori_loop(0, n_lat, lat_step, carry)
    m1, l1, a1, m2, l2, a2 = step(kc_ref[...], vc_ref[...], carry)

    o = a1 / l1 - lam * (a2 / l2)
    ms = jnp.mean(o * o, axis=-1, keepdims=True)
    o_ref[...] = (o * lax.rsqrt(ms + EPS) * sub_ref[...] * (1.0 - lam_init)).astype(BF)


def _diff_attention(p_q, p_lat, p_ctx, lam_par, subln, lam_init, bq, bk):
    b, sq, _ = p_q.shape
    c = p_ctx.shape[1]
    n_lat = 0 if p_lat is None else p_lat.shape[1] // bk
    in_specs = [
        pl.BlockSpec((4, DIFF_QK_DIM), lambda bi, h, qi: (0, 0)),
        pl.BlockSpec((1, DIFF_V_DIM), lambda bi, h, qi: (0, 0)),
        pl.BlockSpec((None, bq, SLAB), lambda bi, h, qi: (bi, qi, DQ0 + h)),
    ]
    args = [lam_par, subln, p_q]
    if n_lat:
        s = p_lat.shape[1]
        in_specs += [pl.BlockSpec((None, s, SLAB), lambda bi, h, qi: (bi, 0, DK0 + h)),
                     pl.BlockSpec((None, s, SLAB), lambda bi, h, qi: (bi, 0, DV0 + h))]
        args += [p_lat, p_lat]
    in_specs += [pl.BlockSpec((None, c, SLAB), lambda bi, h, qi: (bi, 0, DK0 + h)),
                 pl.BlockSpec((None, c, SLAB), lambda bi, h, qi: (bi, 0, DV0 + h))]
    args += [p_ctx, p_ctx]
    return pl.pallas_call(
        functools.partial(_diff_kernel, n_lat=n_lat, bk=bk, lam_init=lam_init),
        grid=(b, DIFF_HEADS, sq // bq),
        in_specs=in_specs,
        out_specs=pl.BlockSpec((None, bq, SLAB), lambda bi, h, qi: (bi, qi, h)),
        out_shape=jax.ShapeDtypeStruct((b, sq, DIFF_HEADS * DIFF_V_DIM), BF),
        compiler_params=_params(("arbitrary", "arbitrary", "arbitrary")),
        name="diff_attn" if n_lat else "diff_attn_ctx",
    )(*args)


def _swa_kernel(*refs, band, seq):
    if band:
        sink_ref, q_ref, kp_ref, kz_ref, kn_ref, vp_ref, vz_ref, vn_ref, kc_ref, vc_ref, o_ref = refs
    else:
        sink_ref, q_ref, kc_ref, vc_ref, o_ref = refs
    g = pl.program_id(1)
    n = pl.program_id(2)
    blk = q_ref.shape[0]
    q = q_ref[...]
    q4 = jnp.concatenate([q[:, h * SLAB:(h + 1) * SLAB] for h in range(SWA_GROUP)], axis=0)
    if band:
        kcat = jnp.concatenate([kp_ref[...], kz_ref[...], kn_ref[...], kc_ref[...]], axis=0)
        vcat = jnp.concatenate([vp_ref[...], vz_ref[...], vn_ref[...], vc_ref[...]], axis=0)
    else:
        kcat, vcat = kc_ref[...], vc_ref[...]
    s = lax.dot_general(q4, kcat, (((1,), (1,)), ((), ())), preferred_element_type=F32)
    row = lax.broadcasted_iota(jnp.int32, s.shape, 0)
    if band:
        colx = lax.broadcasted_iota(jnp.int32, s.shape, 1)
        qpos = n * blk + (row & (blk - 1))
        kpos = (n - 1) * blk + colx
        valid = (colx >= 3 * blk) | ((jnp.abs(qpos - kpos) <= WINDOW) & (kpos >= 0) & (kpos < seq))
        s = jnp.where(valid, s, NEG_INF)
    rowc = lax.broadcasted_iota(jnp.int32, (s.shape[0], 1), 0)
    sk = jnp.full((s.shape[0], 1), sink_ref[g * SWA_GROUP + SWA_GROUP - 1], F32)
    for h in range(SWA_GROUP - 2, -1, -1):
        sk = jnp.where(rowc < (h + 1) * blk, sink_ref[g * SWA_GROUP + h], sk)
    sk = sk * LOG2E
    m = jnp.maximum(jnp.max(s, axis=-1, keepdims=True), sk)
    p = jnp.exp2(s - m)
    l = jnp.sum(p, axis=-1, keepdims=True) + jnp.exp2(sk - m)
    o = jnp.dot(p.astype(BF), vcat, preferred_element_type=F32) / l
    for h in range(SWA_GROUP):
        o_ref[:, h * SLAB:(h + 1) * SLAB] = o[h * blk:(h + 1) * blk, :].astype(BF)


def _swa_attention(p_q, p_ctx, sink, band):
    b, sq, _ = p_q.shape
    c = p_ctx.shape[1]
    blk = WINDOW
    nb = sq // blk
    gw = SWA_GROUP * SLAB
    in_specs = [
        pl.BlockSpec(memory_space=pltpu.SMEM),
        pl.BlockSpec((None, blk, gw), lambda bi, g, n: (bi, n, SQ0 * SLAB // gw + g)),
    ]
    args = [sink, p_q]
    if band:
        for base in (SK0, SV0):
            in_specs += [
                pl.BlockSpec((None, blk, SLAB), lambda bi, g, n, base=base: (bi, jnp.maximum(n - 1, 0), base + g)),
                pl.BlockSpec((None, blk, SLAB), lambda bi, g, n, base=base: (bi, n, base + g)),
                pl.BlockSpec((None, blk, SLAB), lambda bi, g, n, base=base: (bi, jnp.minimum(n + 1, nb - 1), base + g)),
            ]
            args += [p_q, p_q, p_q]
    in_specs += [pl.BlockSpec((None, c, SLAB), lambda bi, g, n: (bi, 0, SK0 + g)),
                 pl.BlockSpec((None, c, SLAB), lambda bi, g, n: (bi, 0, SV0 + g))]
    args += [p_ctx, p_ctx]
    return pl.pallas_call(
        functools.partial(_swa_kernel, band=band, seq=sq),
        grid=(b, SWA_KV_HEADS, nb),
        in_specs=in_specs,
        out_specs=pl.BlockSpec((None, blk, gw), lambda bi, g, n: (bi, n, g)),
        out_shape=jax.ShapeDtypeStruct((b, sq, SWA_HEADS * SWA_HEAD_DIM), BF),
        compiler_params=_params(("arbitrary", "arbitrary", "arbitrary")),
        name="swa_attn" if band else "swa_attn_ctx",
    )(*args)


def _post_norm_residual(src_scr, x_ref, gt_ref, g_ref, o_ref, bm):
    gt, g = gt_ref[...], g_ref[...]

    def rows_step(r, carry):
        rows = pl.ds(pl.multiple_of(r * NORM_CHUNK, NORM_CHUNK), NORM_CHUNK)
        f = src_scr[rows, :]
        ms = jnp.mean(f * f, axis=-1, keepdims=True)
        o_ref[rows, :] = x_ref[rows, :] + gt * (f * lax.rsqrt(ms + EPS) * g)
        return carry

    lax.fori_loop(0, bm // NORM_CHUNK, rows_step, 0)


def _outproj_kernel(od_ref, os_ref, wt_ref, wb_ref, x_ref, gt_ref, g_ref, o_ref, mix_scr, *, bm):
    mix_scr[...] = (jnp.dot(od_ref[...], wt_ref[...], preferred_element_type=F32)
                    + jnp.dot(os_ref[...], wb_ref[...], preferred_element_type=F32))
    _post_norm_residual(mix_scr, x_ref, gt_ref, g_ref, o_ref, bm)


def _outproj(od, osw, w_out, x, mods, layer, row_fn, g_post, bm):
    rows, d = x.shape
    half = od.shape[1]
    return pl.pallas_call(
        functools.partial(_outproj_kernel, bm=bm),
        grid=(rows // bm,),
        in_specs=[
            pl.BlockSpec((bm, half), lambda i: (i, 0)),
            pl.BlockSpec((bm, half), lambda i: (i, 0)),
            _resident((half, d), lambda i: (0, 0)),
            _resident((half, d), lambda i: (1, 0)),
            pl.BlockSpec((bm, d), lambda i: (i, 0)),
            _mod_spec(layer, 2, row_fn, 1)(d),
            pl.BlockSpec((1, d), lambda i: (0, 0)),
        ],
        out_specs=pl.BlockSpec((bm, d), lambda i: (i, 0)),
        out_shape=jax.ShapeDtypeStruct((rows, d), F32),
        scratch_shapes=[pltpu.VMEM((bm, d), F32)],
        compiler_params=_params(("arbitrary",)),
        name="outproj",
    )(od, osw, w_out, w_out, x, mods, g_post)


HALO = 16
CONV_CHUNK = 128


def _ffn_up_kernel(x_ref, xp_ref, xn_ref, sh_ref, sc_ref, g_ref, wg_ref, wu_ref, cw_ref, cb_ref, o_ref,
                   h_scr, g_scr, u_scr, *, bm, seq):
    i = pl.program_id(0)
    j = pl.program_id(1)

    @pl.when(j == 0)
    def _():
        g, sh, sc = g_ref[...], sh_ref[...], sc_ref[...]

        def norm_rows(r, carry):
            rows = pl.multiple_of(r * NORM_CHUNK, NORM_CHUNK)
            h_scr[pl.ds(HALO + rows, NORM_CHUNK), :] = _rms_mod(x_ref[pl.ds(rows, NORM_CHUNK), :], g, sh, sc).astype(BF)
            return carry

        lax.fori_loop(0, bm // NORM_CHUNK, norm_rows, 0)
        prev_ok = ((i * bm) % seq != 0).astype(F32)
        next_ok = (((i + 1) * bm) % seq != 0).astype(F32)
        h_scr[0:HALO, :] = (_rms_mod(xp_ref[...], g, sh, sc) * prev_ok).astype(BF)
        h_scr[HALO + bm:2 * HALO + bm, :] = (_rms_mod(xn_ref[...], g, sh, sc) * next_ok).astype(BF)

    g_scr[...] = jnp.dot(h_scr[...], wg_ref[...], preferred_element_type=F32)
    u_scr[...] = jnp.dot(h_scr[HALO:HALO + bm, :], wu_ref[...], preferred_element_type=F32)
    cw0, cw1, cw2, cb = cw_ref[0:1, :], cw_ref[1:2, :], cw_ref[2:3, :], cb_ref[...]
    win_rows = CONV_CHUNK + 2 * HALO

    def conv_rows(r, carry):
        rows = pl.multiple_of(r * CONV_CHUNK, CONV_CHUNK)
        win = g_scr[pl.ds(rows, win_rows), :]
        below = pltpu.roll(win, 1, axis=0)[HALO:HALO + CONV_CHUNK]
        above = pltpu.roll(win, win_rows - 1, axis=0)[HALO:HALO + CONV_CHUNK]
        gc = below * cw0 + win[HALO:HALO + CONV_CHUNK] * cw1 + above * cw2 + cb
        a = gc * _sigmoid(gc) * u_scr[pl.ds(rows, CONV_CHUNK), :]
        o_ref[pl.ds(rows, CONV_CHUNK), :] = a.astype(BF)
        return carry

    lax.fori_loop(0, bm // CONV_CHUNK, conv_rows, 0)


def _ffn_up(x, mods, layer, row_fn, g_pre, wg, wu, cw, cb, seq, bm, bn):
    rows, d = x.shape
    dff = wg.shape[1]
    hb = bm // HALO
    nh = rows // HALO
    mod = lambda k: _mod_spec(layer, k, row_fn, 2)(d)
    return pl.pallas_call(
        functools.partial(_ffn_up_kernel, bm=bm, seq=seq),
        grid=(rows // bm, dff // bn),
        in_specs=[
            pl.BlockSpec((bm, d), lambda i, j: (i, 0)),
            pl.BlockSpec((HALO, d), lambda i, j: (jnp.maximum(i * hb - 1, 0), 0)),
            pl.BlockSpec((HALO, d), lambda i, j: (jnp.minimum((i + 1) * hb, nh - 1), 0)),
            mod(3), mod(4),
            pl.BlockSpec((1, d), lambda i, j: (0, 0)),
            pl.BlockSpec((d, bn), lambda i, j: (0, j)),
            pl.BlockSpec((d, bn), lambda i, j: (0, j)),
            pl.BlockSpec((3, bn), lambda i, j: (0, j)),
            pl.BlockSpec((1, bn), lambda i, j: (0, j)),
        ],
        out_specs=pl.BlockSpec((bm, bn), lambda i, j: (i, j)),
        out_shape=jax.ShapeDtypeStruct((rows, dff), BF),
        scratch_shapes=[pltpu.VMEM((bm + 2 * HALO, d), BF),
                        pltpu.VMEM((bm + 2 * HALO, bn), F32),
                        pltpu.VMEM((bm, bn), F32)],
        compiler_params=_params(("arbitrary", "arbitrary")),
        name="ffn_up",
    )(x, x, x, mods, mods, g_pre, wg, wu, cw, cb)


def _ffn_down_kernel(a_ref, w_ref, x_ref, gt_ref, g_ref, o_ref, f_scr, *, bm):
    f_scr[...] = jnp.dot(a_ref[...], w_ref[...], preferred_element_type=F32)
    _post_norm_residual(f_scr, x_ref, gt_ref, g_ref, o_ref, bm)


def _ffn_down(a, wd, x, mods, layer, row_fn, g_post, bm):
    rows, d = x.shape
    dff = a.shape[1]
    return pl.pallas_call(
        functools.partial(_ffn_down_kernel, bm=bm),
        grid=(rows // bm,),
        in_specs=[
            pl.BlockSpec((bm, dff), lambda i: (i, 0)),
            _resident((dff, d), lambda i: (0, 0)),
            pl.BlockSpec((bm, d), lambda i: (i, 0)),
            _mod_spec(layer, 5, row_fn, 1)(d),
            pl.BlockSpec((1, d), lambda i: (0, 0)),
        ],
        out_specs=pl.BlockSpec((bm, d), lambda i: (i, 0)),
        out_shape=jax.ShapeDtypeStruct((rows, d), F32),
        scratch_shapes=[pltpu.VMEM((bm, d), F32)],
        compiler_params=_params(("arbitrary",)),
        name="ffn_down",
    )(a, wd, x, mods, g_post)


def _rope_tables(seq, ctx_len):
    t = np.arange(seq)
    row, col = t // GRID_W, t % GRID_W

    def table(half):
        inv = ROPE_THETA ** (-np.arange(half, dtype=np.float64) / half)
        ar, ac = row[:, None] * inv[None, :], col[:, None] * inv[None, :]
        cos = np.concatenate([np.cos(ar), np.cos(ar), np.cos(ac), np.cos(ac)], axis=1)
        sin = np.concatenate([-np.sin(ar), np.sin(ar), -np.sin(ac), np.sin(ac)], axis=1)
        reps = SLAB // cos.shape[1]
        return np.tile(cos, (1, reps)).astype(np.float32), np.tile(sin, (1, reps)).astype(np.float32)

    cd, sd = table(DIFF_QK_DIM // 4)
    cs, ss = table(SWA_HEAD_DIM // 4)
    lat = tuple(jnp.asarray(a) for a in (cd, sd, cs, ss))
    one, zero = jnp.ones((ctx_len, SLAB), F32), jnp.zeros((ctx_len, SLAB), F32)
    return lat, (one, zero, one, zero)


def _pick(n, prefs):
    for p in prefs:
        if n % p == 0:
            return p
    raise ValueError(f"no block size in {prefs} divides {n}")


def kernel(x, c, ctx, c_ctx, w_ada, b_ada, g_pre_mix, g_post_mix, w_in, diff_lambda, diff_subln, swa_sink,
           w_out, g_pre_ffn, g_post_ffn, w_ffn_gate, w_ffn_up, ffn_conv_w, ffn_conv_b, w_ffn_down):
    b, s, d = x.shape
    cl = ctx.shape[1]
    depth = w_ada.shape[0]
    dff = w_ffn_gate.shape[2]
    assert w_in.shape[2] == IN_W and s % WINDOW == 0 and cl % WINDOW == 0

    n_rows = -(-(b + 1) // 8) * 8
    cvec = jnp.concatenate([c, c_ctx[None, :], jnp.zeros((n_rows - b - 1, d), F32)], axis=0)
    mods = _ada(cvec, w_ada, b_ada).reshape(depth, n_rows, 6, 1, d)

    lat_tabs, ctx_tabs = _rope_tables(s, cl)
    bm_in = _pick(s, (512, 256, 128))
    bm_out = _pick(s, (512, 256, 128))
    bm_up = _pick(s, (1024, 512, 256, 128))
    bm_dn = _pick(s, (256, 128))
    bm_c = _pick(cl, (256, 128))
    bn_ff = _pick(dff, (512, 256, 128))
    bq = _pick(s, (256, 128))
    bk = _pick(s, (512, 256, 128))
    bq_c = _pick(cl, (256, 128))

    xl = x.reshape(b * s, d)
    xc = ctx.reshape(b * cl, d)
    ctx_row = lambda i: b

    for l in range(depth):
        last = l == depth - 1
        lam_init = 0.8 - 0.6 * math.exp(-0.3 * l)
        w_in_l = w_in[l].astype(BF)
        w_out_l = w_out[l].astype(BF)
        wg_l, wu_l, wd_l = w_ffn_gate[l].astype(BF), w_ffn_up[l].astype(BF), w_ffn_down[l].astype(BF)
        gpm, gqm = g_pre_mix[l][None, :], g_post_mix[l][None, :]
        gpf, gqf = g_pre_ffn[l][None, :], g_post_ffn[l][None, :]
        sub = diff_subln[l][None, :]
        cb = ffn_conv_b[l][None, :]

        p_lat = _inproj(xl, mods, l, lambda i, n=s // bm_in: i // n, gpm, w_in_l, lat_tabs, s, bm_in)
        p_ctx = _inproj(xc, mods, l, ctx_row, gpm, w_in_l, ctx_tabs, cl, bm_c)
        p_lat3, p_ctx3 = p_lat.reshape(b, s, IN_W), p_ctx.reshape(b, cl, IN_W)
        od = _diff_attention(p_lat3, p_lat3, p_ctx3, diff_lambda[l], sub, lam_init, bq, bk)
        osw = _swa_attention(p_lat3, p_ctx3, swa_sink[l], band=True)
        xl = _outproj(od.reshape(b * s, -1), osw.reshape(b * s, -1), w_out_l, xl, mods, l,
                      lambda i, n=s // bm_out: i // n, gqm, bm_out)
        if not last:
            odc = _diff_attention(p_ctx3, None, p_ctx3, diff_lambda[l], sub, lam_init, bq_c, bk)
            oswc = _swa_attention(p_ctx3, p_ctx3, swa_sink[l], band=False)
            xc = _outproj(odc.reshape(b * cl, -1), oswc.reshape(b * cl, -1), w_out_l, xc, mods, l, ctx_row, gqm, bm_c)

        a = _ffn_up(xl, mods, l, lambda i, n=s // bm_up: i // n, gpf, wg_l, wu_l, ffn_conv_w[l], cb, s, bm_up, bn_ff)
        xl = _ffn_down(a, wd_l, xl, mods, l, lambda i, n=s // bm_dn: i // n, gqf, bm_dn)
        if not last:
            ac = _ffn_up(xc, mods, l, ctx_row, gpf, wg_l, wu_l, ffn_conv_w[l], cb, cl, bm_c, bn_ff)
            xc = _ffn_down(ac, wd_l, xc, mods, l, ctx_row, gqf, bm_c)
    return xl.reshape(b, s, d)
```

```python
import functools
import math

import jax
import jax.numpy as jnp
import numpy as np
from jax import lax
from jax.experimental import pallas as pl
from jax.experimental.pallas import tpu as pltpu

BF = jnp.bfloat16
F32 = jnp.float32

GRID_W = 64
DIFF_HEADS = 8
DIFF_QK_DIM = 64
DIFF_V_DIM = 128
SWA_HEADS = 8
SWA_KV_HEADS = 2
SWA_GROUP = SWA_HEADS // SWA_KV_HEADS
SWA_HEAD_DIM = 128
WINDOW = 128
ROPE_THETA = 10000.0
EPS = 1e-6
NEG_INF = -1e30
LOG2E = 1.4426950408889634
DIFF_QSCALE = DIFF_QK_DIM ** -0.5 * LOG2E
SWA_QSCALE = SWA_HEAD_DIM ** -0.5 * LOG2E

SLAB = 128
DQ0, DK0, DV0, SQ0, SK0, SV0 = 0, 8, 16, 24, 32, 34
N_SLABS = 36
IN_W = N_SLABS * SLAB

VMEM_LIMIT = 56 * 1024 * 1024
NORM_CHUNK = 64


def _params(sem, vmem=VMEM_LIMIT):
    return pltpu.CompilerParams(dimension_semantics=sem, vmem_limit_bytes=vmem)


def _resident(shape, index_map):
    return pl.BlockSpec(shape, index_map, pipeline_mode=pl.Buffered(1))


def _sigmoid(v):
    return 1.0 / (1.0 + jnp.exp(-v))


def _rms_mod(xf, g, sh, sc):
    ms = jnp.mean(xf * xf, axis=-1, keepdims=True)
    return (xf * lax.rsqrt(ms + EPS) * g) * (1.0 + sc) + sh


def _rope(x, cos, sin, shift):
    lane = lax.broadcasted_iota(jnp.int32, x.shape, 1)
    fwd = pltpu.roll(x, SLAB - shift, axis=1)
    bwd = pltpu.roll(x, shift, axis=1)
    partner = jnp.where((lane & (2 * shift - 1)) < shift, fwd, bwd)
    return x * cos + partner * sin


def _ada_kernel(c_ref, w_ref, b_ref, o_ref):
    cv = c_ref[...]
    s = (cv * _sigmoid(cv)).astype(BF)
    o_ref[...] = jnp.dot(s, w_ref[...].astype(BF), preferred_element_type=F32) + b_ref[...]


def _ada(cvec, w_ada, b_ada):
    depth, d, n = w_ada.shape
    rows = cvec.shape[0]
    bn = 1024
    return pl.pallas_call(
        _ada_kernel,
        grid=(depth, n // bn),
        in_specs=[
            pl.BlockSpec((rows, d), lambda l, j: (0, 0)),
            pl.BlockSpec((None, d, bn), lambda l, j: (l, 0, j)),
            pl.BlockSpec((None, 1, bn), lambda l, j: (l, 0, j)),
        ],
        out_specs=pl.BlockSpec((None, rows, bn), lambda l, j: (l, 0, j)),
        out_shape=jax.ShapeDtypeStruct((depth, rows, n), F32),
        compiler_params=_params(("arbitrary", "arbitrary")),
        name="ada",
    )(cvec, w_ada, b_ada.reshape(depth, 1, n))


def _mod_spec(layer, k, row_fn, nargs):
    if nargs == 1:
        return lambda d: pl.BlockSpec((None, None, None, 1, d), lambda i: (layer, row_fn(i), k, 0, 0))
    return lambda d: pl.BlockSpec((None, None, None, 1, d), lambda i, j: (layer, row_fn(i), k, 0, 0))


def _inproj_kernel(x_ref, sh_ref, sc_ref, g_ref, w_ref, cd_ref, sd_ref, cs_ref, ss_ref, o_ref, h_scr, *, bm):
    g, sh, sc = g_ref[...], sh_ref[...], sc_ref[...]

    def norm_rows(r, carry):
        rows = pl.ds(pl.multiple_of(r * NORM_CHUNK, NORM_CHUNK), NORM_CHUNK)
        h_scr[rows, :] = _rms_mod(x_ref[rows, :], g, sh, sc).astype(BF)
        return carry

    lax.fori_loop(0, bm // NORM_CHUNK, norm_rows, 0)

    group = 4
    for grp in range(N_SLABS // group):
        acc = jnp.dot(h_scr[...], w_ref[:, grp * group * SLAB:(grp + 1) * group * SLAB],
                      preferred_element_type=F32)
        for t in range(group):
            slab = grp * group + t
            a = acc[:, t * SLAB:(t + 1) * SLAB]
            if slab < DV0:
                a = _rope(a, cd_ref[...], sd_ref[...], DIFF_QK_DIM // 4)
                if slab < DK0:
                    a = a * DIFF_QSCALE
            elif SQ0 <= slab < SV0:
                a = _rope(a, cs_ref[...], ss_ref[...], SWA_HEAD_DIM // 4)
                if slab < SK0:
                    a = a * SWA_QSCALE
            o_ref[:, slab * SLAB:(slab + 1) * SLAB] = a.astype(BF)


def _inproj(x, mods, layer, row_fn, g_pre, w, tables, seq, bm):
    rows, d = x.shape
    nt = seq // bm
    mod = lambda k: _mod_spec(layer, k, row_fn, 1)(d)
    tab = pl.BlockSpec((bm, SLAB), lambda i: (i % nt, 0))
    return pl.pallas_call(
        functools.partial(_inproj_kernel, bm=bm),
        grid=(rows // bm,),
        in_specs=[
            pl.BlockSpec((bm, d), lambda i: (i, 0)),
            mod(0), mod(1),
            pl.BlockSpec((1, d), lambda i: (0, 0)),
            _resident((d, IN_W), lambda i: (0, 0)),
            tab, tab, tab, tab,
        ],
        out_specs=pl.BlockSpec((bm, IN_W), lambda i: (i, 0)),
        out_shape=jax.ShapeDtypeStruct((rows, IN_W), BF),
        scratch_shapes=[pltpu.VMEM((bm, d), BF)],
        compiler_params=_params(("arbitrary",)),
        name="inproj",
    )(x, mods, mods, g_pre, w, *tables)


ONES_ROWS = 16
DIFF_UNROLL = 16


def _diff_kernel(lam_ref, sub_ref, q_ref, k_ref, v_ref, o_ref, *, n_chunks, lam_init, unroll):
    lp = lam_ref[...]
    lam = (jnp.exp(jnp.sum(lp[0:1] * lp[1:2], axis=-1, keepdims=True))
           - jnp.exp(jnp.sum(lp[2:3] * lp[3:4], axis=-1, keepdims=True)) + lam_init)

    q = q_ref[...].astype(F32)
    bq = q.shape[0]
    lane = lax.broadcasted_iota(jnp.int32, q.shape, 1)
    q_both = jnp.concatenate([jnp.where(lane < DIFF_QK_DIM, q, 0.0), jnp.where(lane >= DIFF_QK_DIM, q, 0.0)], axis=0)
    q_t = jnp.transpose(q_both).astype(BF)

    def scores(k_chunk):
        return jnp.dot(k_chunk, q_t, preferred_element_type=F32)

    def absorb(s, v_t, m, acc):
        mn = jnp.maximum(m, jnp.max(s, axis=0, keepdims=True))
        alpha = jnp.exp2(m - mn)
        p = jnp.exp2(s - mn).astype(BF)
        v_ext = jnp.concatenate([v_t, jnp.ones((ONES_ROWS, v_t.shape[1]), BF)], axis=0)
        return mn, alpha * acc + jnp.dot(v_ext, p, preferred_element_type=F32)

    m = jnp.full((1, 2 * bq), NEG_INF, F32)
    acc = jnp.zeros((DIFF_V_DIM + ONES_ROWS, 2 * bq), F32)

    def chunk_step(c, carry):
        s, m, acc = carry
        s_next = scores(k_ref[c + 1])
        m, acc = absorb(s, v_ref[c], m, acc)
        return s_next, m, acc

    def body(i, carry):
        for t in range(unroll):
            carry = chunk_step(i * unroll + t, carry)
        return carry

    trips = (n_chunks - 1) // unroll
    carry = lax.fori_loop(0, trips, body, (scores(k_ref[0]), m, acc))
    for c in range(trips * unroll, n_chunks - 1):
        carry = chunk_step(c, carry)
    s, m, acc = carry
    m, acc = absorb(s, v_ref[n_chunks - 1], m, acc)

    o_t = acc[:DIFF_V_DIM] / acc[DIFF_V_DIM:DIFF_V_DIM + 1]
    o = jnp.transpose(o_t[:, :bq]) - lam * jnp.transpose(o_t[:, bq:])
    ms = jnp.mean(o * o, axis=-1, keepdims=True)
    o_ref[...] = (o * lax.rsqrt(ms + EPS) * sub_ref[...] * (1.0 - lam_init)).astype(BF)


def _diff_attention(p_q, p_lat, p_ctx, lam_par, subln, lam_init, bq, bk):
    b, sq, _ = p_q.shape
    kw, vw = DIFF_HEADS * 2 * DIFF_QK_DIM, DIFF_HEADS * DIFF_V_DIM
    srcs = [p_ctx] if p_lat is None else [p_lat, p_ctx]
    k_all = jnp.concatenate([p[:, :, DK0 * SLAB:DK0 * SLAB + kw] for p in srcs], axis=1)
    v_all = jnp.concatenate([p[:, :, DV0 * SLAB:DV0 * SLAB + vw] for p in srcs], axis=1)
    n_chunks = k_all.shape[1] // bk
    k_all = k_all.reshape(b, n_chunks, bk, kw)
    v_all = jnp.transpose(v_all.reshape(b, n_chunks, bk, DIFF_HEADS, DIFF_V_DIM), (0, 3, 1, 4, 2))
    return pl.pallas_call(
        functools.partial(_diff_kernel, n_chunks=n_chunks, lam_init=lam_init, unroll=DIFF_UNROLL),
        grid=(b, DIFF_HEADS, sq // bq),
        in_specs=[
            pl.BlockSpec((4, DIFF_QK_DIM), lambda bi, h, qi: (0, 0)),
            pl.BlockSpec((1, DIFF_V_DIM), lambda bi, h, qi: (0, 0)),
            pl.BlockSpec((None, bq, SLAB), lambda bi, h, qi: (bi, qi, DQ0 + h)),
            pl.BlockSpec((None, n_chunks, bk, SLAB), lambda bi, h, qi: (bi, 0, 0, h)),
            pl.BlockSpec((None, None, n_chunks, DIFF_V_DIM, bk), lambda bi, h, qi: (bi, h, 0, 0, 0)),
        ],
        out_specs=pl.BlockSpec((None, bq, SLAB), lambda bi, h, qi: (bi, qi, h)),
        out_shape=jax.ShapeDtypeStruct((b, sq, vw), BF),
        compiler_params=_params(("arbitrary", "arbitrary", "arbitrary")),
        name="diff_attn" if p_lat is not None else "diff_attn_ctx",
    )(lam_par, subln, p_q, k_all, v_all)


def _swa_kernel(*refs, band, seq):
    if band:
        sink_ref, q_ref, kp_ref, kz_ref, kn_ref, vp_ref, vz_ref, vn_ref, kc_ref, vc_ref, o_ref = refs
    else:
        sink_ref, q_ref, kc_ref, vc_ref, o_ref = refs
    g = pl.program_id(1)
    n = pl.program_id(2)
    blk = q_ref.shape[0]
    q = q_ref[...]
    q4 = jnp.concatenate([q[:, h * SLAB:(h + 1) * SLAB] for h in range(SWA_GROUP)], axis=0)
    if band:
        kcat = jnp.concatenate([kp_ref[...], kz_ref[...], kn_ref[...], kc_ref[...]], axis=0)
        vcat = jnp.concatenate([vp_ref[...], vz_ref[...], vn_ref[...], vc_ref[...]], axis=0)
    else:
        kcat, vcat = kc_ref[...], vc_ref[...]
    s = lax.dot_general(q4, kcat, (((1,), (1,)), ((), ())), preferred_element_type=F32)
    row = lax.broadcasted_iota(jnp.int32, s.shape, 0)
    if band:
        colx = lax.broadcasted_iota(jnp.int32, s.shape, 1)
        qpos = n * blk + (row & (blk - 1))
        kpos = (n - 1) * blk + colx
        valid = (colx >= 3 * blk) | ((jnp.abs(qpos - kpos) <= WINDOW) & (kpos >= 0) & (kpos < seq))
        s = jnp.where(valid, s, NEG_INF)
    rowc = lax.broadcasted_iota(jnp.int32, (s.shape[0], 1), 0)
    sk = jnp.full((s.shape[0], 1), sink_ref[g * SWA_GROUP + SWA_GROUP - 1], F32)
    for h in range(SWA_GROUP - 2, -1, -1):
        sk = jnp.where(rowc < (h + 1) * blk, sink_ref[g * SWA_GROUP + h], sk)
    sk = sk * LOG2E
    m = jnp.maximum(jnp.max(s, axis=-1, keepdims=True), sk)
    p = jnp.exp2(s - m)
    l = jnp.sum(p, axis=-1, keepdims=True) + jnp.exp2(sk - m)
    o = jnp.dot(p.astype(BF), vcat, preferred_element_type=F32) / l
    for h in range(SWA_GROUP):
        o_ref[:, h * SLAB:(h + 1) * SLAB] = o[h * blk:(h + 1) * blk, :].astype(BF)


def _swa_attention(p_q, p_ctx, sink, band):
    b, sq, _ = p_q.shape
    c = p_ctx.shape[1]
    blk = WINDOW
    nb = sq // blk
    gw = SWA_GROUP * SLAB
    in_specs = [
        pl.BlockSpec(memory_space=pltpu.SMEM),
        pl.BlockSpec((None, blk, gw), lambda bi, g, n: (bi, n, SQ0 * SLAB // gw + g)),
    ]
    args = [sink, p_q]
    if band:
        for base in (SK0, SV0):
            in_specs += [
                pl.BlockSpec((None, blk, SLAB), lambda bi, g, n, base=base: (bi, jnp.maximum(n - 1, 0), base + g)),
                pl.BlockSpec((None, blk, SLAB), lambda bi, g, n, base=base: (bi, n, base + g)),
                pl.BlockSpec((None, blk, SLAB), lambda bi, g, n, base=base: (bi, jnp.minimum(n + 1, nb - 1), base + g)),
            ]
            args += [p_q, p_q, p_q]
    in_specs += [pl.BlockSpec((None, c, SLAB), lambda bi, g, n: (bi, 0, SK0 + g)),
                 pl.BlockSpec((None, c, SLAB), lambda bi, g, n: (bi, 0, SV0 + g))]
    args += [p_ctx, p_ctx]
    return pl.pallas_call(
        functools.partial(_swa_kernel, band=band, seq=sq),
        grid=(b, SWA_KV_HEADS, nb),
        in_specs=in_specs,
        out_specs=pl.BlockSpec((None, blk, gw), lambda bi, g, n: (bi, n, g)),
        out_shape=jax.ShapeDtypeStruct((b, sq, SWA_HEADS * SWA_HEAD_DIM), BF),
        compiler_params=_params(("arbitrary", "arbitrary", "arbitrary")),
        name="swa_attn" if band else "swa_attn_ctx",
    )(*args)


def _post_norm_residual(src_scr, x_ref, gt_ref, g_ref, o_ref, bm):
    gt, g = gt_ref[...], g_ref[...]

    def rows_step(r, carry):
        rows = pl.ds(pl.multiple_of(r * NORM_CHUNK, NORM_CHUNK), NORM_CHUNK)
        f = src_scr[rows, :]
        ms = jnp.mean(f * f, axis=-1, keepdims=True)
        o_ref[rows, :] = x_ref[rows, :] + gt * (f * lax.rsqrt(ms + EPS) * g)
        return carry

    lax.fori_loop(0, bm // NORM_CHUNK, rows_step, 0)


def _outproj_kernel(od_ref, os_ref, wt_ref, wb_ref, x_ref, gt_ref, g_ref, o_ref, mix_scr, *, bm):
    mix_scr[...] = (jnp.dot(od_ref[...], wt_ref[...], preferred_element_type=F32)
                    + jnp.dot(os_ref[...], wb_ref[...], preferred_element_type=F32))
    _post_norm_residual(mix_scr, x_ref, gt_ref, g_ref, o_ref, bm)


def _outproj(od, osw, w_out, x, mods, layer, row_fn, g_post, bm):
    rows, d = x.shape
    half = od.shape[1]
    return pl.pallas_call(
        functools.partial(_outproj_kernel, bm=bm),
        grid=(rows // bm,),
        in_specs=[
            pl.BlockSpec((bm, half), lambda i: (i, 0)),
            pl.BlockSpec((bm, half), lambda i: (i, 0)),
            _resident((half, d), lambda i: (0, 0)),
            _resident((half, d), lambda i: (1, 0)),
            pl.BlockSpec((bm, d), lambda i: (i, 0)),
            _mod_spec(layer, 2, row_fn, 1)(d),
            pl.BlockSpec((1, d), lambda i: (0, 0)),
        ],
        out_specs=pl.BlockSpec((bm, d), lambda i: (i, 0)),
        out_shape=jax.ShapeDtypeStruct((rows, d), F32),
        scratch_shapes=[pltpu.VMEM((bm, d), F32)],
        compiler_params=_params(("arbitrary",)),
        name="outproj",
    )(od, osw, w_out, w_out, x, mods, g_post)


HALO = 16
CONV_CHUNK = 128


def _ffn_up_kernel(x_ref, xp_ref, xn_ref, sh_ref, sc_ref, g_ref, wg_ref, wu_ref, cw_ref, cb_ref, o_ref,
                   h_scr, g_scr, u_scr, *, bm, seq):
    i = pl.program_id(0)
    j = pl.program_id(1)

    @pl.when(j == 0)
    def _():
        g, sh, sc = g_ref[...], sh_ref[...], sc_ref[...]

        def norm_rows(r, carry):
            rows = pl.multiple_of(r * NORM_CHUNK, NORM_CHUNK)
            h_scr[pl.ds(HALO + rows, NORM_CHUNK), :] = _rms_mod(x_ref[pl.ds(rows, NORM_CHUNK), :], g, sh, sc).astype(BF)
            return carry

        lax.fori_loop(0, bm // NORM_CHUNK, norm_rows, 0)
        prev_ok = ((i * bm) % seq != 0).astype(F32)
        next_ok = (((i + 1) * bm) % seq != 0).astype(F32)
        h_scr[0:HALO, :] = (_rms_mod(xp_ref[...], g, sh, sc) * prev_ok).astype(BF)
        h_scr[HALO + bm:2 * HALO + bm, :] = (_rms_mod(xn_ref[...], g, sh, sc) * next_ok).astype(BF)

    g_scr[...] = jnp.dot(h_scr[...], wg_ref[...], preferred_element_type=F32)
    u_scr[...] = jnp.dot(h_scr[HALO:HALO + bm, :], wu_ref[...], preferred_element_type=F32)
    cw0, cw1, cw2, cb = cw_ref[0:1, :], cw_ref[1:2, :], cw_ref[2:3, :], cb_ref[...]
    win_rows = CONV_CHUNK + 2 * HALO

    def conv_rows(r, carry):
        rows = pl.multiple_of(r * CONV_CHUNK, CONV_CHUNK)
        win = g_scr[pl.ds(rows, win_rows), :]
        below = pltpu.roll(win, 1, axis=0)[HALO:HALO + CONV_CHUNK]
        above = pltpu.roll(win, win_rows - 1, axis=0)[HALO:HALO + CONV_CHUNK]
        gc = below * cw0 + win[HALO:HALO + CONV_CHUNK] * cw1 + above * cw2 + cb
        a = gc * _sigmoid(gc) * u_scr[pl.ds(rows, CONV_CHUNK), :]
        o_ref[pl.ds(rows, CONV_CHUNK), :] = a.astype(BF)
        return carry

    lax.fori_loop(0, bm // CONV_CHUNK, conv_rows, 0)


def _ffn_up(x, mods, layer, row_fn, g_pre, wg, wu, cw, cb, seq, bm, bn):
    rows, d = x.shape
    dff = wg.shape[1]
    hb = bm // HALO
    nh = rows // HALO
    mod = lambda k: _mod_spec(layer, k, row_fn, 2)(d)
    return pl.pallas_call(
        functools.partial(_ffn_up_kernel, bm=bm, seq=seq),
        grid=(rows // bm, dff // bn),
        in_specs=[
            pl.BlockSpec((bm, d), lambda i, j: (i, 0)),
            pl.BlockSpec((HALO, d), lambda i, j: (jnp.maximum(i * hb - 1, 0), 0)),
            pl.BlockSpec((HALO, d), lambda i, j: (jnp.minimum((i + 1) * hb, nh - 1), 0)),
            mod(3), mod(4),
            pl.BlockSpec((1, d), lambda i, j: (0, 0)),
            pl.BlockSpec((d, bn), lambda i, j: (0, j)),
            pl.BlockSpec((d, bn), lambda i, j: (0, j)),
            pl.BlockSpec((3, bn), lambda i, j: (0, j)),
            pl.BlockSpec((1, bn), lambda i, j: (0, j)),
        ],
        out_specs=pl.BlockSpec((bm, bn), lambda i, j: (i, j)),
        out_shape=jax.ShapeDtypeStruct((rows, dff), BF),
        scratch_shapes=[pltpu.VMEM((bm + 2 * HALO, d), BF),
                        pltpu.VMEM((bm + 2 * HALO, bn), F32),
                        pltpu.VMEM((bm, bn), F32)],
        compiler_params=_params(("arbitrary", "arbitrary")),
        name="ffn_up",
    )(x, x, x, mods, mods, g_pre, wg, wu, cw, cb)


def _ffn_down_kernel(a_ref, w_ref, x_ref, gt_ref, g_ref, o_ref, f_scr, *, bm):
    f_scr[...] = jnp.dot(a_ref[...], w_ref[...], preferred_element_type=F32)
    _post_norm_residual(f_scr, x_ref, gt_ref, g_ref, o_ref, bm)


def _ffn_down(a, wd, x, mods, layer, row_fn, g_post, bm):
    rows, d = x.shape
    dff = a.shape[1]
    return pl.pallas_call(
        functools.partial(_ffn_down_kernel, bm=bm),
        grid=(rows // bm,),
        in_specs=[
            pl.BlockSpec((bm, dff), lambda i: (i, 0)),
            _resident((dff, d), lambda i: (0, 0)),
            pl.BlockSpec((bm, d), lambda i: (i, 0)),
            _mod_spec(layer, 5, row_fn, 1)(d),
            pl.BlockSpec((1, d), lambda i: (0, 0)),
        ],
        out_specs=pl.BlockSpec((bm, d), lambda i: (i, 0)),
        out_shape=jax.ShapeDtypeStruct((rows, d), F32),
        scratch_shapes=[pltpu.VMEM((bm, d), F32)],
        compiler_params=_params(("arbitrary",)),
        name="ffn_down",
    )(a, wd, x, mods, g_post)


def _rope_tables(seq, ctx_len):
    t = np.arange(seq)
    row, col = t // GRID_W, t % GRID_W

    def table(half):
        inv = ROPE_THETA ** (-np.arange(half, dtype=np.float64) / half)
        ar, ac = row[:, None] * inv[None, :], col[:, None] * inv[None, :]
        cos = np.concatenate([np.cos(ar), np.cos(ar), np.cos(ac), np.cos(ac)], axis=1)
        sin = np.concatenate([-np.sin(ar), np.sin(ar), -np.sin(ac), np.sin(ac)], axis=1)
        reps = SLAB // cos.shape[1]
        return np.tile(cos, (1, reps)).astype(np.float32), np.tile(sin, (1, reps)).astype(np.float32)

    cd, sd = table(DIFF_QK_DIM // 4)
    cs, ss = table(SWA_HEAD_DIM // 4)
    lat = tuple(jnp.asarray(a) for a in (cd, sd, cs, ss))
    one, zero = jnp.ones((ctx_len, SLAB), F32), jnp.zeros((ctx_len, SLAB), F32)
    return lat, (one, zero, one, zero)


def _pick(n, prefs):
    for p in prefs:
        if n % p == 0:
            return p
    raise ValueError(f"no block size in {prefs} divides {n}")


def kernel(x, c, ctx, c_ctx, w_ada, b_ada, g_pre_mix, g_post_mix, w_in, diff_lambda, diff_subln, swa_sink,
           w_out, g_pre_ffn, g_post_ffn, w_ffn_gate, w_ffn_up, ffn_conv_w, ffn_conv_b, w_ffn_down):
    b, s, d = x.shape
    cl = ctx.shape[1]
    depth = w_ada.shape[0]
    dff = w_ffn_gate.shape[2]
    assert w_in.shape[2] == IN_W and s % WINDOW == 0 and cl % WINDOW == 0

    n_rows = -(-(b + 1) // 8) * 8
    cvec = jnp.concatenate([c, c_ctx[None, :], jnp.zeros((n_rows - b - 1, d), F32)], axis=0)
    mods = _ada(cvec, w_ada, b_ada).reshape(depth, n_rows, 6, 1, d)

    lat_tabs, ctx_tabs = _rope_tables(s, cl)
    bm_in = _pick(s, (512, 256, 128))
    bm_out = _pick(s, (512, 256, 128))
    bm_up = _pick(s, (1024, 512, 256, 128))
    bm_dn = _pick(s, (256, 128))
    bm_c = _pick(cl, (256, 128))
    bn_ff = _pick(dff, (512, 256, 128))
    bq = _pick(s, (256, 128))
    bk = _pick(math.gcd(s, cl), (256, 128))
    bq_c = _pick(cl, (256, 128))

    xl = x.reshape(b * s, d)
    xc = ctx.reshape(b * cl, d)
    ctx_row = lambda i: b

    for l in range(depth):
        last = l == depth - 1
        lam_init = 0.8 - 0.6 * math.exp(-0.3 * l)
        w_in_l = w_in[l].astype(BF)
        w_out_l = w_out[l].astype(BF)
        wg_l, wu_l, wd_l = w_ffn_gate[l].astype(BF), w_ffn_up[l].astype(BF), w_ffn_down[l].astype(BF)
        gpm, gqm = g_pre_mix[l][None, :], g_post_mix[l][None, :]
        gpf, gqf = g_pre_ffn[l][None, :], g_post_ffn[l][None, :]
        sub = diff_subln[l][None, :]
        cb = ffn_conv_b[l][None, :]

        p_lat = _inproj(xl, mods, l, lambda i, n=s // bm_in: i // n, gpm, w_in_l, lat_tabs, s, bm_in)
        p_ctx = _inproj(xc, mods, l, ctx_row, gpm, w_in_l, ctx_tabs, cl, bm_c)
        p_lat3, p_ctx3 = p_lat.reshape(b, s, IN_W), p_ctx.reshape(b, cl, IN_W)
        od = _diff_attention(p_lat3, p_lat3, p_ctx3, diff_lambda[l], sub, lam_init, bq, bk)
        osw = _swa_attention(p_lat3, p_ctx3, swa_sink[l], band=True)
        xl = _outproj(od.reshape(b * s, -1), osw.reshape(b * s, -1), w_out_l, xl, mods, l,
                      lambda i, n=s // bm_out: i // n, gqm, bm_out)
        if not last:
            odc = _diff_attention(p_ctx3, None, p_ctx3, diff_lambda[l], sub, lam_init, bq_c, bk)
            oswc = _swa_attention(p_ctx3, p_ctx3, swa_sink[l], band=False)
            xc = _outproj(odc.reshape(b * cl, -1), oswc.reshape(b * cl, -1), w_out_l, xc, mods, l, ctx_row, gqm, bm_c)

        a = _ffn_up(xl, mods, l, lambda i, n=s // bm_up: i // n, gpf, wg_l, wu_l, ffn_conv_w[l], cb, s, bm_up, bn_ff)
        xl = _ffn_down(a, wd_l, xl, mods, l, lambda i, n=s // bm_dn: i // n, gqf, bm_dn)
        if not last:
            ac = _ffn_up(xc, mods, l, ctx_row, gpf, wg_l, wu_l, ffn_conv_w[l], cb, cl, bm_c, bn_ff)
            xc = _ffn_down(ac, wd_l, xc, mods, l, ctx_row, gqf, bm_c)
    return xl.reshape(b, s, d)
```

```python
import functools
import math

import jax
import jax.numpy as jnp
import numpy as np
from jax import lax
from jax.experimental import pallas as pl
from jax.experimental.pallas import tpu as pltpu

BF = jnp.bfloat16
F32 = jnp.float32

GRID_W = 64
DIFF_HEADS = 8
DIFF_QK_DIM = 64
DIFF_V_DIM = 128
SWA_HEADS = 8
SWA_KV_HEADS = 2
SWA_GROUP = SWA_HEADS // SWA_KV_HEADS
SWA_HEAD_DIM = 128
WINDOW = 128
ROPE_THETA = 10000.0
EPS = 1e-6
NEG_INF = -1e30
LOG2E = 1.4426950408889634
DIFF_QSCALE = DIFF_QK_DIM ** -0.5 * LOG2E
SWA_QSCALE = SWA_HEAD_DIM ** -0.5 * LOG2E

SLAB = 128
DQ0, DK0, DV0, SQ0, SK0, SV0 = 0, 8, 16, 24, 32, 34
N_SLABS = 36
IN_W = N_SLABS * SLAB

VMEM_LIMIT = 56 * 1024 * 1024
NORM_CHUNK = 64


def _params(sem, vmem=VMEM_LIMIT, flags=None):
    return pltpu.CompilerParams(dimension_semantics=sem, vmem_limit_bytes=vmem, flags=flags)


def _resident(shape, index_map):
    return pl.BlockSpec(shape, index_map, pipeline_mode=pl.Buffered(1))


def _sigmoid(v):
    return 1.0 / (1.0 + jnp.exp(-v))


def _rms_mod(xf, g, sh, sc):
    ms = jnp.mean(xf * xf, axis=-1, keepdims=True)
    return (xf * lax.rsqrt(ms + EPS) * g) * (1.0 + sc) + sh


def _rope(x, cos, sin, shift):
    lane = lax.broadcasted_iota(jnp.int32, x.shape, 1)
    fwd = pltpu.roll(x, SLAB - shift, axis=1)
    bwd = pltpu.roll(x, shift, axis=1)
    partner = jnp.where((lane & (2 * shift - 1)) < shift, fwd, bwd)
    return x * cos + partner * sin


def _ada_kernel(c_ref, w_ref, b_ref, o_ref):
    cv = c_ref[...]
    s = (cv * _sigmoid(cv)).astype(BF)
    o_ref[...] = jnp.dot(s, w_ref[...].astype(BF), preferred_element_type=F32) + b_ref[...]


def _ada(cvec, w_ada, b_ada):
    depth, d, n = w_ada.shape
    rows = cvec.shape[0]
    bn = 1024
    return pl.pallas_call(
        _ada_kernel,
        grid=(depth, n // bn),
        in_specs=[
            pl.BlockSpec((rows, d), lambda l, j: (0, 0)),
            pl.BlockSpec((None, d, bn), lambda l, j: (l, 0, j)),
            pl.BlockSpec((None, 1, bn), lambda l, j: (l, 0, j)),
        ],
        out_specs=pl.BlockSpec((None, rows, bn), lambda l, j: (l, 0, j)),
        out_shape=jax.ShapeDtypeStruct((depth, rows, n), F32),
        compiler_params=_params(("arbitrary", "arbitrary")),
        name="ada",
    )(cvec, w_ada, b_ada.reshape(depth, 1, n))


def _mod_spec(layer, k, row_fn, nargs):
    if nargs == 1:
        return lambda d: pl.BlockSpec((None, None, None, 1, d), lambda i: (layer, row_fn(i), k, 0, 0))
    return lambda d: pl.BlockSpec((None, None, None, 1, d), lambda i, j: (layer, row_fn(i), k, 0, 0))


def _inproj_kernel(x_ref, sh_ref, sc_ref, g_ref, w_ref, cd_ref, sd_ref, cs_ref, ss_ref, o_ref, h_scr, *, bm):
    g, sh, sc = g_ref[...], sh_ref[...], sc_ref[...]

    def norm_rows(r, carry):
        rows = pl.ds(pl.multiple_of(r * NORM_CHUNK, NORM_CHUNK), NORM_CHUNK)
        h_scr[rows, :] = _rms_mod(x_ref[rows, :], g, sh, sc).astype(BF)
        return carry

    lax.fori_loop(0, bm // NORM_CHUNK, norm_rows, 0)

    group = 4
    for grp in range(N_SLABS // group):
        acc = jnp.dot(h_scr[...], w_ref[:, grp * group * SLAB:(grp + 1) * group * SLAB],
                      preferred_element_type=F32)
        for t in range(group):
            slab = grp * group + t
            a = acc[:, t * SLAB:(t + 1) * SLAB]
            if slab < DV0:
                a = _rope(a, cd_ref[...], sd_ref[...], DIFF_QK_DIM // 4)
                if slab < DK0:
                    a = a * DIFF_QSCALE
            elif SQ0 <= slab < SV0:
                a = _rope(a, cs_ref[...], ss_ref[...], SWA_HEAD_DIM // 4)
                if slab < SK0:
                    a = a * SWA_QSCALE
            o_ref[:, slab * SLAB:(slab + 1) * SLAB] = a.astype(BF)


def _inproj(x, mods, layer, row_fn, g_pre, w, tables, seq, bm):
    rows, d = x.shape
    nt = seq // bm
    mod = lambda k: _mod_spec(layer, k, row_fn, 1)(d)
    tab = pl.BlockSpec((bm, SLAB), lambda i: (i % nt, 0))
    return pl.pallas_call(
        functools.partial(_inproj_kernel, bm=bm),
        grid=(rows // bm,),
        in_specs=[
            pl.BlockSpec((bm, d), lambda i: (i, 0)),
            mod(0), mod(1),
            pl.BlockSpec((1, d), lambda i: (0, 0)),
            _resident((d, IN_W), lambda i: (0, 0)),
            tab, tab, tab, tab,
        ],
        out_specs=pl.BlockSpec((bm, IN_W), lambda i: (i, 0)),
        out_shape=jax.ShapeDtypeStruct((rows, IN_W), BF),
        scratch_shapes=[pltpu.VMEM((bm, d), BF)],
        compiler_params=_params(("arbitrary",)),
        name="inproj",
    )(x, mods, mods, g_pre, w, *tables)


ONES_ROWS = 16
DIFF_UNROLL = 21


def _diff_kernel(lam_ref, sub_ref, q_ref, k_ref, v_ref, o_ref, *, n_chunks, lam_init, unroll):
    lp = lam_ref[...]
    lam = (jnp.exp(jnp.sum(lp[0:1] * lp[1:2], axis=-1, keepdims=True))
           - jnp.exp(jnp.sum(lp[2:3] * lp[3:4], axis=-1, keepdims=True)) + lam_init)

    q = q_ref[...].astype(F32)
    bq = q.shape[0]
    lane = lax.broadcasted_iota(jnp.int32, q.shape, 1)
    q_both = jnp.concatenate([jnp.where(lane < DIFF_QK_DIM, q, 0.0), jnp.where(lane >= DIFF_QK_DIM, q, 0.0)], axis=0)
    q_t = jnp.transpose(q_both).astype(BF)

    def scores(k_chunk):
        s = jnp.dot(k_chunk, q_t, preferred_element_type=F32)
        return s, jnp.max(s, axis=0, keepdims=True)

    def softmax(sc, m):
        s, smax = sc
        mn = jnp.maximum(m, smax)
        return mn, jnp.exp2(m - mn), jnp.exp2(s - mn).astype(BF)

    def accumulate(v_t, p, alpha, acc):
        v_ext = jnp.concatenate([v_t, jnp.ones((ONES_ROWS, v_t.shape[1]), BF)], axis=0)
        return alpha * acc + jnp.dot(v_ext, p, preferred_element_type=F32)

    m = jnp.full((1, 2 * bq), NEG_INF, F32)
    acc = jnp.zeros((DIFF_V_DIM + ONES_ROWS, 2 * bq), F32)

    s = scores(k_ref[0])
    if n_chunks > 1:
        s_next = scores(k_ref[1])
        m, alpha, p = softmax(s, m)

        def step(t, carry):
            s, p, alpha, m, acc = carry
            s_new = scores(k_ref[t])
            m_new, alpha_new, p_new = softmax(s, m)
            acc = accumulate(v_ref[t - 2], p, alpha, acc)
            return s_new, p_new, alpha_new, m_new, acc

        def body(i, carry):
            for u in range(unroll):
                carry = step(2 + i * unroll + u, carry)
            return carry

        trips = (n_chunks - 2) // unroll
        carry = lax.fori_loop(0, trips, body, (s_next, p, alpha, m, acc))
        for t in range(2 + trips * unroll, n_chunks):
            carry = step(t, carry)
        s, p, alpha, m, acc = carry
        acc = accumulate(v_ref[n_chunks - 2], p, alpha, acc)
    m, alpha, p = softmax(s, m)
    acc = accumulate(v_ref[n_chunks - 1], p, alpha, acc)

    o_t = acc[:DIFF_V_DIM] / acc[DIFF_V_DIM:DIFF_V_DIM + 1]
    o = jnp.transpose(o_t[:, :bq]) - lam * jnp.transpose(o_t[:, bq:])
    ms = jnp.mean(o * o, axis=-1, keepdims=True)
    o_ref[...] = (o * lax.rsqrt(ms + EPS) * sub_ref[...] * (1.0 - lam_init)).astype(BF)


def _diff_attention(p_q, p_lat, p_ctx, lam_par, subln, lam_init, bq, bk):
    b, sq, _ = p_q.shape
    kw, vw = DIFF_HEADS * 2 * DIFF_QK_DIM, DIFF_HEADS * DIFF_V_DIM
    srcs = [p_ctx] if p_lat is None else [p_lat, p_ctx]
    k_all = jnp.concatenate([p[:, :, DK0 * SLAB:DK0 * SLAB + kw] for p in srcs], axis=1)
    v_all = jnp.concatenate([p[:, :, DV0 * SLAB:DV0 * SLAB + vw] for p in srcs], axis=1)
    n_chunks = k_all.shape[1] // bk
    k_all = k_all.reshape(b, n_chunks, bk, kw)
    v_all = jnp.transpose(v_all.reshape(b, n_chunks, bk, DIFF_HEADS, DIFF_V_DIM), (0, 3, 1, 4, 2))
    return pl.pallas_call(
        functools.partial(_diff_kernel, n_chunks=n_chunks, lam_init=lam_init, unroll=DIFF_UNROLL),
        grid=(b, DIFF_HEADS, sq // bq),
        in_specs=[
            pl.BlockSpec((4, DIFF_QK_DIM), lambda bi, h, qi: (0, 0)),
            pl.BlockSpec((1, DIFF_V_DIM), lambda bi, h, qi: (0, 0)),
            pl.BlockSpec((None, bq, SLAB), lambda bi, h, qi: (bi, qi, DQ0 + h)),
            pl.BlockSpec((None, n_chunks, bk, SLAB), lambda bi, h, qi: (bi, 0, 0, h)),
            pl.BlockSpec((None, None, n_chunks, DIFF_V_DIM, bk), lambda bi, h, qi: (bi, h, 0, 0, 0)),
        ],
        out_specs=pl.BlockSpec((None, bq, SLAB), lambda bi, h, qi: (bi, qi, h)),
        out_shape=jax.ShapeDtypeStruct((b, sq, vw), BF),
        compiler_params=_params(("arbitrary", "arbitrary", "arbitrary")),
        name="diff_attn" if p_lat is not None else "diff_attn_ctx",
    )(lam_par, subln, p_q, k_all, v_all)


def _swa_kernel(*refs, band, seq):
    if band:
        sink_ref, q_ref, kp_ref, kz_ref, kn_ref, vp_ref, vz_ref, vn_ref, kc_ref, vc_ref, o_ref = refs
    else:
        sink_ref, q_ref, kc_ref, vc_ref, o_ref = refs
    n = pl.program_id(1)
    blk = q_ref.shape[0]
    rows = SWA_GROUP * blk
    row = lax.broadcasted_iota(jnp.int32, (rows, (3 * blk if band else 0) + kc_ref.shape[0]), 0)
    if band:
        colx = lax.broadcasted_iota(jnp.int32, row.shape, 1)
        qpos = n * blk + (row & (blk - 1))
        kpos = (n - 1) * blk + colx
        valid = (colx >= 3 * blk) | ((jnp.abs(qpos - kpos) <= WINDOW) & (kpos >= 0) & (kpos < seq))
    rowc = lax.broadcasted_iota(jnp.int32, (rows, 1), 0)
    for g in range(SWA_KV_HEADS):
        gs = slice(g * SLAB, (g + 1) * SLAB)
        q4 = jnp.concatenate([q_ref[:, (g * SWA_GROUP + h) * SLAB:(g * SWA_GROUP + h + 1) * SLAB]
                              for h in range(SWA_GROUP)], axis=0)
        if band:
            kcat = jnp.concatenate([kp_ref[:, gs], kz_ref[:, gs], kn_ref[:, gs], kc_ref[:, gs]], axis=0)
            vcat = jnp.concatenate([vp_ref[:, gs], vz_ref[:, gs], vn_ref[:, gs], vc_ref[:, gs]], axis=0)
        else:
            kcat, vcat = kc_ref[:, gs], vc_ref[:, gs]
        s = lax.dot_general(q4, kcat, (((1,), (1,)), ((), ())), preferred_element_type=F32)
        if band:
            s = jnp.where(valid, s, NEG_INF)
        sk = jnp.full((rows, 1), sink_ref[g * SWA_GROUP + SWA_GROUP - 1], F32)
        for h in range(SWA_GROUP - 2, -1, -1):
            sk = jnp.where(rowc < (h + 1) * blk, sink_ref[g * SWA_GROUP + h], sk)
        sk = sk * LOG2E
        m = jnp.maximum(jnp.max(s, axis=-1, keepdims=True), sk)
        p = jnp.exp2(s - m)
        l = jnp.sum(p, axis=-1, keepdims=True) + jnp.exp2(sk - m)
        o = jnp.dot(p.astype(BF), vcat, preferred_element_type=F32) / l
        for h in range(SWA_GROUP):
            hs = (g * SWA_GROUP + h) * SLAB
            o_ref[:, hs:hs + SLAB] = o[h * blk:(h + 1) * blk, :].astype(BF)


def _swa_attention(p_q, p_ctx, sink, band):
    b, sq, _ = p_q.shape
    c = p_ctx.shape[1]
    blk = WINDOW
    nb = sq // blk
    qw = SWA_HEADS * SLAB
    kvw = SWA_KV_HEADS * SLAB
    in_specs = [
        pl.BlockSpec(memory_space=pltpu.SMEM),
        pl.BlockSpec((None, blk, qw), lambda bi, n: (bi, n, SQ0 * SLAB // qw)),
    ]
    args = [sink, p_q]
    if band:
        for base in (SK0 * SLAB // kvw, SV0 * SLAB // kvw):
            in_specs += [
                pl.BlockSpec((None, blk, kvw), lambda bi, n, base=base: (bi, jnp.maximum(n - 1, 0), base)),
                pl.BlockSpec((None, blk, kvw), lambda bi, n, base=base: (bi, n, base)),
                pl.BlockSpec((None, blk, kvw), lambda bi, n, base=base: (bi, jnp.minimum(n + 1, nb - 1), base)),
            ]
            args += [p_q, p_q, p_q]
    in_specs += [pl.BlockSpec((None, c, kvw), lambda bi, n: (bi, 0, SK0 * SLAB // kvw)),
                 pl.BlockSpec((None, c, kvw), lambda bi, n: (bi, 0, SV0 * SLAB // kvw))]
    args += [p_ctx, p_ctx]
    return pl.pallas_call(
        functools.partial(_swa_kernel, band=band, seq=sq),
        grid=(b, nb),
        in_specs=in_specs,
        out_specs=pl.BlockSpec((None, blk, qw), lambda bi, n: (bi, n, 0)),
        out_shape=jax.ShapeDtypeStruct((b, sq, qw), BF),
        compiler_params=_params(("arbitrary", "arbitrary")),
        name="swa_attn" if band else "swa_attn_ctx",
    )(*args)


def _post_norm_residual(src_scr, x_ref, gt_ref, g_ref, o_ref, bm):
    gt, g = gt_ref[...], g_ref[...]

    def rows_step(r, carry):
        rows = pl.ds(pl.multiple_of(r * NORM_CHUNK, NORM_CHUNK), NORM_CHUNK)
        f = src_scr[rows, :]
        ms = jnp.mean(f * f, axis=-1, keepdims=True)
        o_ref[rows, :] = x_ref[rows, :] + gt * (f * lax.rsqrt(ms + EPS) * g)
        return carry

    lax.fori_loop(0, bm // NORM_CHUNK, rows_step, 0)


def _outproj_kernel(od_ref, os_ref, wt_ref, wb_ref, x_ref, gt_ref, g_ref, o_ref, mix_scr, *, bm):
    mix_scr[...] = (jnp.dot(od_ref[...], wt_ref[...], preferred_element_type=F32)
                    + jnp.dot(os_ref[...], wb_ref[...], preferred_element_type=F32))
    _post_norm_residual(mix_scr, x_ref, gt_ref, g_ref, o_ref, bm)


def _outproj(od, osw, w_out, x, mods, layer, row_fn, g_post, bm):
    rows, d = x.shape
    half = od.shape[1]
    return pl.pallas_call(
        functools.partial(_outproj_kernel, bm=bm),
        grid=(rows // bm,),
        in_specs=[
            pl.BlockSpec((bm, half), lambda i: (i, 0)),
            pl.BlockSpec((bm, half), lambda i: (i, 0)),
            _resident((half, d), lambda i: (0, 0)),
            _resident((half, d), lambda i: (1, 0)),
            pl.BlockSpec((bm, d), lambda i: (i, 0)),
            _mod_spec(layer, 2, row_fn, 1)(d),
            pl.BlockSpec((1, d), lambda i: (0, 0)),
        ],
        out_specs=pl.BlockSpec((bm, d), lambda i: (i, 0)),
        out_shape=jax.ShapeDtypeStruct((rows, d), F32),
        scratch_shapes=[pltpu.VMEM((bm, d), F32)],
        compiler_params=_params(("arbitrary",)),
        name="outproj",
    )(od, osw, w_out, w_out, x, mods, g_post)


HALO = 16
CONV_CHUNK = 256


def _ffn_up_kernel(x_ref, xp_ref, xn_ref, sh_ref, sc_ref, g_ref, wg_ref, wu_ref, cw_ref, cb_ref, o_ref,
                   h_scr, *, bm, seq):
    i = pl.program_id(0)
    j = pl.program_id(1)

    @pl.when(j == 0)
    def _():
        g, sh, sc = g_ref[...], sh_ref[...], sc_ref[...]

        def norm_rows(r, carry):
            rows = pl.multiple_of(r * NORM_CHUNK, NORM_CHUNK)
            h_scr[pl.ds(HALO + rows, NORM_CHUNK), :] = _rms_mod(x_ref[pl.ds(rows, NORM_CHUNK), :], g, sh, sc).astype(BF)
            return carry

        lax.fori_loop(0, bm // NORM_CHUNK, norm_rows, 0)
        prev_ok = ((i * bm) % seq != 0).astype(F32)
        next_ok = (((i + 1) * bm) % seq != 0).astype(F32)
        h_scr[0:HALO, :] = (_rms_mod(xp_ref[...], g, sh, sc) * prev_ok).astype(BF)
        h_scr[HALO + bm:2 * HALO + bm, :] = (_rms_mod(xn_ref[...], g, sh, sc) * next_ok).astype(BF)

    cw0, cw1, cw2, cb = cw_ref[0:1, :], cw_ref[1:2, :], cw_ref[2:3, :], cb_ref[...]
    rc = min(bm, CONV_CHUNK)
    pad = 8
    for r in range(bm // rc):
        lo = r * rc
        g = jnp.dot(h_scr[lo:lo + rc + 2 * HALO, :], wg_ref[...], preferred_element_type=F32)
        u = jnp.dot(h_scr[lo + HALO:lo + HALO + rc, :], wu_ref[...], preferred_element_type=F32)
        win = g[HALO - pad:HALO + rc + pad]
        below = pltpu.roll(win, 1, axis=0)[pad:pad + rc]
        above = pltpu.roll(win, rc + 2 * pad - 1, axis=0)[pad:pad + rc]
        gc = below * cw0 + g[HALO:HALO + rc] * cw1 + above * cw2 + cb
        a = gc * _sigmoid(gc) * u
        o_ref[lo:lo + rc, :] = a.astype(BF)


def _ffn_up(x, mods, layer, row_fn, g_pre, wg, wu, cw, cb, seq, bm, bn):
    rows, d = x.shape
    dff = wg.shape[1]
    hb = bm // HALO
    nh = rows // HALO
    mod = lambda k: _mod_spec(layer, k, row_fn, 2)(d)
    return pl.pallas_call(
        functools.partial(_ffn_up_kernel, bm=bm, seq=seq),
        grid=(rows // bm, dff // bn),
        in_specs=[
            pl.BlockSpec((bm, d), lambda i, j: (i, 0)),
            pl.BlockSpec((HALO, d), lambda i, j: (jnp.maximum(i * hb - 1, 0), 0)),
            pl.BlockSpec((HALO, d), lambda i, j: (jnp.minimum((i + 1) * hb, nh - 1), 0)),
            mod(3), mod(4),
            pl.BlockSpec((1, d), lambda i, j: (0, 0)),
            pl.BlockSpec((d, bn), lambda i, j: (0, j)),
            pl.BlockSpec((d, bn), lambda i, j: (0, j)),
            pl.BlockSpec((3, bn), lambda i, j: (0, j)),
            pl.BlockSpec((1, bn), lambda i, j: (0, j)),
        ],
        out_specs=pl.BlockSpec((bm, bn), lambda i, j: (i, j)),
        out_shape=jax.ShapeDtypeStruct((rows, dff), BF),
        scratch_shapes=[pltpu.VMEM((bm + 2 * HALO, d), BF)],
        compiler_params=_params(("arbitrary", "arbitrary")),
        name="ffn_up",
    )(x, x, x, mods, mods, g_pre, wg, wu, cw, cb)


def _ffn_down_kernel(a_ref, w_ref, x_ref, gt_ref, g_ref, o_ref, f_scr, *, bm):
    f_scr[...] = jnp.dot(a_ref[...], w_ref[...], preferred_element_type=F32)
    _post_norm_residual(f_scr, x_ref, gt_ref, g_ref, o_ref, bm)


def _ffn_down(a, wd, x, mods, layer, row_fn, g_post, bm):
    rows, d = x.shape
    dff = a.shape[1]
    return pl.pallas_call(
        functools.partial(_ffn_down_kernel, bm=bm),
        grid=(rows // bm,),
        in_specs=[
            pl.BlockSpec((bm, dff), lambda i: (i, 0)),
            _resident((dff, d), lambda i: (0, 0)),
            pl.BlockSpec((bm, d), lambda i: (i, 0)),
            _mod_spec(layer, 5, row_fn, 1)(d),
            pl.BlockSpec((1, d), lambda i: (0, 0)),
        ],
        out_specs=pl.BlockSpec((bm, d), lambda i: (i, 0)),
        out_shape=jax.ShapeDtypeStruct((rows, d), F32),
        scratch_shapes=[pltpu.VMEM((bm, d), F32)],
        compiler_params=_params(("arbitrary",)),
        name="ffn_down",
    )(a, wd, x, mods, g_post)


def _rope_tables(seq, ctx_len):
    t = np.arange(seq)
    row, col = t // GRID_W, t % GRID_W

    def table(half):
        inv = ROPE_THETA ** (-np.arange(half, dtype=np.float64) / half)
        ar, ac = row[:, None] * inv[None, :], col[:, None] * inv[None, :]
        cos = np.concatenate([np.cos(ar), np.cos(ar), np.cos(ac), np.cos(ac)], axis=1)
        sin = np.concatenate([-np.sin(ar), np.sin(ar), -np.sin(ac), np.sin(ac)], axis=1)
        reps = SLAB // cos.shape[1]
        return np.tile(cos, (1, reps)).astype(np.float32), np.tile(sin, (1, reps)).astype(np.float32)

    cd, sd = table(DIFF_QK_DIM // 4)
    cs, ss = table(SWA_HEAD_DIM // 4)
    lat = tuple(jnp.asarray(a) for a in (cd, sd, cs, ss))
    one, zero = jnp.ones((ctx_len, SLAB), F32), jnp.zeros((ctx_len, SLAB), F32)
    return lat, (one, zero, one, zero)


def _pick(n, prefs):
    for p in prefs:
        if n % p == 0:
            return p
    raise ValueError(f"no block size in {prefs} divides {n}")


def kernel(x, c, ctx, c_ctx, w_ada, b_ada, g_pre_mix, g_post_mix, w_in, diff_lambda, diff_subln, swa_sink,
           w_out, g_pre_ffn, g_post_ffn, w_ffn_gate, w_ffn_up, ffn_conv_w, ffn_conv_b, w_ffn_down):
    b, s, d = x.shape
    cl = ctx.shape[1]
    depth = w_ada.shape[0]
    dff = w_ffn_gate.shape[2]
    assert w_in.shape[2] == IN_W and s % WINDOW == 0 and cl % WINDOW == 0

    n_rows = -(-(b + 1) // 8) * 8
    cvec = jnp.concatenate([c, c_ctx[None, :], jnp.zeros((n_rows - b - 1, d), F32)], axis=0)
    mods = _ada(cvec, w_ada, b_ada).reshape(depth, n_rows, 6, 1, d)

    lat_tabs, ctx_tabs = _rope_tables(s, cl)
    bm_in = _pick(s, (512, 256, 128))
    bm_out = _pick(s, (512, 256, 128))
    bm_up = _pick(s, (1024, 512, 256, 128))
    bm_dn = _pick(s, (256, 128))
    bm_c = _pick(cl, (256, 128))
    bn_ff = _pick(dff, (512, 256, 128))
    bq = _pick(s, (256, 128))
    bk = _pick(math.gcd(s, cl), (256, 128))
    bq_c = _pick(cl, (256, 128))

    xl = x.reshape(b * s, d)
    xc = ctx.reshape(b * cl, d)
    ctx_row = lambda i: b

    for l in range(depth):
        last = l == depth - 1
        lam_init = 0.8 - 0.6 * math.exp(-0.3 * l)
        w_in_l = w_in[l].astype(BF)
        w_out_l = w_out[l].astype(BF)
        wg_l, wu_l, wd_l = w_ffn_gate[l].astype(BF), w_ffn_up[l].astype(BF), w_ffn_down[l].astype(BF)
        gpm, gqm = g_pre_mix[l][None, :], g_post_mix[l][None, :]
        gpf, gqf = g_pre_ffn[l][None, :], g_post_ffn[l][None, :]
        sub = diff_subln[l][None, :]
        cb = ffn_conv_b[l][None, :]

        p_lat = _inproj(xl, mods, l, lambda i, n=s // bm_in: i // n, gpm, w_in_l, lat_tabs, s, bm_in)
        p_ctx = _inproj(xc, mods, l, ctx_row, gpm, w_in_l, ctx_tabs, cl, bm_c)
        p_lat3, p_ctx3 = p_lat.reshape(b, s, IN_W), p_ctx.reshape(b, cl, IN_W)
        od = _diff_attention(p_lat3, p_lat3, p_ctx3, diff_lambda[l], sub, lam_init, bq, bk)
        osw = _swa_attention(p_lat3, p_ctx3, swa_sink[l], band=True)
        xl = _outproj(od.reshape(b * s, -1), osw.reshape(b * s, -1), w_out_l, xl, mods, l,
                      lambda i, n=s // bm_out: i // n, gqm, bm_out)
        if not last:
            odc = _diff_attention(p_ctx3, None, p_ctx3, diff_lambda[l], sub, lam_init, bq_c, bk)
            oswc = _swa_attention(p_ctx3, p_ctx3, swa_sink[l], band=False)
            xc = _outproj(odc.reshape(b * cl, -1), oswc.reshape(b * cl, -1), w_out_l, xc, mods, l, ctx_row, gqm, bm_c)

        a = _ffn_up(xl, mods, l, lambda i, n=s // bm_up: i // n, gpf, wg_l, wu_l, ffn_conv_w[l], cb, s, bm_up, bn_ff)
        xl = _ffn_down(a, wd_l, xl, mods, l, lambda i, n=s // bm_dn: i // n, gqf, bm_dn)
        if not last:
            ac = _ffn_up(xc, mods, l, ctx_row, gpf, wg_l, wu_l, ffn_conv_w[l], cb, cl, bm_c, bn_ff)
            xc = _ffn_down(ac, wd_l, xc, mods, l, ctx_row, gqf, bm_c)
    return xl.reshape(b, s, d)
```

```python
import functools
import math

import jax
import jax.numpy as jnp
import numpy as np
from jax import lax
from jax.experimental import pallas as pl
from jax.experimental.pallas import tpu as pltpu

BF = jnp.bfloat16
F32 = jnp.float32

GRID_W = 64
DIFF_HEADS = 8
DIFF_QK_DIM = 64
DIFF_V_DIM = 128
SWA_HEADS = 8
SWA_KV_HEADS = 2
SWA_GROUP = SWA_HEADS // SWA_KV_HEADS
SWA_HEAD_DIM = 128
WINDOW = 128
ROPE_THETA = 10000.0
EPS = 1e-6
NEG_INF = -1e30
LOG2E = 1.4426950408889634
DIFF_QSCALE = DIFF_QK_DIM ** -0.5 * LOG2E
SWA_QSCALE = SWA_HEAD_DIM ** -0.5 * LOG2E

SLAB = 128
DQ0, DK0, DV0, SQ0, SK0, SV0 = 0, 8, 16, 24, 32, 34
N_SLABS = 36
IN_W = N_SLABS * SLAB

VMEM_LIMIT = 56 * 1024 * 1024
NORM_CHUNK = 64


def _params(sem, vmem=VMEM_LIMIT, flags=None):
    return pltpu.CompilerParams(dimension_semantics=sem, vmem_limit_bytes=vmem, flags=flags)


def _resident(shape, index_map):
    return pl.BlockSpec(shape, index_map, pipeline_mode=pl.Buffered(1))


def _sigmoid(v):
    return 1.0 / (1.0 + jnp.exp(-v))


def _rms_mod(xf, g, sh, sc):
    ms = jnp.mean(xf * xf, axis=-1, keepdims=True)
    return (xf * lax.rsqrt(ms + EPS) * g) * (1.0 + sc) + sh


def _rope(x, cos, sin, shift):
    lane = lax.broadcasted_iota(jnp.int32, x.shape, 1)
    fwd = pltpu.roll(x, SLAB - shift, axis=1)
    bwd = pltpu.roll(x, shift, axis=1)
    partner = jnp.where((lane & (2 * shift - 1)) < shift, fwd, bwd)
    return x * cos + partner * sin


def _ada_kernel(c_ref, w_ref, b_ref, o_ref):
    cv = c_ref[...]
    s = (cv * _sigmoid(cv)).astype(BF)
    o_ref[...] = jnp.dot(s, w_ref[...].astype(BF), preferred_element_type=F32) + b_ref[...]


def _ada(cvec, w_ada, b_ada):
    depth, d, n = w_ada.shape
    rows = cvec.shape[0]
    bn = 1024
    return pl.pallas_call(
        _ada_kernel,
        grid=(depth, n // bn),
        in_specs=[
            pl.BlockSpec((rows, d), lambda l, j: (0, 0)),
            pl.BlockSpec((None, d, bn), lambda l, j: (l, 0, j)),
            pl.BlockSpec((None, 1, bn), lambda l, j: (l, 0, j)),
        ],
        out_specs=pl.BlockSpec((None, rows, bn), lambda l, j: (l, 0, j)),
        out_shape=jax.ShapeDtypeStruct((depth, rows, n), F32),
        compiler_params=_params(("arbitrary", "arbitrary")),
        name="ada",
    )(cvec, w_ada, b_ada.reshape(depth, 1, n))


def _mod_spec(layer, k, row_fn, nargs):
    if nargs == 1:
        return lambda d: pl.BlockSpec((None, None, None, 1, d), lambda i: (layer, row_fn(i), k, 0, 0))
    return lambda d: pl.BlockSpec((None, None, None, 1, d), lambda i, j: (layer, row_fn(i), k, 0, 0))


def _inproj_kernel(x_ref, sh_ref, sc_ref, g_ref, w_ref, cd_ref, sd_ref, cs_ref, ss_ref, o_ref, h_scr, *, bm):
    g, sh, sc = g_ref[...], sh_ref[...], sc_ref[...]

    def norm_rows(r, carry):
        rows = pl.ds(pl.multiple_of(r * NORM_CHUNK, NORM_CHUNK), NORM_CHUNK)
        h_scr[rows, :] = _rms_mod(x_ref[rows, :], g, sh, sc).astype(BF)
        return carry

    lax.fori_loop(0, bm // NORM_CHUNK, norm_rows, 0)

    group = 4
    for grp in range(N_SLABS // group):
        acc = jnp.dot(h_scr[...], w_ref[:, grp * group * SLAB:(grp + 1) * group * SLAB],
                      preferred_element_type=F32)
        for t in range(group):
            slab = grp * group + t
            a = acc[:, t * SLAB:(t + 1) * SLAB]
            if slab < DV0:
                a = _rope(a, cd_ref[...], sd_ref[...], DIFF_QK_DIM // 4)
                if slab < DK0:
                    a = a * DIFF_QSCALE
            elif SQ0 <= slab < SV0:
                a = _rope(a, cs_ref[...], ss_ref[...], SWA_HEAD_DIM // 4)
                if slab < SK0:
                    a = a * SWA_QSCALE
            o_ref[:, slab * SLAB:(slab + 1) * SLAB] = a.astype(BF)


def _inproj(x, mods, layer, row_fn, g_pre, w, tables, seq, bm):
    rows, d = x.shape
    nt = seq // bm
    mod = lambda k: _mod_spec(layer, k, row_fn, 1)(d)
    tab = pl.BlockSpec((bm, SLAB), lambda i: (i % nt, 0))
    return pl.pallas_call(
        functools.partial(_inproj_kernel, bm=bm),
        grid=(rows // bm,),
        in_specs=[
            pl.BlockSpec((bm, d), lambda i: (i, 0)),
            mod(0), mod(1),
            pl.BlockSpec((1, d), lambda i: (0, 0)),
            _resident((d, IN_W), lambda i: (0, 0)),
            tab, tab, tab, tab,
        ],
        out_specs=pl.BlockSpec((bm, IN_W), lambda i: (i, 0)),
        out_shape=jax.ShapeDtypeStruct((rows, IN_W), BF),
        scratch_shapes=[pltpu.VMEM((bm, d), BF)],
        compiler_params=_params(("arbitrary",)),
        name="inproj",
    )(x, mods, mods, g_pre, w, *tables)


ONES_ROWS = 16
DIFF_SUB_BLOCKS = 2
DIFF_UNROLL = 31


def _diff_kernel(*refs, n_lat, lam_init, unroll, n_sub):
    if n_lat:
        lam_ref, sub_ref, q_ref, kl_ref, vl_ref, kc_ref, vc_ref, o_ref = refs
    else:
        lam_ref, sub_ref, q_ref, kc_ref, vc_ref, o_ref = refs
    lp = lam_ref[...]
    lam = (jnp.exp(jnp.sum(lp[0:1] * lp[1:2], axis=-1, keepdims=True))
           - jnp.exp(jnp.sum(lp[2:3] * lp[3:4], axis=-1, keepdims=True)) + lam_init)

    bq = q_ref.shape[0] // n_sub
    n_chunks = n_lat + 1
    keys = lambda t: kc_ref[...] if isinstance(t, int) and t == n_lat else kl_ref[t]
    values = lambda t: vc_ref[...] if isinstance(t, int) and t == n_lat else vl_ref[t]

    def one_block(rows):
        q = q_ref[rows, :].astype(F32)
        lane = lax.broadcasted_iota(jnp.int32, q.shape, 1)
        q_both = jnp.concatenate([jnp.where(lane < DIFF_QK_DIM, q, 0.0), jnp.where(lane >= DIFF_QK_DIM, q, 0.0)],
                                 axis=0)
        q_t = jnp.transpose(q_both).astype(BF)

        def scores(t):
            s = jnp.dot(keys(t), q_t, preferred_element_type=F32)
            return s, jnp.max(s, axis=0, keepdims=True)

        def softmax(sc, m):
            s, smax = sc
            mn = jnp.maximum(m, smax)
            return mn, jnp.exp2(m - mn), jnp.exp2(s - mn).astype(BF)

        def accumulate(t, p, alpha, acc):
            v_t = values(t)
            v_ext = jnp.concatenate([v_t, jnp.ones((ONES_ROWS, v_t.shape[1]), BF)], axis=0)
            return alpha * acc + jnp.dot(v_ext, p, preferred_element_type=F32)

        m = jnp.full((1, 2 * bq), NEG_INF, F32)
        acc = jnp.zeros((DIFF_V_DIM + ONES_ROWS, 2 * bq), F32)

        sc = scores(0)
        if n_chunks > 1:
            sc_next = scores(1)
            m, alpha, p = softmax(sc, m)

            def step(t, carry):
                sc, p, alpha, m, acc = carry
                sc_new = scores(t)
                m_new, alpha_new, p_new = softmax(sc, m)
                acc = accumulate(t - 2, p, alpha, acc)
                return sc_new, p_new, alpha_new, m_new, acc

            def body(i, carry):
                for u in range(unroll):
                    carry = step(2 + i * unroll + u, carry)
                return carry

            trips = max(n_lat - 2, 0) // unroll
            carry = (sc_next, p, alpha, m, acc)
            if trips:
                carry = lax.fori_loop(0, trips, body, carry)
            for t in range(2 + trips * unroll, n_chunks):
                carry = step(t, carry)
            sc, p, alpha, m, acc = carry
            acc = accumulate(n_chunks - 2, p, alpha, acc)
        m, alpha, p = softmax(sc, m)
        acc = accumulate(n_chunks - 1, p, alpha, acc)

        o_t = acc[:DIFF_V_DIM] / acc[DIFF_V_DIM:DIFF_V_DIM + 1]
        o = jnp.transpose(o_t[:, :bq]) - lam * jnp.transpose(o_t[:, bq:])
        ms = jnp.mean(o * o, axis=-1, keepdims=True)
        o_ref[rows, :] = (o * lax.rsqrt(ms + EPS) * sub_ref[...] * (1.0 - lam_init)).astype(BF)

    for sb in range(n_sub):
        one_block(slice(sb * bq, (sb + 1) * bq))


def _diff_attention(p_q, p_lat, p_ctx, lam_par, subln, lam_init, bq, bk):
    b, sq, _ = p_q.shape
    c = p_ctx.shape[1]
    vw = DIFF_HEADS * DIFF_V_DIM
    n_lat = 0 if p_lat is None else p_lat.shape[1] // bk
    n_sub = DIFF_SUB_BLOCKS if sq % (DIFF_SUB_BLOCKS * bq) == 0 else 1
    qb = n_sub * bq
    in_specs = [
        pl.BlockSpec((4, DIFF_QK_DIM), lambda bi, h, qi: (0, 0)),
        pl.BlockSpec((1, DIFF_V_DIM), lambda bi, h, qi: (0, 0)),
        pl.BlockSpec((None, qb, SLAB), lambda bi, h, qi: (bi, qi, DQ0 + h)),
    ]
    args = [lam_par, subln, p_q]
    values_of = lambda p: p[:, :, DV0 * SLAB:DV0 * SLAB + vw]
    if n_lat:
        k_lat = p_lat.reshape(b, n_lat, bk, IN_W)
        v_lat = jnp.transpose(values_of(p_lat).reshape(b, n_lat, bk, DIFF_HEADS, DIFF_V_DIM), (0, 3, 1, 4, 2))
        in_specs += [pl.BlockSpec((None, n_lat, bk, SLAB), lambda bi, h, qi: (bi, 0, 0, DK0 + h)),
                     pl.BlockSpec((None, None, n_lat, DIFF_V_DIM, bk), lambda bi, h, qi: (bi, h, 0, 0, 0))]
        args += [k_lat, v_lat]
    v_ctx = jnp.transpose(values_of(p_ctx).reshape(b, c, DIFF_HEADS, DIFF_V_DIM), (0, 2, 3, 1))
    in_specs += [pl.BlockSpec((None, c, SLAB), lambda bi, h, qi: (bi, 0, DK0 + h)),
                 pl.BlockSpec((None, None, DIFF_V_DIM, c), lambda bi, h, qi: (bi, h, 0, 0))]
    args += [p_ctx, v_ctx]
    return pl.pallas_call(
        functools.partial(_diff_kernel, n_lat=n_lat, lam_init=lam_init, unroll=DIFF_UNROLL, n_sub=n_sub),
        grid=(b, DIFF_HEADS, sq // qb),
        in_specs=in_specs,
        out_specs=pl.BlockSpec((None, qb, SLAB), lambda bi, h, qi: (bi, qi, h)),
        out_shape=jax.ShapeDtypeStruct((b, sq, vw), BF),
        compiler_params=_params(("arbitrary", "arbitrary", "arbitrary")),
        name="diff_attn" if n_lat else "diff_attn_ctx",
    )(*args)


def _swa_kernel(*refs, band):
    if band:
        sink_ref, mask_ref, q_ref, kp_ref, kz_ref, kn_ref, vp_ref, vz_ref, vn_ref, kc_ref, vc_ref, o_ref = refs
    else:
        sink_ref, q_ref, kc_ref, vc_ref, o_ref = refs
    blk = q_ref.shape[0]
    rows = SWA_GROUP * blk
    rowc = lax.broadcasted_iota(jnp.int32, (rows, 1), 0)
    for g in range(SWA_KV_HEADS):
        gs = slice(g * SLAB, (g + 1) * SLAB)
        q4 = jnp.concatenate([q_ref[:, (g * SWA_GROUP + h) * SLAB:(g * SWA_GROUP + h + 1) * SLAB]
                              for h in range(SWA_GROUP)], axis=0)
        if band:
            kcat = jnp.concatenate([kp_ref[:, gs], kz_ref[:, gs], kn_ref[:, gs], kc_ref[:, gs]], axis=0)
            vcat = jnp.concatenate([vp_ref[:, gs], vz_ref[:, gs], vn_ref[:, gs], vc_ref[:, gs]], axis=0)
        else:
            kcat, vcat = kc_ref[:, gs], vc_ref[:, gs]
        s = lax.dot_general(q4, kcat, (((1,), (1,)), ((), ())), preferred_element_type=F32)
        if band:
            s = s + mask_ref[...]
        sk = jnp.full((rows, 1), sink_ref[g * SWA_GROUP + SWA_GROUP - 1], F32)
        for h in range(SWA_GROUP - 2, -1, -1):
            sk = jnp.where(rowc < (h + 1) * blk, sink_ref[g * SWA_GROUP + h], sk)
        sk = sk * LOG2E
        m = jnp.maximum(jnp.max(s, axis=-1, keepdims=True), sk)
        p = jnp.exp2(s - m)
        l = jnp.sum(p, axis=-1, keepdims=True) + jnp.exp2(sk - m)
        o = jnp.dot(p.astype(BF), vcat, preferred_element_type=F32) / l
        for h in range(SWA_GROUP):
            hs = (g * SWA_GROUP + h) * SLAB
            o_ref[:, hs:hs + SLAB] = o[h * blk:(h + 1) * blk, :].astype(BF)


def _swa_attention(p_q, p_ctx, sink, band):
    b, sq, _ = p_q.shape
    c = p_ctx.shape[1]
    blk = WINDOW
    nb = sq // blk
    qw = SWA_HEADS * SLAB
    kvw = SWA_KV_HEADS * SLAB
    in_specs = [
        pl.BlockSpec(memory_space=pltpu.SMEM),
        pl.BlockSpec((None, blk, qw), lambda bi, n: (bi, n, SQ0 * SLAB // qw)),
    ]
    args = [sink, p_q]
    if band:
        r = np.arange(SWA_GROUP * blk)[:, None] % blk
        col = np.arange(3 * blk + c)[None, :]
        in_band = (np.abs(r - (col - blk)) <= WINDOW) | (col >= 3 * blk)
        variants = [in_band & ~((col < blk) & bool(v & 1)) & ~((col >= 2 * blk) & (col < 3 * blk) & bool(v & 2))
                    for v in range(4)]
        mask = jnp.asarray(np.where(np.stack(variants), 0.0, NEG_INF).astype(np.float32))
        in_specs.insert(1, pl.BlockSpec(
            (None,) + mask.shape[1:],
            lambda bi, n: (jnp.where(n == 0, 1, 0) + jnp.where(n == nb - 1, 2, 0), 0, 0)))
        args.insert(1, mask)
        for base in (SK0 * SLAB // kvw, SV0 * SLAB // kvw):
            in_specs += [
                pl.BlockSpec((None, blk, kvw), lambda bi, n, base=base: (bi, jnp.maximum(n - 1, 0), base)),
                pl.BlockSpec((None, blk, kvw), lambda bi, n, base=base: (bi, n, base)),
                pl.BlockSpec((None, blk, kvw), lambda bi, n, base=base: (bi, jnp.minimum(n + 1, nb - 1), base)),
            ]
            args += [p_q, p_q, p_q]
    in_specs += [pl.BlockSpec((None, c, kvw), lambda bi, n: (bi, 0, SK0 * SLAB // kvw)),
                 pl.BlockSpec((None, c, kvw), lambda bi, n: (bi, 0, SV0 * SLAB // kvw))]
    args += [p_ctx, p_ctx]
    return pl.pallas_call(
        functools.partial(_swa_kernel, band=band),
        grid=(b, nb),
        in_specs=in_specs,
        out_specs=pl.BlockSpec((None, blk, qw), lambda bi, n: (bi, n, 0)),
        out_shape=jax.ShapeDtypeStruct((b, sq, qw), BF),
        compiler_params=_params(("arbitrary", "arbitrary")),
        name="swa_attn" if band else "swa_attn_ctx",
    )(*args)


def _post_norm_residual(src_scr, x_ref, gt_ref, g_ref, o_ref, bm):
    gt, g = gt_ref[...], g_ref[...]

    def rows_step(r, carry):
        rows = pl.ds(pl.multiple_of(r * NORM_CHUNK, NORM_CHUNK), NORM_CHUNK)
        f = src_scr[rows, :]
        ms = jnp.mean(f * f, axis=-1, keepdims=True)
        o_ref[rows, :] = x_ref[rows, :] + gt * (f * lax.rsqrt(ms + EPS) * g)
        return carry

    lax.fori_loop(0, bm // NORM_CHUNK, rows_step, 0)


def _outproj_kernel(od_ref, os_ref, wt_ref, wb_ref, x_ref, gt_ref, g_ref, o_ref, mix_scr, *, bm):
    mix_scr[...] = (jnp.dot(od_ref[...], wt_ref[...], preferred_element_type=F32)
                    + jnp.dot(os_ref[...], wb_ref[...], preferred_element_type=F32))
    _post_norm_residual(mix_scr, x_ref, gt_ref, g_ref, o_ref, bm)


def _outproj(od, osw, w_out, x, mods, layer, row_fn, g_post, bm):
    rows, d = x.shape
    half = od.shape[1]
    return pl.pallas_call(
        functools.partial(_outproj_kernel, bm=bm),
        grid=(rows // bm,),
        in_specs=[
            pl.BlockSpec((bm, half), lambda i: (i, 0)),
            pl.BlockSpec((bm, half), lambda i: (i, 0)),
            _resident((half, d), lambda i: (0, 0)),
            _resident((half, d), lambda i: (1, 0)),
            pl.BlockSpec((bm, d), lambda i: (i, 0)),
            _mod_spec(layer, 2, row_fn, 1)(d),
            pl.BlockSpec((1, d), lambda i: (0, 0)),
        ],
        out_specs=pl.BlockSpec((bm, d), lambda i: (i, 0)),
        out_shape=jax.ShapeDtypeStruct((rows, d), F32),
        scratch_shapes=[pltpu.VMEM((bm, d), F32)],
        compiler_params=_params(("arbitrary",)),
        name="outproj",
    )(od, osw, w_out, w_out, x, mods, g_post)


HALO = 16
CONV_CHUNK = 256


def _ffn_up_kernel(x_ref, xp_ref, xn_ref, sh_ref, sc_ref, g_ref, wg_ref, wu_ref, cw_ref, cb_ref, o_ref,
                   h_scr, *, bm, seq):
    i = pl.program_id(0)
    j = pl.program_id(1)

    @pl.when(j == 0)
    def _():
        g, sh, sc = g_ref[...], sh_ref[...], sc_ref[...]

        def norm_rows(r, carry):
            rows = pl.multiple_of(r * NORM_CHUNK, NORM_CHUNK)
            h_scr[pl.ds(HALO + rows, NORM_CHUNK), :] = _rms_mod(x_ref[pl.ds(rows, NORM_CHUNK), :], g, sh, sc).astype(BF)
            return carry

        lax.fori_loop(0, bm // NORM_CHUNK, norm_rows, 0)
        prev_ok = ((i * bm) % seq != 0).astype(F32)
        next_ok = (((i + 1) * bm) % seq != 0).astype(F32)
        h_scr[0:HALO, :] = (_rms_mod(xp_ref[...], g, sh, sc) * prev_ok).astype(BF)
        h_scr[HALO + bm:2 * HALO + bm, :] = (_rms_mod(xn_ref[...], g, sh, sc) * next_ok).astype(BF)

    cw0, cw1, cw2, cb = cw_ref[0:1, :], cw_ref[1:2, :], cw_ref[2:3, :], cb_ref[...]
    rc = min(bm, CONV_CHUNK)
    pad = 8
    for r in range(bm // rc):
        lo = r * rc
        g = jnp.dot(h_scr[lo:lo + rc + 2 * HALO, :], wg_ref[...], preferred_element_type=F32)
        u = jnp.dot(h_scr[lo + HALO:lo + HALO + rc, :], wu_ref[...], preferred_element_type=F32)
        win = g[HALO - pad:HALO + rc + pad]
        below = pltpu.roll(win, 1, axis=0)[pad:pad + rc]
        above = pltpu.roll(win, rc + 2 * pad - 1, axis=0)[pad:pad + rc]
        gc = below * cw0 + g[HALO:HALO + rc] * cw1 + above * cw2 + cb
        a = gc * _sigmoid(gc) * u
        o_ref[lo:lo + rc, :] = a.astype(BF)


def _ffn_up(x, mods, layer, row_fn, g_pre, wg, wu, cw, cb, seq, bm, bn):
    rows, d = x.shape
    dff = wg.shape[1]
    hb = bm // HALO
    nh = rows // HALO
    mod = lambda k: _mod_spec(layer, k, row_fn, 2)(d)
    return pl.pallas_call(
        functools.partial(_ffn_up_kernel, bm=bm, seq=seq),
        grid=(rows // bm, dff // bn),
        in_specs=[
            pl.BlockSpec((bm, d), lambda i, j: (i, 0)),
            pl.BlockSpec((HALO, d), lambda i, j: (jnp.maximum(i * hb - 1, 0), 0)),
            pl.BlockSpec((HALO, d), lambda i, j: (jnp.minimum((i + 1) * hb, nh - 1), 0)),
            mod(3), mod(4),
            pl.BlockSpec((1, d), lambda i, j: (0, 0)),
            pl.BlockSpec((d, bn), lambda i, j: (0, j)),
            pl.BlockSpec((d, bn), lambda i, j: (0, j)),
            pl.BlockSpec((3, bn), lambda i, j: (0, j)),
            pl.BlockSpec((1, bn), lambda i, j: (0, j)),
        ],
        out_specs=pl.BlockSpec((bm, bn), lambda i, j: (i, j)),
        out_shape=jax.ShapeDtypeStruct((rows, dff), BF),
        scratch_shapes=[pltpu.VMEM((bm + 2 * HALO, d), BF)],
        compiler_params=_params(("arbitrary", "arbitrary")),
        name="ffn_up",
    )(x, x, x, mods, mods, g_pre, wg, wu, cw, cb)


def _ffn_down_kernel(a_ref, w_ref, x_ref, gt_ref, g_ref, o_ref, f_scr, *, bm):
    f_scr[...] = jnp.dot(a_ref[...], w_ref[...], preferred_element_type=F32)
    _post_norm_residual(f_scr, x_ref, gt_ref, g_ref, o_ref, bm)


def _ffn_down(a, wd, x, mods, layer, row_fn, g_post, bm):
    rows, d = x.shape
    dff = a.shape[1]
    return pl.pallas_call(
        functools.partial(_ffn_down_kernel, bm=bm),
        grid=(rows // bm,),
        in_specs=[
            pl.BlockSpec((bm, dff), lambda i: (i, 0)),
            _resident((dff, d), lambda i: (0, 0)),
            pl.BlockSpec((bm, d), lambda i: (i, 0)),
            _mod_spec(layer, 5, row_fn, 1)(d),
            pl.BlockSpec((1, d), lambda i: (0, 0)),
        ],
        out_specs=pl.BlockSpec((bm, d), lambda i: (i, 0)),
        out_shape=jax.ShapeDtypeStruct((rows, d), F32),
        scratch_shapes=[pltpu.VMEM((bm, d), F32)],
        compiler_params=_params(("arbitrary",)),
        name="ffn_down",
    )(a, wd, x, mods, g_post)


def _rope_tables(seq, ctx_len):
    t = np.arange(seq)
    row, col = t // GRID_W, t % GRID_W

    def table(half):
        inv = ROPE_THETA ** (-np.arange(half, dtype=np.float64) / half)
        ar, ac = row[:, None] * inv[None, :], col[:, None] * inv[None, :]
        cos = np.concatenate([np.cos(ar), np.cos(ar), np.cos(ac), np.cos(ac)], axis=1)
        sin = np.concatenate([-np.sin(ar), np.sin(ar), -np.sin(ac), np.sin(ac)], axis=1)
        reps = SLAB // cos.shape[1]
        return np.tile(cos, (1, reps)).astype(np.float32), np.tile(sin, (1, reps)).astype(np.float32)

    cd, sd = table(DIFF_QK_DIM // 4)
    cs, ss = table(SWA_HEAD_DIM // 4)
    lat = tuple(jnp.asarray(a) for a in (cd, sd, cs, ss))
    one, zero = jnp.ones((ctx_len, SLAB), F32), jnp.zeros((ctx_len, SLAB), F32)
    return lat, (one, zero, one, zero)


def _pick(n, prefs):
    for p in prefs:
        if n % p == 0:
            return p
    raise ValueError(f"no block size in {prefs} divides {n}")


def kernel(x, c, ctx, c_ctx, w_ada, b_ada, g_pre_mix, g_post_mix, w_in, diff_lambda, diff_subln, swa_sink,
           w_out, g_pre_ffn, g_post_ffn, w_ffn_gate, w_ffn_up, ffn_conv_w, ffn_conv_b, w_ffn_down):
    b, s, d = x.shape
    cl = ctx.shape[1]
    depth = w_ada.shape[0]
    dff = w_ffn_gate.shape[2]
    assert w_in.shape[2] == IN_W and s % WINDOW == 0 and cl % WINDOW == 0

    n_rows = -(-(b + 1) // 8) * 8
    cvec = jnp.concatenate([c, c_ctx[None, :], jnp.zeros((n_rows - b - 1, d), F32)], axis=0)
    mods = _ada(cvec, w_ada, b_ada).reshape(depth, n_rows, 6, 1, d)

    lat_tabs, ctx_tabs = _rope_tables(s, cl)
    bm_in = _pick(s, (512, 256, 128))
    bm_out = _pick(s, (512, 256, 128))
    bm_up = _pick(s, (1024, 512, 256, 128))
    bm_dn = _pick(s, (256, 128))
    bm_c = _pick(cl, (256, 128))
    bn_ff = _pick(dff, (512, 256, 128))
    bq = _pick(s, (256, 128))
    bk = _pick(s, (256, 128))
    bq_c = _pick(cl, (256, 128))

    xl = x.reshape(b * s, d)
    xc = ctx.reshape(b * cl, d)
    ctx_row = lambda i: b

    for l in range(depth):
        last = l == depth - 1
        lam_init = 0.8 - 0.6 * math.exp(-0.3 * l)
        w_in_l = w_in[l].astype(BF)
        w_out_l = w_out[l].astype(BF)
        wg_l, wu_l, wd_l = w_ffn_gate[l].astype(BF), w_ffn_up[l].astype(BF), w_ffn_down[l].astype(BF)
        gpm, gqm = g_pre_mix[l][None, :], g_post_mix[l][None, :]
        gpf, gqf = g_pre_ffn[l][None, :], g_post_ffn[l][None, :]
        sub = diff_subln[l][None, :]
        cb = ffn_conv_b[l][None, :]

        p_lat = _inproj(xl, mods, l, lambda i, n=s // bm_in: i // n, gpm, w_in_l, lat_tabs, s, bm_in)
        p_ctx = _inproj(xc, mods, l, ctx_row, gpm, w_in_l, ctx_tabs, cl, bm_c)
        p_lat3, p_ctx3 = p_lat.reshape(b, s, IN_W), p_ctx.reshape(b, cl, IN_W)
        od = _diff_attention(p_lat3, p_lat3, p_ctx3, diff_lambda[l], sub, lam_init, bq, bk)
        osw = _swa_attention(p_lat3, p_ctx3, swa_sink[l], band=True)
        xl = _outproj(od.reshape(b * s, -1), osw.reshape(b * s, -1), w_out_l, xl, mods, l,
                      lambda i, n=s // bm_out: i // n, gqm, bm_out)
        if not last:
            odc = _diff_attention(p_ctx3, None, p_ctx3, diff_lambda[l], sub, lam_init, bq_c, bk)
            oswc = _swa_attention(p_ctx3, p_ctx3, swa_sink[l], band=False)
            xc = _outproj(odc.reshape(b * cl, -1), oswc.reshape(b * cl, -1), w_out_l, xc, mods, l, ctx_row, gqm, bm_c)

        a = _ffn_up(xl, mods, l, lambda i, n=s // bm_up: i // n, gpf, wg_l, wu_l, ffn_conv_w[l], cb, s, bm_up, bn_ff)
        xl = _ffn_down(a, wd_l, xl, mods, l, lambda i, n=s // bm_dn: i // n, gqf, bm_dn)
        if not last:
            ac = _ffn_up(xc, mods, l, ctx_row, gpf, wg_l, wu_l, ffn_conv_w[l], cb, cl, bm_c, bn_ff)
            xc = _ffn_down(ac, wd_l, xc, mods, l, ctx_row, gqf, bm_c)
    return xl.reshape(b, s, d)
```

```python
import functools
import math

import jax
import jax.numpy as jnp
import numpy as np
from jax import lax
from jax.experimental import pallas as pl
from jax.experimental.pallas import tpu as pltpu

BF = jnp.bfloat16
F32 = jnp.float32

GRID_W = 64
DIFF_HEADS = 8
DIFF_QK_DIM = 64
DIFF_V_DIM = 128
SWA_HEADS = 8
SWA_KV_HEADS = 2
SWA_GROUP = SWA_HEADS // SWA_KV_HEADS
SWA_HEAD_DIM = 128
WINDOW = 128
ROPE_THETA = 10000.0
EPS = 1e-6
NEG_INF = -1e30
LOG2E = 1.4426950408889634
DIFF_QSCALE = DIFF_QK_DIM ** -0.5 * LOG2E
SWA_QSCALE = SWA_HEAD_DIM ** -0.5 * LOG2E

SLAB = 128
DQ0, DK0, DV0, SQ0, SK0, SV0 = 0, 8, 16, 24, 32, 34
N_SLABS = 36
IN_W = N_SLABS * SLAB

VMEM_LIMIT = 56 * 1024 * 1024
NORM_CHUNK = 64
ROW_CHUNK = 256


def _params(sem, vmem=VMEM_LIMIT, flags=None):
    return pltpu.CompilerParams(dimension_semantics=sem, vmem_limit_bytes=vmem, flags=flags)


def _resident(shape, index_map):
    return pl.BlockSpec(shape, index_map, pipeline_mode=pl.Buffered(1))


def _sigmoid(v):
    return 1.0 / (1.0 + jnp.exp(-v))


def _rms_mod(xf, g, sh, sc):
    ms = jnp.mean(xf * xf, axis=-1, keepdims=True)
    return (xf * lax.rsqrt(ms + EPS) * g) * (1.0 + sc) + sh


def _rope(x, cos, sin, shift):
    lane = lax.broadcasted_iota(jnp.int32, x.shape, 1)
    fwd = pltpu.roll(x, SLAB - shift, axis=1)
    bwd = pltpu.roll(x, shift, axis=1)
    partner = jnp.where((lane & (2 * shift - 1)) < shift, fwd, bwd)
    return x * cos + partner * sin


def _ada_kernel(c_ref, w_ref, b_ref, o_ref):
    cv = c_ref[...]
    s = (cv * _sigmoid(cv)).astype(BF)
    o_ref[...] = jnp.dot(s, w_ref[...].astype(BF), preferred_element_type=F32) + b_ref[...]


def _ada(cvec, w_ada, b_ada):
    depth, d, n = w_ada.shape
    rows = cvec.shape[0]
    bn = 1024
    return pl.pallas_call(
        _ada_kernel,
        grid=(depth, n // bn),
        in_specs=[
            pl.BlockSpec((rows, d), lambda l, j: (0, 0)),
            pl.BlockSpec((None, d, bn), lambda l, j: (l, 0, j)),
            pl.BlockSpec((None, 1, bn), lambda l, j: (l, 0, j)),
        ],
        out_specs=pl.BlockSpec((None, rows, bn), lambda l, j: (l, 0, j)),
        out_shape=jax.ShapeDtypeStruct((depth, rows, n), F32),
        compiler_params=_params(("arbitrary", "arbitrary")),
        name="ada",
    )(cvec, w_ada, b_ada.reshape(depth, 1, n))


def _mod_spec(layer, k, row_fn, nargs):
    if nargs == 1:
        return lambda d: pl.BlockSpec((None, None, None, 1, d), lambda i: (layer, row_fn(i), k, 0, 0))
    return lambda d: pl.BlockSpec((None, None, None, 1, d), lambda i, j: (layer, row_fn(i), k, 0, 0))


def _inproj_kernel(x_ref, sh_ref, sc_ref, g_ref, w_ref, cd_ref, sd_ref, cs_ref, ss_ref, o_ref, h_scr, *, bm):
    g, sh, sc = g_ref[...], sh_ref[...], sc_ref[...]
    group = 4
    rc = min(bm, ROW_CHUNK)
    for lo in range(0, bm, rc):
        for r in range(lo, lo + rc, NORM_CHUNK):
            h_scr[r:r + NORM_CHUNK, :] = _rms_mod(x_ref[r:r + NORM_CHUNK, :], g, sh, sc).astype(BF)
        rows = slice(lo, lo + rc)
        for grp in range(N_SLABS // group):
            acc = jnp.dot(h_scr[rows, :], w_ref[:, grp * group * SLAB:(grp + 1) * group * SLAB],
                          preferred_element_type=F32)
            for t in range(group):
                slab = grp * group + t
                a = acc[:, t * SLAB:(t + 1) * SLAB]
                if slab < DV0:
                    a = _rope(a, cd_ref[rows, :], sd_ref[rows, :], DIFF_QK_DIM // 4)
                    if slab < DK0:
                        a = a * DIFF_QSCALE
                elif SQ0 <= slab < SV0:
                    a = _rope(a, cs_ref[rows, :], ss_ref[rows, :], SWA_HEAD_DIM // 4)
                    if slab < SK0:
                        a = a * SWA_QSCALE
                o_ref[rows, slab * SLAB:(slab + 1) * SLAB] = a.astype(BF)


def _inproj(x, mods, layer, row_fn, g_pre, w, tables, seq, bm):
    rows, d = x.shape
    nt = seq // bm
    mod = lambda k: _mod_spec(layer, k, row_fn, 1)(d)
    tab = pl.BlockSpec((bm, SLAB), lambda i: (i % nt, 0))
    return pl.pallas_call(
        functools.partial(_inproj_kernel, bm=bm),
        grid=(rows // bm,),
        in_specs=[
            pl.BlockSpec((bm, d), lambda i: (i, 0)),
            mod(0), mod(1),
            pl.BlockSpec((1, d), lambda i: (0, 0)),
            _resident((None, d, IN_W), lambda i: (layer, 0, 0)),
            tab, tab, tab, tab,
        ],
        out_specs=pl.BlockSpec((bm, IN_W), lambda i: (i, 0)),
        out_shape=jax.ShapeDtypeStruct((rows, IN_W), BF),
        scratch_shapes=[pltpu.VMEM((bm, d), BF)],
        compiler_params=_params(("arbitrary",)),
        name="inproj",
    )(x, mods, mods, g_pre, w, *tables)


ONES_ROWS = 16
DIFF_SUB_BLOCKS = 2
DIFF_UNROLL = 31


def _diff_kernel(*refs, n_lat, lam_init, unroll, n_sub):
    if n_lat:
        lam_ref, sub_ref, q_ref, kl_ref, vl_ref, kc_ref, vc_ref, o_ref = refs
    else:
        lam_ref, sub_ref, q_ref, kc_ref, vc_ref, o_ref = refs
    lp = lam_ref[...]
    lam = (jnp.exp(jnp.sum(lp[0:1] * lp[1:2], axis=-1, keepdims=True))
           - jnp.exp(jnp.sum(lp[2:3] * lp[3:4], axis=-1, keepdims=True)) + lam_init)

    bq = q_ref.shape[0] // n_sub
    n_chunks = n_lat + 1
    keys = lambda t: kc_ref[...] if isinstance(t, int) and t == n_lat else kl_ref[t]
    values = lambda t: vc_ref[...] if isinstance(t, int) and t == n_lat else vl_ref[t]

    def one_block(rows):
        q = q_ref[rows, :].astype(F32)
        lane = lax.broadcasted_iota(jnp.int32, q.shape, 1)
        q_both = jnp.concatenate([jnp.where(lane < DIFF_QK_DIM, q, 0.0), jnp.where(lane >= DIFF_QK_DIM, q, 0.0)],
                                 axis=0)
        q_t = jnp.transpose(q_both).astype(BF)

        def scores(t):
            s = jnp.dot(keys(t), q_t, preferred_element_type=F32)
            return s, jnp.max(s, axis=0, keepdims=True)

        def softmax(sc, m):
            s, smax = sc
            mn = jnp.maximum(m, smax)
            return mn, jnp.exp2(m - mn), jnp.exp2(s - mn).astype(BF)

        def accumulate(t, p, alpha, acc):
            v_t = values(t)
            v_ext = jnp.concatenate([v_t, jnp.ones((ONES_ROWS, v_t.shape[1]), BF)], axis=0)
            return alpha * acc + jnp.dot(v_ext, p, preferred_element_type=F32)

        m = jnp.full((1, 2 * bq), NEG_INF, F32)
        acc = jnp.zeros((DIFF_V_DIM + ONES_ROWS, 2 * bq), F32)

        sc = scores(0)
        if n_chunks > 1:
            sc_next = scores(1)
            m, alpha, p = softmax(sc, m)

            def step(t, carry):
                sc, p, alpha, m, acc = carry
                sc_new = scores(t)
                m_new, alpha_new, p_new = softmax(sc, m)
                acc = accumulate(t - 2, p, alpha, acc)
                return sc_new, p_new, alpha_new, m_new, acc

            def body(i, carry):
                for u in range(unroll):
                    carry = step(2 + i * unroll + u, carry)
                return carry

            trips = max(n_lat - 2, 0) // unroll
            carry = (sc_next, p, alpha, m, acc)
            if trips:
                carry = lax.fori_loop(0, trips, body, carry)
            for t in range(2 + trips * unroll, n_chunks):
                carry = step(t, carry)
            sc, p, alpha, m, acc = carry
            acc = accumulate(n_chunks - 2, p, alpha, acc)
        m, alpha, p = softmax(sc, m)
        acc = accumulate(n_chunks - 1, p, alpha, acc)

        o_t = acc[:DIFF_V_DIM] / acc[DIFF_V_DIM:DIFF_V_DIM + 1]
        o = jnp.transpose(o_t[:, :bq]) - lam * jnp.transpose(o_t[:, bq:])
        ms = jnp.mean(o * o, axis=-1, keepdims=True)
        o_ref[rows, :] = (o * lax.rsqrt(ms + EPS) * sub_ref[...] * (1.0 - lam_init)).astype(BF)

    for sb in range(n_sub):
        one_block(slice(sb * bq, (sb + 1) * bq))


def _diff_attention(p_q, p_lat, p_ctx, lam_par, subln, lam_init, bq, bk):
    b, sq, _ = p_q.shape
    c = p_ctx.shape[1]
    vw = DIFF_HEADS * DIFF_V_DIM
    n_lat = 0 if p_lat is None else p_lat.shape[1] // bk
    n_sub = DIFF_SUB_BLOCKS if sq % (DIFF_SUB_BLOCKS * bq) == 0 else 1
    qb = n_sub * bq
    in_specs = [
        pl.BlockSpec((4, DIFF_QK_DIM), lambda bi, h, qi: (0, 0)),
        pl.BlockSpec((1, DIFF_V_DIM), lambda bi, h, qi: (0, 0)),
        pl.BlockSpec((None, qb, SLAB), lambda bi, h, qi: (bi, qi, DQ0 + h)),
    ]
    args = [lam_par, subln, p_q]
    values_of = lambda p: p[:, :, DV0 * SLAB:DV0 * SLAB + vw]
    if n_lat:
        k_lat = p_lat.reshape(b, n_lat, bk, IN_W)
        v_lat = jnp.transpose(values_of(p_lat).reshape(b, n_lat, bk, DIFF_HEADS, DIFF_V_DIM), (0, 3, 1, 4, 2))
        in_specs += [pl.BlockSpec((None, n_lat, bk, SLAB), lambda bi, h, qi: (bi, 0, 0, DK0 + h)),
                     pl.BlockSpec((None, None, n_lat, DIFF_V_DIM, bk), lambda bi, h, qi: (bi, h, 0, 0, 0))]
        args += [k_lat, v_lat]
    v_ctx = jnp.transpose(values_of(p_ctx).reshape(b, c, DIFF_HEADS, DIFF_V_DIM), (0, 2, 3, 1))
    in_specs += [pl.BlockSpec((None, c, SLAB), lambda bi, h, qi: (bi, 0, DK0 + h)),
                 pl.BlockSpec((None, None, DIFF_V_DIM, c), lambda bi, h, qi: (bi, h, 0, 0))]
    args += [p_ctx, v_ctx]
    return pl.pallas_call(
        functools.partial(_diff_kernel, n_lat=n_lat, lam_init=lam_init, unroll=DIFF_UNROLL, n_sub=n_sub),
        grid=(b, DIFF_HEADS, sq // qb),
        in_specs=in_specs,
        out_specs=pl.BlockSpec((None, qb, SLAB), lambda bi, h, qi: (bi, qi, h)),
        out_shape=jax.ShapeDtypeStruct((b, sq, vw), BF),
        compiler_params=_params(("arbitrary", "arbitrary", "arbitrary")),
        name="diff_attn" if n_lat else "diff_attn_ctx",
    )(*args)


def _swa_kernel(*refs, band):
    if band:
        sink_ref, mask_ref, q_ref, kp_ref, kz_ref, kn_ref, vp_ref, vz_ref, vn_ref, kc_ref, vc_ref, o_ref = refs
    else:
        sink_ref, q_ref, kc_ref, vc_ref, o_ref = refs
    blk = q_ref.shape[0]
    rows = SWA_GROUP * blk
    rowc = lax.broadcasted_iota(jnp.int32, (rows, 1), 0)
    for g in range(SWA_KV_HEADS):
        gs = slice(g * SLAB, (g + 1) * SLAB)
        q4 = jnp.concatenate([q_ref[:, (g * SWA_GROUP + h) * SLAB:(g * SWA_GROUP + h + 1) * SLAB]
                              for h in range(SWA_GROUP)], axis=0)
        if band:
            kcat = jnp.concatenate([kp_ref[:, gs], kz_ref[:, gs], kn_ref[:, gs], kc_ref[:, gs]], axis=0)
            vcat = jnp.concatenate([vp_ref[:, gs], vz_ref[:, gs], vn_ref[:, gs], vc_ref[:, gs]], axis=0)
        else:
            kcat, vcat = kc_ref[:, gs], vc_ref[:, gs]
        s = lax.dot_general(q4, kcat, (((1,), (1,)), ((), ())), preferred_element_type=F32)
        if band:
            s = s + mask_ref[...]
        sk = jnp.full((rows, 1), sink_ref[g * SWA_GROUP + SWA_GROUP - 1], F32)
        for h in range(SWA_GROUP - 2, -1, -1):
            sk = jnp.where(rowc < (h + 1) * blk, sink_ref[g * SWA_GROUP + h], sk)
        sk = sk * LOG2E
        m = jnp.maximum(jnp.max(s, axis=-1, keepdims=True), sk)
        p = jnp.exp2(s - m)
        l = jnp.sum(p, axis=-1, keepdims=True) + jnp.exp2(sk - m)
        o = jnp.dot(p.astype(BF), vcat, preferred_element_type=F32) / l
        for h in range(SWA_GROUP):
            hs = (g * SWA_GROUP + h) * SLAB
            o_ref[:, hs:hs + SLAB] = o[h * blk:(h + 1) * blk, :].astype(BF)


def _swa_attention(p_q, p_ctx, sink, band):
    b, sq, _ = p_q.shape
    c = p_ctx.shape[1]
    blk = WINDOW
    nb = sq // blk
    qw = SWA_HEADS * SLAB
    kvw = SWA_KV_HEADS * SLAB
    in_specs = [
        pl.BlockSpec(memory_space=pltpu.SMEM),
        pl.BlockSpec((None, blk, qw), lambda bi, n: (bi, n, SQ0 * SLAB // qw)),
    ]
    args = [sink, p_q]
    if band:
        r = np.arange(SWA_GROUP * blk)[:, None] % blk
        col = np.arange(3 * blk + c)[None, :]
        in_band = (np.abs(r - (col - blk)) <= WINDOW) | (col >= 3 * blk)
        variants = [in_band & ~((col < blk) & bool(v & 1)) & ~((col >= 2 * blk) & (col < 3 * blk) & bool(v & 2))
                    for v in range(4)]
        mask = jnp.asarray(np.where(np.stack(variants), 0.0, NEG_INF).astype(np.float32))
        in_specs.insert(1, pl.BlockSpec(
            (None,) + mask.shape[1:],
            lambda bi, n: (jnp.where(n == 0, 1, 0) + jnp.where(n == nb - 1, 2, 0), 0, 0)))
        args.insert(1, mask)
        for base in (SK0 * SLAB // kvw, SV0 * SLAB // kvw):
            in_specs += [
                pl.BlockSpec((None, blk, kvw), lambda bi, n, base=base: (bi, jnp.maximum(n - 1, 0), base)),
                pl.BlockSpec((None, blk, kvw), lambda bi, n, base=base: (bi, n, base)),
                pl.BlockSpec((None, blk, kvw), lambda bi, n, base=base: (bi, jnp.minimum(n + 1, nb - 1), base)),
            ]
            args += [p_q, p_q, p_q]
    in_specs += [pl.BlockSpec((None, c, kvw), lambda bi, n: (bi, 0, SK0 * SLAB // kvw)),
                 pl.BlockSpec((None, c, kvw), lambda bi, n: (bi, 0, SV0 * SLAB // kvw))]
    args += [p_ctx, p_ctx]
    return pl.pallas_call(
        functools.partial(_swa_kernel, band=band),
        grid=(b, nb),
        in_specs=in_specs,
        out_specs=pl.BlockSpec((None, blk, qw), lambda bi, n: (bi, n, 0)),
        out_shape=jax.ShapeDtypeStruct((b, sq, qw), BF),
        compiler_params=_params(("arbitrary", "arbitrary")),
        name="swa_attn" if band else "swa_attn_ctx",
    )(*args)


def _matmul_post_norm(operands, x_ref, gt_ref, g_ref, o_ref, f_scr, bm):
    gt, g = gt_ref[...], g_ref[...]
    rc = min(bm, ROW_CHUNK)
    for lo in range(0, bm, rc):
        rows = slice(lo, lo + rc)
        f_scr[rows, :] = sum(jnp.dot(lhs[rows, :], w[...], preferred_element_type=F32) for lhs, w in operands)
        for r in range(lo, lo + rc, NORM_CHUNK):
            sub = slice(r, r + NORM_CHUNK)
            f = f_scr[sub, :]
            ms = jnp.mean(f * f, axis=-1, keepdims=True)
            o_ref[sub, :] = x_ref[sub, :] + gt * (f * lax.rsqrt(ms + EPS) * g)


def _outproj_kernel(od_ref, os_ref, wt_ref, wb_ref, x_ref, gt_ref, g_ref, o_ref, mix_scr, *, bm):
    _matmul_post_norm(((od_ref, wt_ref), (os_ref, wb_ref)), x_ref, gt_ref, g_ref, o_ref, mix_scr, bm)


def _outproj(od, osw, w_out, x, mods, layer, row_fn, g_post, bm):
    rows, d = x.shape
    half = od.shape[1]
    return pl.pallas_call(
        functools.partial(_outproj_kernel, bm=bm),
        grid=(rows // bm,),
        in_specs=[
            pl.BlockSpec((bm, half), lambda i: (i, 0)),
            pl.BlockSpec((bm, half), lambda i: (i, 0)),
            _resident((None, half, d), lambda i: (layer, 0, 0)),
            _resident((None, half, d), lambda i: (layer, 1, 0)),
            pl.BlockSpec((bm, d), lambda i: (i, 0)),
            _mod_spec(layer, 2, row_fn, 1)(d),
            pl.BlockSpec((1, d), lambda i: (0, 0)),
        ],
        out_specs=pl.BlockSpec((bm, d), lambda i: (i, 0)),
        out_shape=jax.ShapeDtypeStruct((rows, d), F32),
        scratch_shapes=[pltpu.VMEM((bm, d), F32)],
        compiler_params=_params(("arbitrary",)),
        name="outproj",
    )(od, osw, w_out, w_out, x, mods, g_post)


HALO = 16
CONV_CHUNK = 512


def _ffn_up_kernel(x_ref, xp_ref, xn_ref, sh_ref, sc_ref, g_ref, wg_ref, wu_ref, cw_ref, cb_ref, o_ref,
                   h_scr, *, bm, seq):
    i = pl.program_id(0)
    j = pl.program_id(1)

    @pl.when(j == 0)
    def _():
        g, sh, sc = g_ref[...], sh_ref[...], sc_ref[...]

        def norm_rows(r, carry):
            rows = pl.multiple_of(r * NORM_CHUNK, NORM_CHUNK)
            h_scr[pl.ds(HALO + rows, NORM_CHUNK), :] = _rms_mod(x_ref[pl.ds(rows, NORM_CHUNK), :], g, sh, sc).astype(BF)
            return carry

        lax.fori_loop(0, bm // NORM_CHUNK, norm_rows, 0)
        prev_ok = ((i * bm) % seq != 0).astype(F32)
        next_ok = (((i + 1) * bm) % seq != 0).astype(F32)
        h_scr[0:HALO, :] = (_rms_mod(xp_ref[...], g, sh, sc) * prev_ok).astype(BF)
        h_scr[HALO + bm:2 * HALO + bm, :] = (_rms_mod(xn_ref[...], g, sh, sc) * next_ok).astype(BF)

    cw0, cw1, cw2, cb = cw_ref[0:1, :], cw_ref[1:2, :], cw_ref[2:3, :], cb_ref[...]
    rc = min(bm, CONV_CHUNK)
    pad = 8
    for r in range(bm // rc):
        lo = r * rc
        g = jnp.dot(h_scr[lo:lo + rc + 2 * HALO, :], wg_ref[...], preferred_element_type=F32)
        u = jnp.dot(h_scr[lo + HALO:lo + HALO + rc, :], wu_ref[...], preferred_element_type=F32)
        win = g[HALO - pad:HALO + rc + pad]
        below = pltpu.roll(win, 1, axis=0)[pad:pad + rc]
        above = pltpu.roll(win, rc + 2 * pad - 1, axis=0)[pad:pad + rc]
        gc = below * cw0 + g[HALO:HALO + rc] * cw1 + above * cw2 + cb
        a = gc * _sigmoid(gc) * u
        o_ref[lo:lo + rc, :] = a.astype(BF)


def _ffn_up(x, mods, layer, row_fn, g_pre, wg, wu, cw, cb, seq, bm, bn):
    rows, d = x.shape
    dff = wg.shape[2]
    hb = bm // HALO
    nh = rows // HALO
    mod = lambda k: _mod_spec(layer, k, row_fn, 2)(d)
    return pl.pallas_call(
        functools.partial(_ffn_up_kernel, bm=bm, seq=seq),
        grid=(rows // bm, dff // bn),
        in_specs=[
            pl.BlockSpec((bm, d), lambda i, j: (i, 0)),
            pl.BlockSpec((HALO, d), lambda i, j: (jnp.maximum(i * hb - 1, 0), 0)),
            pl.BlockSpec((HALO, d), lambda i, j: (jnp.minimum((i + 1) * hb, nh - 1), 0)),
            mod(3), mod(4),
            pl.BlockSpec((1, d), lambda i, j: (0, 0)),
            pl.BlockSpec((None, d, bn), lambda i, j: (layer, 0, j)),
            pl.BlockSpec((None, d, bn), lambda i, j: (layer, 0, j)),
            pl.BlockSpec((3, bn), lambda i, j: (0, j)),
            pl.BlockSpec((1, bn), lambda i, j: (0, j)),
        ],
        out_specs=pl.BlockSpec((bm, bn), lambda i, j: (i, j)),
        out_shape=jax.ShapeDtypeStruct((rows, dff), BF),
        scratch_shapes=[pltpu.VMEM((bm + 2 * HALO, d), BF)],
        compiler_params=_params(("arbitrary", "arbitrary")),
        name="ffn_up",
    )(x, x, x, mods, mods, g_pre, wg, wu, cw, cb)


def _ffn_down_kernel(a_ref, w_ref, x_ref, gt_ref, g_ref, o_ref, f_scr, *, bm):
    _matmul_post_norm(((a_ref, w_ref),), x_ref, gt_ref, g_ref, o_ref, f_scr, bm)


def _ffn_down(a, wd, x, mods, layer, row_fn, g_post, bm):
    rows, d = x.shape
    dff = a.shape[1]
    return pl.pallas_call(
        functools.partial(_ffn_down_kernel, bm=bm),
        grid=(rows // bm,),
        in_specs=[
            pl.BlockSpec((bm, dff), lambda i: (i, 0)),
            _resident((None, dff, d), lambda i: (layer, 0, 0)),
            pl.BlockSpec((bm, d), lambda i: (i, 0)),
            _mod_spec(layer, 5, row_fn, 1)(d),
            pl.BlockSpec((1, d), lambda i: (0, 0)),
        ],
        out_specs=pl.BlockSpec((bm, d), lambda i: (i, 0)),
        out_shape=jax.ShapeDtypeStruct((rows, d), F32),
        scratch_shapes=[pltpu.VMEM((bm, d), F32)],
        compiler_params=_params(("arbitrary",)),
        name="ffn_down",
    )(a, wd, x, mods, g_post)


def _rope_tables(seq, ctx_len):
    t = np.arange(seq)
    row, col = t // GRID_W, t % GRID_W

    def table(half):
        inv = ROPE_THETA ** (-np.arange(half, dtype=np.float64) / half)
        ar, ac = row[:, None] * inv[None, :], col[:, None] * inv[None, :]
        cos = np.concatenate([np.cos(ar), np.cos(ar), np.cos(ac), np.cos(ac)], axis=1)
        sin = np.concatenate([-np.sin(ar), np.sin(ar), -np.sin(ac), np.sin(ac)], axis=1)
        reps = SLAB // cos.shape[1]
        return np.tile(cos, (1, reps)).astype(np.float32), np.tile(sin, (1, reps)).astype(np.float32)

    cd, sd = table(DIFF_QK_DIM // 4)
    cs, ss = table(SWA_HEAD_DIM // 4)
    lat = tuple(jnp.asarray(a) for a in (cd, sd, cs, ss))
    one, zero = jnp.ones((ctx_len, SLAB), F32), jnp.zeros((ctx_len, SLAB), F32)
    return lat, (one, zero, one, zero)


def _pick(n, prefs):
    for p in prefs:
        if n % p == 0:
            return p
    raise ValueError(f"no block size in {prefs} divides {n}")


def kernel(x, c, ctx, c_ctx, w_ada, b_ada, g_pre_mix, g_post_mix, w_in, diff_lambda, diff_subln, swa_sink,
           w_out, g_pre_ffn, g_post_ffn, w_ffn_gate, w_ffn_up, ffn_conv_w, ffn_conv_b, w_ffn_down):
    b, s, d = x.shape
    cl = ctx.shape[1]
    depth = w_ada.shape[0]
    dff = w_ffn_gate.shape[2]
    assert w_in.shape[2] == IN_W and s % WINDOW == 0 and cl % WINDOW == 0

    n_rows = -(-(b + 1) // 8) * 8
    cvec = jnp.concatenate([c, c_ctx[None, :], jnp.zeros((n_rows - b - 1, d), F32)], axis=0)
    mods = _ada(cvec, w_ada, b_ada).reshape(depth, n_rows, 6, 1, d)

    lat_tabs, ctx_tabs = _rope_tables(s, cl)
    bm_in = _pick(s, (512, 256, 128))
    bm_out = _pick(s, (512, 256, 128))
    bm_up = _pick(s, (1024, 512, 256, 128))
    bm_dn = _pick(s, (256, 128))
    bm_c = _pick(cl, (256, 128))
    bn_ff = _pick(dff, (512, 256, 128))
    bq = _pick(s, (256, 128))
    bk = _pick(s, (256, 128))
    bq_c = _pick(cl, (256, 128))

    xl = x.reshape(b * s, d)
    xc = ctx.reshape(b * cl, d)
    ctx_row = lambda i: b

    w_in_l, w_out_l = w_in.astype(BF), w_out.astype(BF)
    wg_l, wu_l, wd_l = w_ffn_gate.astype(BF), w_ffn_up.astype(BF), w_ffn_down.astype(BF)

    for l in range(depth):
        last = l == depth - 1
        lam_init = 0.8 - 0.6 * math.exp(-0.3 * l)
        gpm, gqm = g_pre_mix[l][None, :], g_post_mix[l][None, :]
        gpf, gqf = g_pre_ffn[l][None, :], g_post_ffn[l][None, :]
        sub = diff_subln[l][None, :]
        cb = ffn_conv_b[l][None, :]

        p_lat = _inproj(xl, mods, l, lambda i, n=s // bm_in: i // n, gpm, w_in_l, lat_tabs, s, bm_in)
        p_ctx = _inproj(xc, mods, l, ctx_row, gpm, w_in_l, ctx_tabs, cl, bm_c)
        p_lat3, p_ctx3 = p_lat.reshape(b, s, IN_W), p_ctx.reshape(b, cl, IN_W)
        od = _diff_attention(p_lat3, p_lat3, p_ctx3, diff_lambda[l], sub, lam_init, bq, bk)
        osw = _swa_attention(p_lat3, p_ctx3, swa_sink[l], band=True)
        xl = _outproj(od.reshape(b * s, -1), osw.reshape(b * s, -1), w_out_l, xl, mods, l,
                      lambda i, n=s // bm_out: i // n, gqm, bm_out)
        if not last:
            odc = _diff_attention(p_ctx3, None, p_ctx3, diff_lambda[l], sub, lam_init, bq_c, bk)
            oswc = _swa_attention(p_ctx3, p_ctx3, swa_sink[l], band=False)
            xc = _outproj(odc.reshape(b * cl, -1), oswc.reshape(b * cl, -1), w_out_l, xc, mods, l, ctx_row, gqm, bm_c)

        a = _ffn_up(xl, mods, l, lambda i, n=s // bm_up: i // n, gpf, wg_l, wu_l, ffn_conv_w[l], cb, s, bm_up, bn_ff)
        xl = _ffn_down(a, wd_l, xl, mods, l, lambda i, n=s // bm_dn: i // n, gqf, bm_dn)
        if not last:
            ac = _ffn_up(xc, mods, l, ctx_row, gpf, wg_l, wu_l, ffn_conv_w[l], cb, cl, bm_c, bn_ff)
            xc = _ffn_down(ac, wd_l, xc, mods, l, ctx_row, gqf, bm_c)
    return xl.reshape(b, s, d)
```

```python
import functools
import math

import jax
import jax.numpy as jnp
import numpy as np
from jax import lax
from jax.experimental import pallas as pl
from jax.experimental.pallas import tpu as pltpu

BF = jnp.bfloat16
F32 = jnp.float32

GRID_W = 64
DIFF_HEADS = 8
DIFF_QK_DIM = 64
DIFF_V_DIM = 128
SWA_HEADS = 8
SWA_KV_HEADS = 2
SWA_GROUP = SWA_HEADS // SWA_KV_HEADS
SWA_HEAD_DIM = 128
WINDOW = 128
ROPE_THETA = 10000.0
EPS = 1e-6
NEG_INF = -1e30
LOG2E = 1.4426950408889634
DIFF_QSCALE = DIFF_QK_DIM ** -0.5 * LOG2E
SWA_QSCALE = SWA_HEAD_DIM ** -0.5 * LOG2E

SLAB = 128
DQ0, DK0, DV0, SQ0, SK0, SV0 = 0, 8, 16, 24, 32, 34
N_SLABS = 36
IN_W = N_SLABS * SLAB

VMEM_LIMIT = 56 * 1024 * 1024
NORM_CHUNK = 64
ROW_CHUNK = 256


def _params(sem, vmem=VMEM_LIMIT, flags=None):
    return pltpu.CompilerParams(dimension_semantics=sem, vmem_limit_bytes=vmem, flags=flags)


def _resident(shape, index_map):
    return pl.BlockSpec(shape, index_map, pipeline_mode=pl.Buffered(1))


def _sigmoid(v):
    return 1.0 / (1.0 + jnp.exp(-v))


def _rms_mod(xf, g, sh, sc):
    ms = jnp.mean(xf * xf, axis=-1, keepdims=True)
    return (xf * lax.rsqrt(ms + EPS) * g) * (1.0 + sc) + sh


def _rope(x, cos, sin, shift):
    lane = lax.broadcasted_iota(jnp.int32, x.shape, 1)
    fwd = pltpu.roll(x, SLAB - shift, axis=1)
    bwd = pltpu.roll(x, shift, axis=1)
    partner = jnp.where((lane & (2 * shift - 1)) < shift, fwd, bwd)
    return x * cos + partner * sin


def _ada_kernel(c_ref, w_ref, b_ref, o_ref):
    cv = c_ref[...]
    s = (cv * _sigmoid(cv)).astype(BF)
    o_ref[...] = jnp.dot(s, w_ref[...].astype(BF), preferred_element_type=F32) + b_ref[...]


def _ada(cvec, w_ada, b_ada):
    depth, d, n = w_ada.shape
    rows = cvec.shape[0]
    bn = 1024
    return pl.pallas_call(
        _ada_kernel,
        grid=(depth, n // bn),
        in_specs=[
            pl.BlockSpec((rows, d), lambda l, j: (0, 0)),
            pl.BlockSpec((None, d, bn), lambda l, j: (l, 0, j)),
            pl.BlockSpec((None, 1, bn), lambda l, j: (l, 0, j)),
        ],
        out_specs=pl.BlockSpec((None, rows, bn), lambda l, j: (l, 0, j)),
        out_shape=jax.ShapeDtypeStruct((depth, rows, n), F32),
        compiler_params=_params(("arbitrary", "arbitrary")),
        name="ada",
    )(cvec, w_ada, b_ada.reshape(depth, 1, n))


def _mod_spec(layer, k, row_fn, nargs):
    if nargs == 1:
        return lambda d: pl.BlockSpec((None, None, None, 1, d), lambda i: (layer, row_fn(i), k, 0, 0))
    return lambda d: pl.BlockSpec((None, None, None, 1, d), lambda i, j: (layer, row_fn(i), k, 0, 0))


def _inproj_kernel(x_ref, sh_ref, sc_ref, g_ref, w_ref, cd_ref, sd_ref, cs_ref, ss_ref, o_ref, h_scr, *, bm):
    g, sh, sc = g_ref[...], sh_ref[...], sc_ref[...]
    group = 4
    rc = min(bm, ROW_CHUNK)
    for lo in range(0, bm, rc):
        for r in range(lo, lo + rc, NORM_CHUNK):
            h_scr[r:r + NORM_CHUNK, :] = _rms_mod(x_ref[r:r + NORM_CHUNK, :], g, sh, sc).astype(BF)
        rows = slice(lo, lo + rc)
        for grp in range(N_SLABS // group):
            acc = jnp.dot(h_scr[rows, :], w_ref[:, grp * group * SLAB:(grp + 1) * group * SLAB],
                          preferred_element_type=F32)
            for t in range(group):
                slab = grp * group + t
                a = acc[:, t * SLAB:(t + 1) * SLAB]
                if slab < DV0:
                    a = _rope(a, cd_ref[rows, :], sd_ref[rows, :], DIFF_QK_DIM // 4)
                    if slab < DK0:
                        a = a * DIFF_QSCALE
                elif SQ0 <= slab < SV0:
                    a = _rope(a, cs_ref[rows, :], ss_ref[rows, :], SWA_HEAD_DIM // 4)
                    if slab < SK0:
                        a = a * SWA_QSCALE
                o_ref[rows, slab * SLAB:(slab + 1) * SLAB] = a.astype(BF)


def _inproj(x, mods, layer, row_fn, g_pre, w, tables, seq, bm):
    rows, d = x.shape
    nt = seq // bm
    mod = lambda k: _mod_spec(layer, k, row_fn, 1)(d)
    tab = pl.BlockSpec((bm, SLAB), lambda i: (i % nt, 0))
    return pl.pallas_call(
        functools.partial(_inproj_kernel, bm=bm),
        grid=(rows // bm,),
        in_specs=[
            pl.BlockSpec((bm, d), lambda i: (i, 0)),
            mod(0), mod(1),
            pl.BlockSpec((1, d), lambda i: (0, 0)),
            _resident((None, d, IN_W), lambda i: (layer, 0, 0)),
            tab, tab, tab, tab,
        ],
        out_specs=pl.BlockSpec((bm, IN_W), lambda i: (i, 0)),
        out_shape=jax.ShapeDtypeStruct((rows, IN_W), BF),
        scratch_shapes=[pltpu.VMEM((bm, d), BF)],
        compiler_params=_params(("arbitrary",)),
        name="inproj",
    )(x, mods, mods, g_pre, w, *tables)


ONES_ROWS = 16
DIFF_SUB_BLOCKS = 2
DIFF_UNROLL = 62


def _diff_kernel(*refs, n_lat, lam_init, unroll, n_sub):
    if n_lat:
        lam_ref, sub_ref, q_ref, kl_ref, vl_ref, kc_ref, vc_ref, o_ref = refs
    else:
        lam_ref, sub_ref, q_ref, kc_ref, vc_ref, o_ref = refs
    lp = lam_ref[...]
    lam = (jnp.exp(jnp.sum(lp[0:1] * lp[1:2], axis=-1, keepdims=True))
           - jnp.exp(jnp.sum(lp[2:3] * lp[3:4], axis=-1, keepdims=True)) + lam_init)

    bq = q_ref.shape[0] // n_sub
    n_chunks = n_lat + 1
    keys = lambda t: kc_ref[...] if isinstance(t, int) and t == n_lat else kl_ref[t]
    values = lambda t: vc_ref[...] if isinstance(t, int) and t == n_lat else vl_ref[t]

    def one_block(rows):
        q = q_ref[rows, :].astype(F32)
        lane = lax.broadcasted_iota(jnp.int32, q.shape, 1)
        q_both = jnp.concatenate([jnp.where(lane < DIFF_QK_DIM, q, 0.0), jnp.where(lane >= DIFF_QK_DIM, q, 0.0)],
                                 axis=0)
        q_t = jnp.transpose(q_both).astype(BF)

        def scores(t):
            s = jnp.dot(keys(t), q_t, preferred_element_type=F32)
            return s, jnp.max(s, axis=0, keepdims=True)

        def softmax(sc, m):
            s, smax = sc
            mn = jnp.maximum(m, smax)
            return mn, jnp.exp2(m - mn), jnp.exp2(s - mn).astype(BF)

        def accumulate(t, p, alpha, acc):
            v_t = values(t)
            v_ext = jnp.concatenate([v_t, jnp.ones((ONES_ROWS, v_t.shape[1]), BF)], axis=0)
            return alpha * acc + jnp.dot(v_ext, p, preferred_element_type=F32)

        m = jnp.full((1, 2 * bq), NEG_INF, F32)
        acc = jnp.zeros((DIFF_V_DIM + ONES_ROWS, 2 * bq), F32)

        sc = scores(0)
        if n_chunks > 1:
            sc_next = scores(1)
            m, alpha, p = softmax(sc, m)

            def step(t, carry):
                sc, p, alpha, m, acc = carry
                sc_new = scores(t)
                m_new, alpha_new, p_new = softmax(sc, m)
                acc = accumulate(t - 2, p, alpha, acc)
                return sc_new, p_new, alpha_new, m_new, acc

            def body(i, carry):
                for u in range(unroll):
                    carry = step(2 + i * unroll + u, carry)
                return carry

            trips = max(n_lat - 2, 0) // unroll
            carry = (sc_next, p, alpha, m, acc)
            if trips:
                carry = lax.fori_loop(0, trips, body, carry)
            for t in range(2 + trips * unroll, n_chunks):
                carry = step(t, carry)
            sc, p, alpha, m, acc = carry
            acc = accumulate(n_chunks - 2, p, alpha, acc)
        m, alpha, p = softmax(sc, m)
        acc = accumulate(n_chunks - 1, p, alpha, acc)

        o_t = acc[:DIFF_V_DIM] / acc[DIFF_V_DIM:DIFF_V_DIM + 1]
        o = jnp.transpose(o_t[:, :bq]) - lam * jnp.transpose(o_t[:, bq:])
        ms = jnp.mean(o * o, axis=-1, keepdims=True)
        o_ref[rows, :] = (o * lax.rsqrt(ms + EPS) * sub_ref[...] * (1.0 - lam_init)).astype(BF)

    for sb in range(n_sub):
        one_block(slice(sb * bq, (sb + 1) * bq))


def _diff_attention(p_q, p_lat, p_ctx, lam_par, subln, lam_init, bq, bk):
    b, sq, _ = p_q.shape
    c = p_ctx.shape[1]
    vw = DIFF_HEADS * DIFF_V_DIM
    n_lat = 0 if p_lat is None else p_lat.shape[1] // bk
    n_sub = DIFF_SUB_BLOCKS if sq % (DIFF_SUB_BLOCKS * bq) == 0 else 1
    qb = n_sub * bq
    in_specs = [
        pl.BlockSpec((4, DIFF_QK_DIM), lambda bi, h, qi: (0, 0)),
        pl.BlockSpec((1, DIFF_V_DIM), lambda bi, h, qi: (0, 0)),
        pl.BlockSpec((None, qb, SLAB), lambda bi, h, qi: (bi, qi, DQ0 + h)),
    ]
    args = [lam_par, subln, p_q]
    values_of = lambda p: p[:, :, DV0 * SLAB:DV0 * SLAB + vw]
    if n_lat:
        k_lat = p_lat.reshape(b, n_lat, bk, IN_W)
        v_lat = jnp.transpose(values_of(p_lat).reshape(b, n_lat, bk, DIFF_HEADS, DIFF_V_DIM), (0, 3, 1, 4, 2))
        in_specs += [pl.BlockSpec((None, n_lat, bk, SLAB), lambda bi, h, qi: (bi, 0, 0, DK0 + h)),
                     pl.BlockSpec((None, None, n_lat, DIFF_V_DIM, bk), lambda bi, h, qi: (bi, h, 0, 0, 0))]
        args += [k_lat, v_lat]
    v_ctx = jnp.transpose(values_of(p_ctx).reshape(b, c, DIFF_HEADS, DIFF_V_DIM), (0, 2, 3, 1))
    in_specs += [pl.BlockSpec((None, c, SLAB), lambda bi, h, qi: (bi, 0, DK0 + h)),
                 pl.BlockSpec((None, None, DIFF_V_DIM, c), lambda bi, h, qi: (bi, h, 0, 0))]
    args += [p_ctx, v_ctx]
    return pl.pallas_call(
        functools.partial(_diff_kernel, n_lat=n_lat, lam_init=lam_init, unroll=DIFF_UNROLL, n_sub=n_sub),
        grid=(b, DIFF_HEADS, sq // qb),
        in_specs=in_specs,
        out_specs=pl.BlockSpec((None, qb, SLAB), lambda bi, h, qi: (bi, qi, h)),
        out_shape=jax.ShapeDtypeStruct((b, sq, vw), BF),
        compiler_params=_params(("arbitrary", "arbitrary", "arbitrary")),
        name="diff_attn" if n_lat else "diff_attn_ctx",
    )(*args)


def _swa_kernel(*refs, band):
    if band:
        sink_ref, mask_ref, q_ref, kp_ref, kz_ref, kn_ref, vp_ref, vz_ref, vn_ref, kc_ref, vc_ref, o_ref = refs
    else:
        sink_ref, q_ref, kc_ref, vc_ref, o_ref = refs
    blk = q_ref.shape[0]
    rows = SWA_GROUP * blk
    rowc = lax.broadcasted_iota(jnp.int32, (rows, 1), 0)
    for g in range(SWA_KV_HEADS):
        gs = slice(g * SLAB, (g + 1) * SLAB)
        q4 = jnp.concatenate([q_ref[:, (g * SWA_GROUP + h) * SLAB:(g * SWA_GROUP + h + 1) * SLAB]
                              for h in range(SWA_GROUP)], axis=0)
        if band:
            kcat = jnp.concatenate([kp_ref[:, gs], kz_ref[:, gs], kn_ref[:, gs], kc_ref[:, gs]], axis=0)
            vcat = jnp.concatenate([vp_ref[:, gs], vz_ref[:, gs], vn_ref[:, gs], vc_ref[:, gs]], axis=0)
        else:
            kcat, vcat = kc_ref[:, gs], vc_ref[:, gs]
        s = lax.dot_general(q4, kcat, (((1,), (1,)), ((), ())), preferred_element_type=F32)
        if band:
            s = s + mask_ref[...]
        sk = jnp.full((rows, 1), sink_ref[g * SWA_GROUP + SWA_GROUP - 1], F32)
        for h in range(SWA_GROUP - 2, -1, -1):
            sk = jnp.where(rowc < (h + 1) * blk, sink_ref[g * SWA_GROUP + h], sk)
        sk = sk * LOG2E
        m = jnp.maximum(jnp.max(s, axis=-1, keepdims=True), sk)
        p = jnp.exp2(s - m)
        l = jnp.sum(p, axis=-1, keepdims=True) + jnp.exp2(sk - m)
        o = jnp.dot(p.astype(BF), vcat, preferred_element_type=F32) / l
        for h in range(SWA_GROUP):
            hs = (g * SWA_GROUP + h) * SLAB
            o_ref[:, hs:hs + SLAB] = o[h * blk:(h + 1) * blk, :].astype(BF)


def _swa_attention(p_q, p_ctx, sink, band):
    b, sq, _ = p_q.shape
    c = p_ctx.shape[1]
    blk = WINDOW
    nb = sq // blk
    qw = SWA_HEADS * SLAB
    kvw = SWA_KV_HEADS * SLAB
    in_specs = [
        pl.BlockSpec(memory_space=pltpu.SMEM),
        pl.BlockSpec((None, blk, qw), lambda bi, n: (bi, n, SQ0 * SLAB // qw)),
    ]
    args = [sink, p_q]
    if band:
        r = np.arange(SWA_GROUP * blk)[:, None] % blk
        col = np.arange(3 * blk + c)[None, :]
        in_band = (np.abs(r - (col - blk)) <= WINDOW) | (col >= 3 * blk)
        variants = [in_band & ~((col < blk) & bool(v & 1)) & ~((col >= 2 * blk) & (col < 3 * blk) & bool(v & 2))
                    for v in range(4)]
        mask = jnp.asarray(np.where(np.stack(variants), 0.0, NEG_INF).astype(np.float32))
        in_specs.insert(1, pl.BlockSpec(
            (None,) + mask.shape[1:],
            lambda bi, n: (jnp.where(n == 0, 1, 0) + jnp.where(n == nb - 1, 2, 0), 0, 0)))
        args.insert(1, mask)
        for base in (SK0 * SLAB // kvw, SV0 * SLAB // kvw):
            in_specs += [
                pl.BlockSpec((None, blk, kvw), lambda bi, n, base=base: (bi, jnp.maximum(n - 1, 0), base)),
                pl.BlockSpec((None, blk, kvw), lambda bi, n, base=base: (bi, n, base)),
                pl.BlockSpec((None, blk, kvw), lambda bi, n, base=base: (bi, jnp.minimum(n + 1, nb - 1), base)),
            ]
            args += [p_q, p_q, p_q]
    in_specs += [pl.BlockSpec((None, c, kvw), lambda bi, n: (bi, 0, SK0 * SLAB // kvw)),
                 pl.BlockSpec((None, c, kvw), lambda bi, n: (bi, 0, SV0 * SLAB // kvw))]
    args += [p_ctx, p_ctx]
    return pl.pallas_call(
        functools.partial(_swa_kernel, band=band),
        grid=(b, nb),
        in_specs=in_specs,
        out_specs=pl.BlockSpec((None, blk, qw), lambda bi, n: (bi, n, 0)),
        out_shape=jax.ShapeDtypeStruct((b, sq, qw), BF),
        compiler_params=_params(("arbitrary", "arbitrary")),
        name="swa_attn" if band else "swa_attn_ctx",
    )(*args)


def _matmul_post_norm(operands, x_ref, gt_ref, g_ref, o_ref, f_scr, bm):
    gt, g = gt_ref[...], g_ref[...]
    rc = min(bm, ROW_CHUNK)
    for lo in range(0, bm, rc):
        rows = slice(lo, lo + rc)
        f_scr[rows, :] = sum(jnp.dot(lhs[rows, :], w[...], preferred_element_type=F32) for lhs, w in operands)
        for r in range(lo, lo + rc, NORM_CHUNK):
            sub = slice(r, r + NORM_CHUNK)
            f = f_scr[sub, :]
            ms = jnp.mean(f * f, axis=-1, keepdims=True)
            o_ref[sub, :] = x_ref[sub, :] + gt * (f * lax.rsqrt(ms + EPS) * g)


def _outproj_kernel(od_ref, os_ref, wt_ref, wb_ref, x_ref, gt_ref, g_ref, o_ref, mix_scr, *, bm):
    _matmul_post_norm(((od_ref, wt_ref), (os_ref, wb_ref)), x_ref, gt_ref, g_ref, o_ref, mix_scr, bm)


def _outproj(od, osw, w_out, x, mods, layer, row_fn, g_post, bm):
    rows, d = x.shape
    half = od.shape[1]
    return pl.pallas_call(
        functools.partial(_outproj_kernel, bm=bm),
        grid=(rows // bm,),
        in_specs=[
            pl.BlockSpec((bm, half), lambda i: (i, 0)),
            pl.BlockSpec((bm, half), lambda i: (i, 0)),
            _resident((None, half, d), lambda i: (layer, 0, 0)),
            _resident((None, half, d), lambda i: (layer, 1, 0)),
            pl.BlockSpec((bm, d), lambda i: (i, 0)),
            _mod_spec(layer, 2, row_fn, 1)(d),
            pl.BlockSpec((1, d), lambda i: (0, 0)),
        ],
        out_specs=pl.BlockSpec((bm, d), lambda i: (i, 0)),
        out_shape=jax.ShapeDtypeStruct((rows, d), F32),
        scratch_shapes=[pltpu.VMEM((bm, d), F32)],
        compiler_params=_params(("arbitrary",)),
        name="outproj",
    )(od, osw, w_out, w_out, x, mods, g_post)


HALO = 16
CONV_CHUNK = 512


def _ffn_up_kernel(x_ref, xp_ref, xn_ref, sh_ref, sc_ref, g_ref, wg_ref, wu_ref, cw_ref, cb_ref, o_ref,
                   h_scr, *, bm, seq):
    i = pl.program_id(0)
    j = pl.program_id(1)

    @pl.when(j == 0)
    def _():
        g, sh, sc = g_ref[...], sh_ref[...], sc_ref[...]

        def norm_rows(r, carry):
            rows = pl.multiple_of(r * NORM_CHUNK, NORM_CHUNK)
            h_scr[pl.ds(HALO + rows, NORM_CHUNK), :] = _rms_mod(x_ref[pl.ds(rows, NORM_CHUNK), :], g, sh, sc).astype(BF)
            return carry

        lax.fori_loop(0, bm // NORM_CHUNK, norm_rows, 0)
        prev_ok = ((i * bm) % seq != 0).astype(F32)
        next_ok = (((i + 1) * bm) % seq != 0).astype(F32)
        h_scr[0:HALO, :] = (_rms_mod(xp_ref[...], g, sh, sc) * prev_ok).astype(BF)
        h_scr[HALO + bm:2 * HALO + bm, :] = (_rms_mod(xn_ref[...], g, sh, sc) * next_ok).astype(BF)

    cw0, cw1, cw2, cb = cw_ref[0:1, :], cw_ref[1:2, :], cw_ref[2:3, :], cb_ref[...]
    rc = min(bm, CONV_CHUNK)
    pad = 8
    for r in range(bm // rc):
        lo = r * rc
        g = jnp.dot(h_scr[lo:lo + rc + 2 * HALO, :], wg_ref[...], preferred_element_type=F32)
        u = jnp.dot(h_scr[lo + HALO:lo + HALO + rc, :], wu_ref[...], preferred_element_type=F32)
        win = g[HALO - pad:HALO + rc + pad]
        below = pltpu.roll(win, 1, axis=0)[pad:pad + rc]
        above = pltpu.roll(win, rc + 2 * pad - 1, axis=0)[pad:pad + rc]
        gc = below * cw0 + g[HALO:HALO + rc] * cw1 + above * cw2 + cb
        a = gc * _sigmoid(gc) * u
        o_ref[lo:lo + rc, :] = a.astype(BF)


def _ffn_up(x, mods, layer, row_fn, g_pre, wg, wu, cw, cb, seq, bm, bn):
    rows, d = x.shape
    dff = wg.shape[2]
    hb = bm // HALO
    nh = rows // HALO
    mod = lambda k: _mod_spec(layer, k, row_fn, 2)(d)
    return pl.pallas_call(
        functools.partial(_ffn_up_kernel, bm=bm, seq=seq),
        grid=(rows // bm, dff // bn),
        in_specs=[
            pl.BlockSpec((bm, d), lambda i, j: (i, 0)),
            pl.BlockSpec((HALO, d), lambda i, j: (jnp.maximum(i * hb - 1, 0), 0)),
            pl.BlockSpec((HALO, d), lambda i, j: (jnp.minimum((i + 1) * hb, nh - 1), 0)),
            mod(3), mod(4),
            pl.BlockSpec((1, d), lambda i, j: (0, 0)),
            pl.BlockSpec((None, d, bn), lambda i, j: (layer, 0, j)),
            pl.BlockSpec((None, d, bn), lambda i, j: (layer, 0, j)),
            pl.BlockSpec((3, bn), lambda i, j: (0, j)),
            pl.BlockSpec((1, bn), lambda i, j: (0, j)),
        ],
        out_specs=pl.BlockSpec((bm, bn), lambda i, j: (i, j)),
        out_shape=jax.ShapeDtypeStruct((rows, dff), BF),
        scratch_shapes=[pltpu.VMEM((bm + 2 * HALO, d), BF)],
        compiler_params=_params(("arbitrary", "arbitrary")),
        name="ffn_up",
    )(x, x, x, mods, mods, g_pre, wg, wu, cw, cb)


def _ffn_down_kernel(a_ref, w_ref, x_ref, gt_ref, g_ref, o_ref, f_scr, *, bm):
    _matmul_post_norm(((a_ref, w_ref),), x_ref, gt_ref, g_ref, o_ref, f_scr, bm)


def _ffn_down(a, wd, x, mods, layer, row_fn, g_post, bm):
    rows, d = x.shape
    dff = a.shape[1]
    return pl.pallas_call(
        functools.partial(_ffn_down_kernel, bm=bm),
        grid=(rows // bm,),
        in_specs=[
            pl.BlockSpec((bm, dff), lambda i: (i, 0)),
            _resident((None, dff, d), lambda i: (layer, 0, 0)),
            pl.BlockSpec((bm, d), lambda i: (i, 0)),
            _mod_spec(layer, 5, row_fn, 1)(d),
            pl.BlockSpec((1, d), lambda i: (0, 0)),
        ],
        out_specs=pl.BlockSpec((bm, d), lambda i: (i, 0)),
        out_shape=jax.ShapeDtypeStruct((rows, d), F32),
        scratch_shapes=[pltpu.VMEM((bm, d), F32)],
        compiler_params=_params(("arbitrary",)),
        name="ffn_down",
    )(a, wd, x, mods, g_post)


def _rope_tables(seq, ctx_len):
    t = np.arange(seq)
    row, col = t // GRID_W, t % GRID_W

    def table(half):
        inv = ROPE_THETA ** (-np.arange(half, dtype=np.float64) / half)
        ar, ac = row[:, None] * inv[None, :], col[:, None] * inv[None, :]
        cos = np.concatenate([np.cos(ar), np.cos(ar), np.cos(ac), np.cos(ac)], axis=1)
        sin = np.concatenate([-np.sin(ar), np.sin(ar), -np.sin(ac), np.sin(ac)], axis=1)
        reps = SLAB // cos.shape[1]
        return np.tile(cos, (1, reps)).astype(np.float32), np.tile(sin, (1, reps)).astype(np.float32)

    cd, sd = table(DIFF_QK_DIM // 4)
    cs, ss = table(SWA_HEAD_DIM // 4)
    lat = tuple(jnp.asarray(a) for a in (cd, sd, cs, ss))
    one, zero = jnp.ones((ctx_len, SLAB), F32), jnp.zeros((ctx_len, SLAB), F32)
    return lat, (one, zero, one, zero)


def _pick(n, prefs):
    for p in prefs:
        if n % p == 0:
            return p
    raise ValueError(f"no block size in {prefs} divides {n}")


def kernel(x, c, ctx, c_ctx, w_ada, b_ada, g_pre_mix, g_post_mix, w_in, diff_lambda, diff_subln, swa_sink,
           w_out, g_pre_ffn, g_post_ffn, w_ffn_gate, w_ffn_up, ffn_conv_w, ffn_conv_b, w_ffn_down):
    b, s, d = x.shape
    cl = ctx.shape[1]
    depth = w_ada.shape[0]
    dff = w_ffn_gate.shape[2]
    assert w_in.shape[2] == IN_W and s % WINDOW == 0 and cl % WINDOW == 0

    n_rows = -(-(b + 1) // 8) * 8
    cvec = jnp.concatenate([c, c_ctx[None, :], jnp.zeros((n_rows - b - 1, d), F32)], axis=0)
    mods = _ada(cvec, w_ada, b_ada).reshape(depth, n_rows, 6, 1, d)

    lat_tabs, ctx_tabs = _rope_tables(s, cl)
    bm_in = _pick(s, (512, 256, 128))
    bm_out = _pick(s, (512, 256, 128))
    bm_up = _pick(s, (1024, 512, 256, 128))
    bm_dn = _pick(s, (256, 128))
    bm_c = _pick(cl, (256, 128))
    bn_ff = _pick(dff, (512, 256, 128))
    bq = _pick(s, (256, 128))
    bk = _pick(s, (256, 128))
    bq_c = _pick(cl, (256, 128))

    xl = x.reshape(b * s, d)
    xc = ctx.reshape(b * cl, d)
    ctx_row = lambda i: b

    w_in_l, w_out_l = w_in.astype(BF), w_out.astype(BF)
    wg_l, wu_l, wd_l = w_ffn_gate.astype(BF), w_ffn_up.astype(BF), w_ffn_down.astype(BF)

    for l in range(depth):
        last = l == depth - 1
        lam_init = 0.8 - 0.6 * math.exp(-0.3 * l)
        gpm, gqm = g_pre_mix[l][None, :], g_post_mix[l][None, :]
        gpf, gqf = g_pre_ffn[l][None, :], g_post_ffn[l][None, :]
        sub = diff_subln[l][None, :]
        cb = ffn_conv_b[l][None, :]

        p_lat = _inproj(xl, mods, l, lambda i, n=s // bm_in: i // n, gpm, w_in_l, lat_tabs, s, bm_in)
        p_ctx = _inproj(xc, mods, l, ctx_row, gpm, w_in_l, ctx_tabs, cl, bm_c)
        p_lat3, p_ctx3 = p_lat.reshape(b, s, IN_W), p_ctx.reshape(b, cl, IN_W)
        od = _diff_attention(p_lat3, p_lat3, p_ctx3, diff_lambda[l], sub, lam_init, bq, bk)
        osw = _swa_attention(p_lat3, p_ctx3, swa_sink[l], band=True)
        xl = _outproj(od.reshape(b * s, -1), osw.reshape(b * s, -1), w_out_l, xl, mods, l,
                      lambda i, n=s // bm_out: i // n, gqm, bm_out)
        if not last:
            odc = _diff_attention(p_ctx3, None, p_ctx3, diff_lambda[l], sub, lam_init, bq_c, bk)
            oswc = _swa_attention(p_ctx3, p_ctx3, swa_sink[l], band=False)
            xc = _outproj(odc.reshape(b * cl, -1), oswc.reshape(b * cl, -1), w_out_l, xc, mods, l, ctx_row, gqm, bm_c)

        a = _ffn_up(xl, mods, l, lambda i, n=s // bm_up: i // n, gpf, wg_l, wu_l, ffn_conv_w[l], cb, s, bm_up, bn_ff)
        xl = _ffn_down(a, wd_l, xl, mods, l, lambda i, n=s // bm_dn: i // n, gqf, bm_dn)
        if not last:
            ac = _ffn_up(xc, mods, l, ctx_row, gpf, wg_l, wu_l, ffn_conv_w[l], cb, cl, bm_c, bn_ff)
            xc = _ffn_down(ac, wd_l, xc, mods, l, ctx_row, gqf, bm_c)
    return xl.reshape(b, s, d)
```

```python
import functools
import math

import jax
import jax.numpy as jnp
import numpy as np
from jax import lax
from jax.experimental import pallas as pl
from jax.experimental.pallas import tpu as pltpu

BF = jnp.bfloat16
F32 = jnp.float32

GRID_W = 64
DIFF_HEADS = 8
DIFF_QK_DIM = 64
DIFF_V_DIM = 128
SWA_HEADS = 8
SWA_KV_HEADS = 2
SWA_GROUP = SWA_HEADS // SWA_KV_HEADS
SWA_HEAD_DIM = 128
WINDOW = 128
ROPE_THETA = 10000.0
EPS = 1e-6
NEG_INF = -1e30
LOG2E = 1.4426950408889634
DIFF_QSCALE = DIFF_QK_DIM ** -0.5 * LOG2E
SWA_QSCALE = SWA_HEAD_DIM ** -0.5 * LOG2E

SLAB = 128
DQ0, DK0, DV0, SQ0, SK0, SV0 = 0, 8, 16, 24, 32, 34
N_SLABS = 36
IN_W = N_SLABS * SLAB

VMEM_LIMIT = 56 * 1024 * 1024
NORM_CHUNK = 64
ROW_CHUNK = 256


def _params(sem, vmem=VMEM_LIMIT, flags=None):
    return pltpu.CompilerParams(dimension_semantics=sem, vmem_limit_bytes=vmem, flags=flags)


def _resident(shape, index_map):
    return pl.BlockSpec(shape, index_map, pipeline_mode=pl.Buffered(1))


def _sigmoid(v):
    return 1.0 / (1.0 + jnp.exp(-v))


def _rms_mod(xf, g, sh, sc):
    ms = jnp.mean(xf * xf, axis=-1, keepdims=True)
    return (xf * lax.rsqrt(ms + EPS) * g) * (1.0 + sc) + sh


def _rope(x, cos, sin, shift):
    lane = lax.broadcasted_iota(jnp.int32, x.shape, 1)
    fwd = pltpu.roll(x, SLAB - shift, axis=1)
    bwd = pltpu.roll(x, shift, axis=1)
    partner = jnp.where((lane & (2 * shift - 1)) < shift, fwd, bwd)
    return x * cos + partner * sin


def _ada_kernel(c_ref, w_ref, b_ref, o_ref):
    cv = c_ref[...]
    s = (cv * _sigmoid(cv)).astype(BF)
    o_ref[...] = jnp.dot(s, w_ref[...].astype(BF), preferred_element_type=F32) + b_ref[...]


def _ada(cvec, w_ada, b_ada):
    depth, d, n = w_ada.shape
    rows = cvec.shape[0]
    bn = 1024
    return pl.pallas_call(
        _ada_kernel,
        grid=(depth, n // bn),
        in_specs=[
            pl.BlockSpec((rows, d), lambda l, j: (0, 0)),
            pl.BlockSpec((None, d, bn), lambda l, j: (l, 0, j)),
            pl.BlockSpec((None, 1, bn), lambda l, j: (l, 0, j)),
        ],
        out_specs=pl.BlockSpec((None, rows, bn), lambda l, j: (l, 0, j)),
        out_shape=jax.ShapeDtypeStruct((depth, rows, n), F32),
        compiler_params=_params(("arbitrary", "arbitrary")),
        name="ada",
    )(cvec, w_ada, b_ada.reshape(depth, 1, n))


def _mod_spec(layer, k, row_fn, nargs):
    if nargs == 1:
        return lambda d: pl.BlockSpec((None, None, None, 1, d), lambda i: (layer, row_fn(i), k, 0, 0))
    return lambda d: pl.BlockSpec((None, None, None, 1, d), lambda i, j: (layer, row_fn(i), k, 0, 0))


def _inproj_kernel(x_ref, sh_ref, sc_ref, g_ref, w_ref, cd_ref, sd_ref, cs_ref, ss_ref, o_ref, h_scr, *, bm):
    g, sh, sc = g_ref[...], sh_ref[...], sc_ref[...]
    group = 4
    rc = min(bm, ROW_CHUNK)
    for lo in range(0, bm, rc):
        for r in range(lo, lo + rc, NORM_CHUNK):
            h_scr[r:r + NORM_CHUNK, :] = _rms_mod(x_ref[r:r + NORM_CHUNK, :], g, sh, sc).astype(BF)
        rows = slice(lo, lo + rc)
        for grp in range(N_SLABS // group):
            acc = jnp.dot(h_scr[rows, :], w_ref[:, grp * group * SLAB:(grp + 1) * group * SLAB],
                          preferred_element_type=F32)
            for t in range(group):
                slab = grp * group + t
                a = acc[:, t * SLAB:(t + 1) * SLAB]
                if slab < DV0:
                    a = _rope(a, cd_ref[rows, :], sd_ref[rows, :], DIFF_QK_DIM // 4)
                    if slab < DK0:
                        a = a * DIFF_QSCALE
                elif SQ0 <= slab < SV0:
                    a = _rope(a, cs_ref[rows, :], ss_ref[rows, :], SWA_HEAD_DIM // 4)
                    if slab < SK0:
                        a = a * SWA_QSCALE
                o_ref[rows, slab * SLAB:(slab + 1) * SLAB] = a.astype(BF)


def _inproj(x, mods, layer, row_fn, g_pre, w, tables, seq, bm):
    rows, d = x.shape
    nt = seq // bm
    mod = lambda k: _mod_spec(layer, k, row_fn, 1)(d)
    tab = pl.BlockSpec((bm, SLAB), lambda i: (i % nt, 0))
    return pl.pallas_call(
        functools.partial(_inproj_kernel, bm=bm),
        grid=(rows // bm,),
        in_specs=[
            pl.BlockSpec((bm, d), lambda i: (i, 0)),
            mod(0), mod(1),
            pl.BlockSpec((1, d), lambda i: (0, 0)),
            _resident((None, d, IN_W), lambda i: (layer, 0, 0)),
            tab, tab, tab, tab,
        ],
        out_specs=pl.BlockSpec((bm, IN_W), lambda i: (i, 0)),
        out_shape=jax.ShapeDtypeStruct((rows, IN_W), BF),
        scratch_shapes=[pltpu.VMEM((bm, d), BF)],
        compiler_params=_params(("arbitrary",)),
        name="inproj",
    )(x, mods, mods, g_pre, w, *tables)


ONES_ROWS = 16
DIFF_SUB_BLOCKS = 2
DIFF_UNROLL = 64
DIFF_HAZARD = 64.0


def _diff_kernel(*refs, n_lat, unroll, n_sub, lagged):
    n_out = 2 if lagged else 1
    ins, outs = refs[:-n_out], refs[-n_out:]
    if n_lat:
        lam_ref, sub_ref, q_ref, kl_ref, vl_ref, kc_ref, vc_ref = ins
    else:
        lam_ref, sub_ref, q_ref, kc_ref, vc_ref = ins
    o_ref = outs[0]
    lp = lam_ref[...]
    lam_init = lp[4:5, 0:1]
    lam = (jnp.exp(jnp.sum(lp[0:1] * lp[1:2], axis=-1, keepdims=True))
           - jnp.exp(jnp.sum(lp[2:3] * lp[3:4], axis=-1, keepdims=True)) + lam_init)

    bq = q_ref.shape[0] // n_sub
    n_chunks = n_lat + 1
    keys = lambda t: kc_ref[...] if isinstance(t, int) and t == n_lat else kl_ref[t]
    values = lambda t: vc_ref[...] if isinstance(t, int) and t == n_lat else vl_ref[t]

    def run_steps(step, carry, first, n_loop):
        trips = n_loop // unroll

        def body(i, carry):
            for u in range(unroll):
                carry = step(first + i * unroll + u, carry)
            return carry

        if trips:
            carry = lax.fori_loop(0, trips, body, carry)
        for t in range(first + trips * unroll, n_chunks):
            carry = step(t, carry)
        return carry

    def one_block(sb):
        rows = slice(sb * bq, (sb + 1) * bq)
        q = q_ref[rows, :].astype(F32)
        lane = lax.broadcasted_iota(jnp.int32, q.shape, 1)
        q_both = jnp.concatenate([jnp.where(lane < DIFF_QK_DIM, q, 0.0), jnp.where(lane >= DIFF_QK_DIM, q, 0.0)],
                                 axis=0)
        q_t = jnp.transpose(q_both).astype(BF)

        def qk(t):
            return jnp.dot(keys(t), q_t, preferred_element_type=F32)

        def pv(t, p):
            v_t = values(t)
            v_ext = jnp.concatenate([v_t, jnp.ones((ONES_ROWS, v_t.shape[1]), BF)], axis=0)
            return jnp.dot(v_ext, p, preferred_element_type=F32)

        colmax = lambda s: jnp.max(s, axis=0, keepdims=True)

        if lagged:
            s = qk(0)
            r = colmax(s)
            acc = jnp.zeros((DIFF_V_DIM + ONES_ROWS, 2 * bq), F32)

            def absorb(t, s, r_prev, r, acc, excess):
                p = jnp.exp2(s - r).astype(BF)
                cm = colmax(s)
                acc = jnp.exp2(r_prev - r) * acc + pv(t, p)
                return r, jnp.maximum(r, cm), acc, jnp.maximum(excess, cm - r)

            def step(t, carry):
                s, state = carry
                s_new = qk(t)
                return s_new, absorb(t - 1, s, *state)

            carry = (s, (r, r, acc, jnp.zeros_like(r)))
            s, state = run_steps(step, carry, 1, max(n_lat - 1, 0))
            _, _, acc, excess = absorb(n_chunks - 1, s, *state)
            outs[1][sb:sb + 1, :] = excess
        else:
            def scores(t):
                s = qk(t)
                return s, colmax(s)

            def softmax(sc, m):
                s, smax = sc
                mn = jnp.maximum(m, smax)
                return mn, jnp.exp2(m - mn), jnp.exp2(s - mn).astype(BF)

            m = jnp.full((1, 2 * bq), NEG_INF, F32)
            acc = jnp.zeros((DIFF_V_DIM + ONES_ROWS, 2 * bq), F32)
            sc = scores(0)
            if n_chunks > 1:
                sc_next = scores(1)
                m, alpha, p = softmax(sc, m)

                def step(t, carry):
                    sc, p, alpha, m, acc = carry
                    sc_new = scores(t)
                    m_new, alpha_new, p_new = softmax(sc, m)
                    acc = alpha * acc + pv(t - 2, p)
                    return sc_new, p_new, alpha_new, m_new, acc

                sc, p, alpha, m, acc = run_steps(step, (sc_next, p, alpha, m, acc), 2, max(n_lat - 2, 0))
                acc = alpha * acc + pv(n_chunks - 2, p)
            m, alpha, p = softmax(sc, m)
            acc = alpha * acc + pv(n_chunks - 1, p)

        o_t = acc[:DIFF_V_DIM] / acc[DIFF_V_DIM:DIFF_V_DIM + 1]
        o = jnp.transpose(o_t[:, :bq]) - lam * jnp.transpose(o_t[:, bq:])
        ms = jnp.mean(o * o, axis=-1, keepdims=True)
        o_ref[rows, :] = (o * lax.rsqrt(ms + EPS) * sub_ref[...] * (1.0 - lam_init)).astype(BF)

    for sb in range(n_sub):
        one_block(sb)


def _diff_attention(p_q, p_lat, p_ctx, lam_par, subln, bq, bk):
    b, sq, _ = p_q.shape
    c = p_ctx.shape[1]
    vw = DIFF_HEADS * DIFF_V_DIM
    n_lat = 0 if p_lat is None else p_lat.shape[1] // bk
    n_sub = DIFF_SUB_BLOCKS if sq % (DIFF_SUB_BLOCKS * bq) == 0 else 1
    qb = n_sub * bq
    in_specs = [
        pl.BlockSpec(lam_par.shape, lambda bi, h, qi: (0, 0)),
        pl.BlockSpec((1, DIFF_V_DIM), lambda bi, h, qi: (0, 0)),
        pl.BlockSpec((None, qb, SLAB), lambda bi, h, qi: (bi, qi, DQ0 + h)),
    ]
    args = [lam_par, subln, p_q]
    values_of = lambda p: p[:, :, DV0 * SLAB:DV0 * SLAB + vw]
    if n_lat:
        k_lat = p_lat.reshape(b, n_lat, bk, IN_W)
        v_lat = jnp.transpose(values_of(p_lat).reshape(b, n_lat, bk, DIFF_HEADS, DIFF_V_DIM), (0, 3, 1, 4, 2))
        in_specs += [pl.BlockSpec((None, n_lat, bk, SLAB), lambda bi, h, qi: (bi, 0, 0, DK0 + h)),
                     pl.BlockSpec((None, None, n_lat, DIFF_V_DIM, bk), lambda bi, h, qi: (bi, h, 0, 0, 0))]
        args += [k_lat, v_lat]
    v_ctx = jnp.transpose(values_of(p_ctx).reshape(b, c, DIFF_HEADS, DIFF_V_DIM), (0, 2, 3, 1))
    in_specs += [pl.BlockSpec((None, c, SLAB), lambda bi, h, qi: (bi, 0, DK0 + h)),
                 pl.BlockSpec((None, None, DIFF_V_DIM, c), lambda bi, h, qi: (bi, h, 0, 0))]
    args += [p_ctx, v_ctx]
    nq = sq // qb

    def call(lagged):
        out_specs = [pl.BlockSpec((None, qb, SLAB), lambda bi, h, qi: (bi, qi, h))]
        out_shape = [jax.ShapeDtypeStruct((b, sq, vw), BF)]
        if lagged:
            out_specs.append(pl.BlockSpec((None, None, None, n_sub, 2 * bq), lambda bi, h, qi: (bi, h, qi, 0, 0)))
            out_shape.append(jax.ShapeDtypeStruct((b, DIFF_HEADS, nq, n_sub, 2 * bq), F32))
        return pl.pallas_call(
            functools.partial(_diff_kernel, n_lat=n_lat, unroll=DIFF_UNROLL, n_sub=n_sub, lagged=lagged),
            grid=(b, DIFF_HEADS, nq),
            in_specs=in_specs,
            out_specs=out_specs,
            out_shape=out_shape,
            compiler_params=_params(("arbitrary", "arbitrary", "arbitrary")),
            name=("diff_attn_lagged" if lagged else "diff_attn") if n_lat else "diff_attn_ctx",
        )(*args)

    if not n_lat:
        return call(False)[0]
    od, excess = call(True)
    safe = jnp.max(excess) <= DIFF_HAZARD
    return lax.cond(safe, lambda: od, lambda: call(False)[0])


def _swa_kernel(*refs, band):
    if band:
        sink_ref, mask_ref, q_ref, kp_ref, kz_ref, kn_ref, vp_ref, vz_ref, vn_ref, kc_ref, vc_ref, o_ref = refs
    else:
        sink_ref, q_ref, kc_ref, vc_ref, o_ref = refs
    blk = q_ref.shape[0]
    rows = SWA_GROUP * blk
    rowc = lax.broadcasted_iota(jnp.int32, (rows, 1), 0)
    for g in range(SWA_KV_HEADS):
        gs = slice(g * SLAB, (g + 1) * SLAB)
        q4 = jnp.concatenate([q_ref[:, (g * SWA_GROUP + h) * SLAB:(g * SWA_GROUP + h + 1) * SLAB]
                              for h in range(SWA_GROUP)], axis=0)
        if band:
            kcat = jnp.concatenate([kp_ref[:, gs], kz_ref[:, gs], kn_ref[:, gs], kc_ref[:, gs]], axis=0)
            vcat = jnp.concatenate([vp_ref[:, gs], vz_ref[:, gs], vn_ref[:, gs], vc_ref[:, gs]], axis=0)
        else:
            kcat, vcat = kc_ref[:, gs], vc_ref[:, gs]
        s = lax.dot_general(q4, kcat, (((1,), (1,)), ((), ())), preferred_element_type=F32)
        if band:
            s = s + mask_ref[...]
        sk = jnp.full((rows, 1), sink_ref[g * SWA_GROUP + SWA_GROUP - 1], F32)
        for h in range(SWA_GROUP - 2, -1, -1):
            sk = jnp.where(rowc < (h + 1) * blk, sink_ref[g * SWA_GROUP + h], sk)
        sk = sk * LOG2E
        m = jnp.maximum(jnp.max(s, axis=-1, keepdims=True), sk)
        p = jnp.exp2(s - m)
        l = jnp.sum(p, axis=-1, keepdims=True) + jnp.exp2(sk - m)
        o = jnp.dot(p.astype(BF), vcat, preferred_element_type=F32) / l
        for h in range(SWA_GROUP):
            hs = (g * SWA_GROUP + h) * SLAB
            o_ref[:, hs:hs + SLAB] = o[h * blk:(h + 1) * blk, :].astype(BF)


def _swa_attention(p_q, p_ctx, sink, band):
    b, sq, _ = p_q.shape
    c = p_ctx.shape[1]
    blk = WINDOW
    nb = sq // blk
    qw = SWA_HEADS * SLAB
    kvw = SWA_KV_HEADS * SLAB
    in_specs = [
        pl.BlockSpec(memory_space=pltpu.SMEM),
        pl.BlockSpec((None, blk, qw), lambda bi, n: (bi, n, SQ0 * SLAB // qw)),
    ]
    args = [sink, p_q]
    if band:
        r = np.arange(SWA_GROUP * blk)[:, None] % blk
        col = np.arange(3 * blk + c)[None, :]
        in_band = (np.abs(r - (col - blk)) <= WINDOW) | (col >= 3 * blk)
        variants = [in_band & ~((col < blk) & bool(v & 1)) & ~((col >= 2 * blk) & (col < 3 * blk) & bool(v & 2))
                    for v in range(4)]
        mask = jnp.asarray(np.where(np.stack(variants), 0.0, NEG_INF).astype(np.float32))
        in_specs.insert(1, pl.BlockSpec(
            (None,) + mask.shape[1:],
            lambda bi, n: (jnp.where(n == 0, 1, 0) + jnp.where(n == nb - 1, 2, 0), 0, 0)))
        args.insert(1, mask)
        for base in (SK0 * SLAB // kvw, SV0 * SLAB // kvw):
            in_specs += [
                pl.BlockSpec((None, blk, kvw), lambda bi, n, base=base: (bi, jnp.maximum(n - 1, 0), base)),
                pl.BlockSpec((None, blk, kvw), lambda bi, n, base=base: (bi, n, base)),
                pl.BlockSpec((None, blk, kvw), lambda bi, n, base=base: (bi, jnp.minimum(n + 1, nb - 1), base)),
            ]
            args += [p_q, p_q, p_q]
    in_specs += [pl.BlockSpec((None, c, kvw), lambda bi, n: (bi, 0, SK0 * SLAB // kvw)),
                 pl.BlockSpec((None, c, kvw), lambda bi, n: (bi, 0, SV0 * SLAB // kvw))]
    args += [p_ctx, p_ctx]
    return pl.pallas_call(
        functools.partial(_swa_kernel, band=band),
        grid=(b, nb),
        in_specs=in_specs,
        out_specs=pl.BlockSpec((None, blk, qw), lambda bi, n: (bi, n, 0)),
        out_shape=jax.ShapeDtypeStruct((b, sq, qw), BF),
        compiler_params=_params(("arbitrary", "arbitrary")),
        name="swa_attn" if band else "swa_attn_ctx",
    )(*args)


def _matmul_post_norm(operands, x_ref, gt_ref, g_ref, o_ref, f_scr, bm):
    gt, g = gt_ref[...], g_ref[...]
    rc = min(bm, ROW_CHUNK)
    for lo in range(0, bm, rc):
        rows = slice(lo, lo + rc)
        f_scr[rows, :] = sum(jnp.dot(lhs[rows, :], w[...], preferred_element_type=F32) for lhs, w in operands)
        for r in range(lo, lo + rc, NORM_CHUNK):
            sub = slice(r, r + NORM_CHUNK)
            f = f_scr[sub, :]
            ms = jnp.mean(f * f, axis=-1, keepdims=True)
            o_ref[sub, :] = x_ref[sub, :] + gt * (f * lax.rsqrt(ms + EPS) * g)


def _outproj_kernel(od_ref, os_ref, wt_ref, wb_ref, x_ref, gt_ref, g_ref, o_ref, mix_scr, *, bm):
    _matmul_post_norm(((od_ref, wt_ref), (os_ref, wb_ref)), x_ref, gt_ref, g_ref, o_ref, mix_scr, bm)


def _outproj(od, osw, w_out, x, mods, layer, row_fn, g_post, bm):
    rows, d = x.shape
    half = od.shape[1]
    return pl.pallas_call(
        functools.partial(_outproj_kernel, bm=bm),
        grid=(rows // bm,),
        in_specs=[
            pl.BlockSpec((bm, half), lambda i: (i, 0)),
            pl.BlockSpec((bm, half), lambda i: (i, 0)),
            _resident((None, half, d), lambda i: (layer, 0, 0)),
            _resident((None, half, d), lambda i: (layer, 1, 0)),
            pl.BlockSpec((bm, d), lambda i: (i, 0)),
            _mod_spec(layer, 2, row_fn, 1)(d),
            pl.BlockSpec((1, d), lambda i: (0, 0)),
        ],
        out_specs=pl.BlockSpec((bm, d), lambda i: (i, 0)),
        out_shape=jax.ShapeDtypeStruct((rows, d), F32),
        scratch_shapes=[pltpu.VMEM((bm, d), F32)],
        compiler_params=_params(("arbitrary",)),
        name="outproj",
    )(od, osw, w_out, w_out, x, mods, g_post)


HALO = 16
CONV_CHUNK = 512


def _ffn_up_kernel(x_ref, xp_ref, xn_ref, sh_ref, sc_ref, g_ref, wg_ref, wu_ref, cw_ref, cb_ref, o_ref,
                   h_scr, *, bm, seq):
    i = pl.program_id(0)
    j = pl.program_id(1)

    @pl.when(j == 0)
    def _():
        g, sh, sc = g_ref[...], sh_ref[...], sc_ref[...]

        def norm_rows(r, carry):
            rows = pl.multiple_of(r * NORM_CHUNK, NORM_CHUNK)
            h_scr[pl.ds(HALO + rows, NORM_CHUNK), :] = _rms_mod(x_ref[pl.ds(rows, NORM_CHUNK), :], g, sh, sc).astype(BF)
            return carry

        lax.fori_loop(0, bm // NORM_CHUNK, norm_rows, 0)
        prev_ok = ((i * bm) % seq != 0).astype(F32)
        next_ok = (((i + 1) * bm) % seq != 0).astype(F32)
        h_scr[0:HALO, :] = (_rms_mod(xp_ref[...], g, sh, sc) * prev_ok).astype(BF)
        h_scr[HALO + bm:2 * HALO + bm, :] = (_rms_mod(xn_ref[...], g, sh, sc) * next_ok).astype(BF)

    cw0, cw1, cw2, cb = cw_ref[0:1, :], cw_ref[1:2, :], cw_ref[2:3, :], cb_ref[...]
    rc = min(bm, CONV_CHUNK)
    pad = 8
    for r in range(bm // rc):
        lo = r * rc
        g = jnp.dot(h_scr[lo:lo + rc + 2 * HALO, :], wg_ref[...], preferred_element_type=F32)
        u = jnp.dot(h_scr[lo + HALO:lo + HALO + rc, :], wu_ref[...], preferred_element_type=F32)
        win = g[HALO - pad:HALO + rc + pad]
        below = pltpu.roll(win, 1, axis=0)[pad:pad + rc]
        above = pltpu.roll(win, rc + 2 * pad - 1, axis=0)[pad:pad + rc]
        gc = below * cw0 + g[HALO:HALO + rc] * cw1 + above * cw2 + cb
        a = gc * _sigmoid(gc) * u
        o_ref[lo:lo + rc, :] = a.astype(BF)


def _ffn_up(x, mods, layer, row_fn, g_pre, wg, wu, cw, cb, seq, bm, bn):
    rows, d = x.shape
    dff = wg.shape[2]
    hb = bm // HALO
    nh = rows // HALO
    mod = lambda k: _mod_spec(layer, k, row_fn, 2)(d)
    return pl.pallas_call(
        functools.partial(_ffn_up_kernel, bm=bm, seq=seq),
        grid=(rows // bm, dff // bn),
        in_specs=[
            pl.BlockSpec((bm, d), lambda i, j: (i, 0)),
            pl.BlockSpec((HALO, d), lambda i, j: (jnp.maximum(i * hb - 1, 0), 0)),
            pl.BlockSpec((HALO, d), lambda i, j: (jnp.minimum((i + 1) * hb, nh - 1), 0)),
            mod(3), mod(4),
            pl.BlockSpec((1, d), lambda i, j: (0, 0)),
            pl.BlockSpec((None, d, bn), lambda i, j: (layer, 0, j)),
            pl.BlockSpec((None, d, bn), lambda i, j: (layer, 0, j)),
            pl.BlockSpec((3, bn), lambda i, j: (0, j)),
            pl.BlockSpec((1, bn), lambda i, j: (0, j)),
        ],
        out_specs=pl.BlockSpec((bm, bn), lambda i, j: (i, j)),
        out_shape=jax.ShapeDtypeStruct((rows, dff), BF),
        scratch_shapes=[pltpu.VMEM((bm + 2 * HALO, d), BF)],
        compiler_params=_params(("arbitrary", "arbitrary")),
        name="ffn_up",
    )(x, x, x, mods, mods, g_pre, wg, wu, cw, cb)


def _ffn_down_kernel(a_ref, w_ref, x_ref, gt_ref, g_ref, o_ref, f_scr, *, bm):
    _matmul_post_norm(((a_ref, w_ref),), x_ref, gt_ref, g_ref, o_ref, f_scr, bm)


def _ffn_down(a, wd, x, mods, layer, row_fn, g_post, bm):
    rows, d = x.shape
    dff = a.shape[1]
    return pl.pallas_call(
        functools.partial(_ffn_down_kernel, bm=bm),
        grid=(rows // bm,),
        in_specs=[
            pl.BlockSpec((bm, dff), lambda i: (i, 0)),
            _resident((None, dff, d), lambda i: (layer, 0, 0)),
            pl.BlockSpec((bm, d), lambda i: (i, 0)),
            _mod_spec(layer, 5, row_fn, 1)(d),
            pl.BlockSpec((1, d), lambda i: (0, 0)),
        ],
        out_specs=pl.BlockSpec((bm, d), lambda i: (i, 0)),
        out_shape=jax.ShapeDtypeStruct((rows, d), F32),
        scratch_shapes=[pltpu.VMEM((bm, d), F32)],
        compiler_params=_params(("arbitrary",)),
        name="ffn_down",
    )(a, wd, x, mods, g_post)


def _rope_tables(seq, ctx_len):
    t = np.arange(seq)
    row, col = t // GRID_W, t % GRID_W

    def table(half):
        inv = ROPE_THETA ** (-np.arange(half, dtype=np.float64) / half)
        ar, ac = row[:, None] * inv[None, :], col[:, None] * inv[None, :]
        cos = np.concatenate([np.cos(ar), np.cos(ar), np.cos(ac), np.cos(ac)], axis=1)
        sin = np.concatenate([-np.sin(ar), np.sin(ar), -np.sin(ac), np.sin(ac)], axis=1)
        reps = SLAB // cos.shape[1]
        return np.tile(cos, (1, reps)).astype(np.float32), np.tile(sin, (1, reps)).astype(np.float32)

    cd, sd = table(DIFF_QK_DIM // 4)
    cs, ss = table(SWA_HEAD_DIM // 4)
    lat = tuple(jnp.asarray(a) for a in (cd, sd, cs, ss))
    one, zero = jnp.ones((ctx_len, SLAB), F32), jnp.zeros((ctx_len, SLAB), F32)
    return lat, (one, zero, one, zero)


def _pick(n, prefs):
    for p in prefs:
        if n % p == 0:
            return p
    raise ValueError(f"no block size in {prefs} divides {n}")


def kernel(x, c, ctx, c_ctx, w_ada, b_ada, g_pre_mix, g_post_mix, w_in, diff_lambda, diff_subln, swa_sink,
           w_out, g_pre_ffn, g_post_ffn, w_ffn_gate, w_ffn_up, ffn_conv_w, ffn_conv_b, w_ffn_down):
    b, s, d = x.shape
    cl = ctx.shape[1]
    depth = w_ada.shape[0]
    dff = w_ffn_gate.shape[2]
    assert w_in.shape[2] == IN_W and s % WINDOW == 0 and cl % WINDOW == 0

    n_rows = -(-(b + 1) // 8) * 8
    cvec = jnp.concatenate([c, c_ctx[None, :], jnp.zeros((n_rows - b - 1, d), F32)], axis=0)
    mods = _ada(cvec, w_ada, b_ada).reshape(depth, n_rows, 6, 1, d)

    lat_tabs, ctx_tabs = _rope_tables(s, cl)
    bm_in = _pick(s, (512, 256, 128))
    bm_out = _pick(s, (512, 256, 128))
    bm_up = _pick(s, (1024, 512, 256, 128))
    bm_dn = _pick(s, (256, 128))
    bm_c = _pick(cl, (256, 128))
    bn_ff = _pick(dff, (512, 256, 128))
    bq = _pick(s, (256, 128))
    bk = _pick(s, (256, 128))
    bq_c = _pick(cl, (256, 128))

    xl = x.reshape(b * s, d)
    xc = ctx.reshape(b * cl, d)
    ctx_row = lambda i: b

    w_in_l, w_out_l = w_in.astype(BF), w_out.astype(BF)
    wg_l, wu_l, wd_l = w_ffn_gate.astype(BF), w_ffn_up.astype(BF), w_ffn_down.astype(BF)

    for l in range(depth):
        last = l == depth - 1
        lam_init = 0.8 - 0.6 * math.exp(-0.3 * l)
        lam_par = jnp.concatenate([diff_lambda[l], jnp.full_like(diff_lambda[l], lam_init)], axis=0)
        gpm, gqm = g_pre_mix[l][None, :], g_post_mix[l][None, :]
        gpf, gqf = g_pre_ffn[l][None, :], g_post_ffn[l][None, :]
        sub = diff_subln[l][None, :]
        cb = ffn_conv_b[l][None, :]

        p_lat = _inproj(xl, mods, l, lambda i, n=s // bm_in: i // n, gpm, w_in_l, lat_tabs, s, bm_in)
        p_ctx = _inproj(xc, mods, l, ctx_row, gpm, w_in_l, ctx_tabs, cl, bm_c)
        p_lat3, p_ctx3 = p_lat.reshape(b, s, IN_W), p_ctx.reshape(b, cl, IN_W)
        od = _diff_attention(p_lat3, p_lat3, p_ctx3, lam_par, sub, bq, bk)
        osw = _swa_attention(p_lat3, p_ctx3, swa_sink[l], band=True)
        xl = _outproj(od.reshape(b * s, -1), osw.reshape(b * s, -1), w_out_l, xl, mods, l,
                      lambda i, n=s // bm_out: i // n, gqm, bm_out)
        if not last:
            odc = _diff_attention(p_ctx3, None, p_ctx3, lam_par, sub, bq_c, bk)
            oswc = _swa_attention(p_ctx3, p_ctx3, swa_sink[l], band=False)
            xc = _outproj(odc.reshape(b * cl, -1), oswc.reshape(b * cl, -1), w_out_l, xc, mods, l, ctx_row, gqm, bm_c)

        a = _ffn_up(xl, mods, l, lambda i, n=s // bm_up: i // n, gpf, wg_l, wu_l, ffn_conv_w[l], cb, s, bm_up, bn_ff)
        xl = _ffn_down(a, wd_l, xl, mods, l, lambda i, n=s // bm_dn: i // n, gqf, bm_dn)
        if not last:
            ac = _ffn_up(xc, mods, l, ctx_row, gpf, wg_l, wu_l, ffn_conv_w[l], cb, cl, bm_c, bn_ff)
            xc = _ffn_down(ac, wd_l, xc, mods, l, ctx_row, gqf, bm_c)
    return xl.reshape(b, s, d)
```

```python
import functools
import math

import jax
import jax.numpy as jnp
import numpy as np
from jax import lax
from jax.experimental import pallas as pl
from jax.experimental.pallas import tpu as pltpu

BF = jnp.bfloat16
F32 = jnp.float32

GRID_W = 64
DIFF_HEADS = 8
DIFF_QK_DIM = 64
DIFF_V_DIM = 128
SWA_HEADS = 8
SWA_KV_HEADS = 2
SWA_GROUP = SWA_HEADS // SWA_KV_HEADS
SWA_HEAD_DIM = 128
WINDOW = 128
ROPE_THETA = 10000.0
EPS = 1e-6
NEG_INF = -1e30
LOG2E = 1.4426950408889634
DIFF_QSCALE = DIFF_QK_DIM ** -0.5 * LOG2E
SWA_QSCALE = SWA_HEAD_DIM ** -0.5 * LOG2E

SLAB = 128
DQ0, DK0, DV0, SQ0, SK0, SV0 = 0, 8, 16, 24, 32, 34
N_SLABS = 36
IN_W = N_SLABS * SLAB

VMEM_LIMIT = 56 * 1024 * 1024
NORM_CHUNK = 64
ROW_CHUNK = 256


def _params(sem, vmem=VMEM_LIMIT, flags=None):
    return pltpu.CompilerParams(dimension_semantics=sem, vmem_limit_bytes=vmem, flags=flags)


def _resident(shape, index_map):
    return pl.BlockSpec(shape, index_map, pipeline_mode=pl.Buffered(1))


def _sigmoid(v):
    return 1.0 / (1.0 + jnp.exp(-v))


def _rms_mod(xf, g, sh, sc):
    ms = jnp.mean(xf * xf, axis=-1, keepdims=True)
    return (xf * lax.rsqrt(ms + EPS) * g) * (1.0 + sc) + sh


def _rope(x, cos, sin, shift):
    lane = lax.broadcasted_iota(jnp.int32, x.shape, 1)
    fwd = pltpu.roll(x, SLAB - shift, axis=1)
    bwd = pltpu.roll(x, shift, axis=1)
    partner = jnp.where((lane & (2 * shift - 1)) < shift, fwd, bwd)
    return x * cos + partner * sin


def _ada_kernel(c_ref, w_ref, b_ref, o_ref):
    cv = c_ref[...]
    s = (cv * _sigmoid(cv)).astype(BF)
    o_ref[...] = jnp.dot(s, w_ref[...].astype(BF), preferred_element_type=F32) + b_ref[...]


def _ada(cvec, w_ada, b_ada):
    depth, d, n = w_ada.shape
    rows = cvec.shape[0]
    bn = 1024
    return pl.pallas_call(
        _ada_kernel,
        grid=(depth, n // bn),
        in_specs=[
            pl.BlockSpec((rows, d), lambda l, j: (0, 0)),
            pl.BlockSpec((None, d, bn), lambda l, j: (l, 0, j)),
            pl.BlockSpec((None, 1, bn), lambda l, j: (l, 0, j)),
        ],
        out_specs=pl.BlockSpec((None, rows, bn), lambda l, j: (l, 0, j)),
        out_shape=jax.ShapeDtypeStruct((depth, rows, n), F32),
        compiler_params=_params(("arbitrary", "arbitrary")),
        name="ada",
    )(cvec, w_ada, b_ada.reshape(depth, 1, n))


def _mod_spec(layer, k, row_fn, nargs):
    if nargs == 1:
        return lambda d: pl.BlockSpec((None, None, None, 1, d), lambda i: (layer, row_fn(i), k, 0, 0))
    return lambda d: pl.BlockSpec((None, None, None, 1, d), lambda i, j: (layer, row_fn(i), k, 0, 0))


def _inproj_kernel(x_ref, sh_ref, sc_ref, g_ref, w_ref, cd_ref, sd_ref, cs_ref, ss_ref, o_ref, h_scr, *, bm):
    g, sh, sc = g_ref[...], sh_ref[...], sc_ref[...]
    group = 4
    rc = min(bm, ROW_CHUNK)
    for lo in range(0, bm, rc):
        for r in range(lo, lo + rc, NORM_CHUNK):
            h_scr[r:r + NORM_CHUNK, :] = _rms_mod(x_ref[r:r + NORM_CHUNK, :], g, sh, sc).astype(BF)
        rows = slice(lo, lo + rc)
        for grp in range(N_SLABS // group):
            acc = jnp.dot(h_scr[rows, :], w_ref[:, grp * group * SLAB:(grp + 1) * group * SLAB],
                          preferred_element_type=F32)
            for t in range(group):
                slab = grp * group + t
                a = acc[:, t * SLAB:(t + 1) * SLAB]
                if slab < DV0:
                    a = _rope(a, cd_ref[rows, :], sd_ref[rows, :], DIFF_QK_DIM // 4)
                    if slab < DK0:
                        a = a * DIFF_QSCALE
                elif SQ0 <= slab < SV0:
                    a = _rope(a, cs_ref[rows, :], ss_ref[rows, :], SWA_HEAD_DIM // 4)
                    if slab < SK0:
                        a = a * SWA_QSCALE
                o_ref[rows, slab * SLAB:(slab + 1) * SLAB] = a.astype(BF)


def _inproj(x, mods, layer, row_fn, g_pre, w, tables, seq, bm):
    rows, d = x.shape
    nt = seq // bm
    mod = lambda k: _mod_spec(layer, k, row_fn, 1)(d)
    tab = pl.BlockSpec((bm, SLAB), lambda i: (i % nt, 0))
    return pl.pallas_call(
        functools.partial(_inproj_kernel, bm=bm),
        grid=(rows // bm,),
        in_specs=[
            pl.BlockSpec((bm, d), lambda i: (i, 0)),
            mod(0), mod(1),
            pl.BlockSpec((1, d), lambda i: (0, 0)),
            _resident((None, d, IN_W), lambda i: (layer, 0, 0)),
            tab, tab, tab, tab,
        ],
        out_specs=pl.BlockSpec((bm, IN_W), lambda i: (i, 0)),
        out_shape=jax.ShapeDtypeStruct((rows, IN_W), BF),
        scratch_shapes=[pltpu.VMEM((bm, d), BF)],
        compiler_params=_params(("arbitrary",)),
        name="inproj",
    )(x, mods, mods, g_pre, w, *tables)


ONES_ROWS = 16
DIFF_SUB_BLOCKS = 2
DIFF_UNROLL = 64
DIFF_HAZARD = 64.0


def _diff_kernel(*refs, n_lat, unroll, n_sub, lagged):
    n_out = 2 if lagged else 1
    ins, outs = refs[:-n_out], refs[-n_out:]
    if n_lat:
        lam_ref, sub_ref, q_ref, kl_ref, vl_ref, kc_ref, vc_ref = ins
    else:
        lam_ref, sub_ref, q_ref, kc_ref, vc_ref = ins
    o_ref = outs[0]
    lp = lam_ref[...]
    lam_init = lp[4:5, 0:1]
    lam = (jnp.exp(jnp.sum(lp[0:1] * lp[1:2], axis=-1, keepdims=True))
           - jnp.exp(jnp.sum(lp[2:3] * lp[3:4], axis=-1, keepdims=True)) + lam_init)

    bq = q_ref.shape[0] // n_sub
    n_chunks = n_lat + 1
    keys = lambda t: kc_ref[...] if isinstance(t, int) and t == n_lat else kl_ref[t]
    values = lambda t: vc_ref[...] if isinstance(t, int) and t == n_lat else vl_ref[t]

    def run_steps(step, carry, first, n_loop):
        trips = n_loop // unroll

        def body(i, carry):
            for u in range(unroll):
                carry = step(first + i * unroll + u, carry)
            return carry

        if trips:
            carry = lax.fori_loop(0, trips, body, carry)
        for t in range(first + trips * unroll, n_chunks):
            carry = step(t, carry)
        return carry

    def one_block(sb):
        rows = slice(sb * bq, (sb + 1) * bq)
        q = q_ref[rows, :].astype(F32)
        lane = lax.broadcasted_iota(jnp.int32, q.shape, 1)
        q_both = jnp.concatenate([jnp.where(lane < DIFF_QK_DIM, q, 0.0), jnp.where(lane >= DIFF_QK_DIM, q, 0.0)],
                                 axis=0)
        q_t = jnp.transpose(q_both).astype(BF)

        def qk(t):
            return jnp.dot(keys(t), q_t, preferred_element_type=F32)

        def pv(t, p):
            v_t = values(t)
            v_ext = jnp.concatenate([v_t, jnp.ones((ONES_ROWS, v_t.shape[1]), BF)], axis=0)
            return jnp.dot(v_ext, p, preferred_element_type=F32)

        colmax = lambda s: jnp.max(s, axis=0, keepdims=True)

        if lagged:
            s = qk(0)
            r = colmax(s)
            acc = jnp.zeros((DIFF_V_DIM + ONES_ROWS, 2 * bq), F32)

            def absorb(t, s, r_prev, r, acc, excess):
                p = jnp.exp2(s - r).astype(BF)
                cm = colmax(s)
                acc = jnp.exp2(r_prev - r) * acc + pv(t, p)
                return r, jnp.maximum(r, cm), acc, jnp.maximum(excess, cm - r)

            def step(t, carry):
                s, state = carry
                s_new = qk(t)
                return s_new, absorb(t - 1, s, *state)

            carry = (s, (r, r, acc, jnp.zeros_like(r)))
            s, state = run_steps(step, carry, 1, max(n_lat - 1, 0))
            _, _, acc, excess = absorb(n_chunks - 1, s, *state)
            outs[1][sb:sb + 1, :] = excess
        else:
            def scores(t):
                s = qk(t)
                return s, colmax(s)

            def softmax(sc, m):
                s, smax = sc
                mn = jnp.maximum(m, smax)
                return mn, jnp.exp2(m - mn), jnp.exp2(s - mn).astype(BF)

            m = jnp.full((1, 2 * bq), NEG_INF, F32)
            acc = jnp.zeros((DIFF_V_DIM + ONES_ROWS, 2 * bq), F32)
            sc = scores(0)
            if n_chunks > 1:
                sc_next = scores(1)
                m, alpha, p = softmax(sc, m)

                def step(t, carry):
                    sc, p, alpha, m, acc = carry
                    sc_new = scores(t)
                    m_new, alpha_new, p_new = softmax(sc, m)
                    acc = alpha * acc + pv(t - 2, p)
                    return sc_new, p_new, alpha_new, m_new, acc

                sc, p, alpha, m, acc = run_steps(step, (sc_next, p, alpha, m, acc), 2, max(n_lat - 2, 0))
                acc = alpha * acc + pv(n_chunks - 2, p)
            m, alpha, p = softmax(sc, m)
            acc = alpha * acc + pv(n_chunks - 1, p)

        o_t = acc[:DIFF_V_DIM] / acc[DIFF_V_DIM:DIFF_V_DIM + 1]
        o = jnp.transpose(o_t[:, :bq]) - lam * jnp.transpose(o_t[:, bq:])
        ms = jnp.mean(o * o, axis=-1, keepdims=True)
        o_ref[rows, :] = (o * lax.rsqrt(ms + EPS) * sub_ref[...] * (1.0 - lam_init)).astype(BF)

    for sb in range(n_sub):
        one_block(sb)


def _diff_attention(p_q, p_lat, p_ctx, lam_par, subln, bq, bk):
    b, sq, _ = p_q.shape
    c = p_ctx.shape[1]
    vw = DIFF_HEADS * DIFF_V_DIM
    n_lat = 0 if p_lat is None else p_lat.shape[1] // bk
    n_sub = DIFF_SUB_BLOCKS if sq % (DIFF_SUB_BLOCKS * bq) == 0 else 1
    qb = n_sub * bq
    in_specs = [
        pl.BlockSpec(lam_par.shape, lambda bi, h, qi: (0, 0)),
        pl.BlockSpec((1, DIFF_V_DIM), lambda bi, h, qi: (0, 0)),
        pl.BlockSpec((None, qb, SLAB), lambda bi, h, qi: (bi, qi, DQ0 + h)),
    ]
    args = [lam_par, subln, p_q]
    values_of = lambda p: p[:, :, DV0 * SLAB:DV0 * SLAB + vw]
    if n_lat:
        k_lat = p_lat.reshape(b, n_lat, bk, IN_W)
        v_lat = jnp.transpose(values_of(p_lat).reshape(b, n_lat, bk, DIFF_HEADS, DIFF_V_DIM), (0, 3, 1, 4, 2))
        in_specs += [pl.BlockSpec((None, n_lat, bk, SLAB), lambda bi, h, qi: (bi, 0, 0, DK0 + h)),
                     pl.BlockSpec((None, None, n_lat, DIFF_V_DIM, bk), lambda bi, h, qi: (bi, h, 0, 0, 0))]
        args += [k_lat, v_lat]
    v_ctx = jnp.transpose(values_of(p_ctx).reshape(b, c, DIFF_HEADS, DIFF_V_DIM), (0, 2, 3, 1))
    in_specs += [pl.BlockSpec((None, c, SLAB), lambda bi, h, qi: (bi, 0, DK0 + h)),
                 pl.BlockSpec((None, None, DIFF_V_DIM, c), lambda bi, h, qi: (bi, h, 0, 0))]
    args += [p_ctx, v_ctx]
    nq = sq // qb

    def call(lagged):
        out_specs = [pl.BlockSpec((None, qb, SLAB), lambda bi, h, qi: (bi, qi, h))]
        out_shape = [jax.ShapeDtypeStruct((b, sq, vw), BF)]
        if lagged:
            out_specs.append(pl.BlockSpec((None, None, None, n_sub, 2 * bq), lambda bi, h, qi: (bi, h, qi, 0, 0)))
            out_shape.append(jax.ShapeDtypeStruct((b, DIFF_HEADS, nq, n_sub, 2 * bq), F32))
        return pl.pallas_call(
            functools.partial(_diff_kernel, n_lat=n_lat, unroll=DIFF_UNROLL, n_sub=n_sub, lagged=lagged),
            grid=(b, DIFF_HEADS, nq),
            in_specs=in_specs,
            out_specs=out_specs,
            out_shape=out_shape,
            compiler_params=_params(("arbitrary", "arbitrary", "arbitrary")),
            name=("diff_attn_lagged" if lagged else "diff_attn") if n_lat else "diff_attn_ctx",
        )(*args)

    if not n_lat:
        return call(False)[0]
    od, excess = call(True)
    safe = jnp.max(excess) <= DIFF_HAZARD
    return lax.cond(safe, lambda: od, lambda: call(False)[0])


def _swa_kernel(*refs, band):
    if band:
        sink_ref, mask_ref, q_ref, kp_ref, kz_ref, kn_ref, vp_ref, vz_ref, vn_ref, kc_ref, vc_ref, o_ref = refs
    else:
        sink_ref, q_ref, kc_ref, vc_ref, o_ref = refs
    blk = q_ref.shape[0]
    rows = SWA_GROUP * blk
    rowc = lax.broadcasted_iota(jnp.int32, (rows, 1), 0)
    for g in range(SWA_KV_HEADS):
        gs = slice(g * SLAB, (g + 1) * SLAB)
        q4 = jnp.concatenate([q_ref[:, (g * SWA_GROUP + h) * SLAB:(g * SWA_GROUP + h + 1) * SLAB]
                              for h in range(SWA_GROUP)], axis=0)
        if band:
            kcat = jnp.concatenate([kp_ref[:, gs], kz_ref[:, gs], kn_ref[:, gs], kc_ref[:, gs]], axis=0)
            vcat = jnp.concatenate([vp_ref[:, gs], vz_ref[:, gs], vn_ref[:, gs], vc_ref[:, gs]], axis=0)
        else:
            kcat, vcat = kc_ref[:, gs], vc_ref[:, gs]
        s = lax.dot_general(q4, kcat, (((1,), (1,)), ((), ())), preferred_element_type=F32)
        if band:
            s = s + mask_ref[...]
        sk = jnp.full((rows, 1), sink_ref[g * SWA_GROUP + SWA_GROUP - 1], F32)
        for h in range(SWA_GROUP - 2, -1, -1):
            sk = jnp.where(rowc < (h + 1) * blk, sink_ref[g * SWA_GROUP + h], sk)
        sk = sk * LOG2E
        m = jnp.maximum(jnp.max(s, axis=-1, keepdims=True), sk)
        p = jnp.exp2(s - m)
        l = jnp.sum(p, axis=-1, keepdims=True) + jnp.exp2(sk - m)
        o = jnp.dot(p.astype(BF), vcat, preferred_element_type=F32) / l
        for h in range(SWA_GROUP):
            hs = (g * SWA_GROUP + h) * SLAB
            o_ref[:, hs:hs + SLAB] = o[h * blk:(h + 1) * blk, :].astype(BF)


def _swa_attention(p_q, p_ctx, sink, band):
    b, sq, _ = p_q.shape
    c = p_ctx.shape[1]
    blk = WINDOW
    nb = sq // blk
    qw = SWA_HEADS * SLAB
    kvw = SWA_KV_HEADS * SLAB
    in_specs = [
        pl.BlockSpec(memory_space=pltpu.SMEM),
        pl.BlockSpec((None, blk, qw), lambda bi, n: (bi, n, SQ0 * SLAB // qw)),
    ]
    args = [sink, p_q]
    if band:
        r = np.arange(SWA_GROUP * blk)[:, None] % blk
        col = np.arange(3 * blk + c)[None, :]
        in_band = (np.abs(r - (col - blk)) <= WINDOW) | (col >= 3 * blk)
        variants = [in_band & ~((col < blk) & bool(v & 1)) & ~((col >= 2 * blk) & (col < 3 * blk) & bool(v & 2))
                    for v in range(4)]
        mask = jnp.asarray(np.where(np.stack(variants), 0.0, NEG_INF).astype(np.float32))
        in_specs.insert(1, pl.BlockSpec(
            (None,) + mask.shape[1:],
            lambda bi, n: (jnp.where(n == 0, 1, 0) + jnp.where(n == nb - 1, 2, 0), 0, 0)))
        args.insert(1, mask)
        for base in (SK0 * SLAB // kvw, SV0 * SLAB // kvw):
            in_specs += [
                pl.BlockSpec((None, blk, kvw), lambda bi, n, base=base: (bi, jnp.maximum(n - 1, 0), base)),
                pl.BlockSpec((None, blk, kvw), lambda bi, n, base=base: (bi, n, base)),
                pl.BlockSpec((None, blk, kvw), lambda bi, n, base=base: (bi, jnp.minimum(n + 1, nb - 1), base)),
            ]
            args += [p_q, p_q, p_q]
    in_specs += [pl.BlockSpec((None, c, kvw), lambda bi, n: (bi, 0, SK0 * SLAB // kvw)),
                 pl.BlockSpec((None, c, kvw), lambda bi, n: (bi, 0, SV0 * SLAB // kvw))]
    args += [p_ctx, p_ctx]
    return pl.pallas_call(
        functools.partial(_swa_kernel, band=band),
        grid=(b, nb),
        in_specs=in_specs,
        out_specs=pl.BlockSpec((None, blk, qw), lambda bi, n: (bi, n, 0)),
        out_shape=jax.ShapeDtypeStruct((b, sq, qw), BF),
        compiler_params=_params(("arbitrary", "arbitrary")),
        name="swa_attn" if band else "swa_attn_ctx",
    )(*args)


def _matmul_post_norm(operands, x_ref, gt_ref, g_ref, o_ref, f_scr, bm):
    gt, g = gt_ref[...], g_ref[...]
    rc = min(bm, ROW_CHUNK)
    for lo in range(0, bm, rc):
        rows = slice(lo, lo + rc)
        f_scr[rows, :] = sum(jnp.dot(lhs[rows, :], w[...], preferred_element_type=F32) for lhs, w in operands)
        for r in range(lo, lo + rc, NORM_CHUNK):
            sub = slice(r, r + NORM_CHUNK)
            f = f_scr[sub, :]
            ms = jnp.mean(f * f, axis=-1, keepdims=True)
            o_ref[sub, :] = x_ref[sub, :] + gt * (f * lax.rsqrt(ms + EPS) * g)


def _outproj_kernel(od_ref, os_ref, wt_ref, wb_ref, x_ref, gt_ref, g_ref, o_ref, mix_scr, *, bm):
    _matmul_post_norm(((od_ref, wt_ref), (os_ref, wb_ref)), x_ref, gt_ref, g_ref, o_ref, mix_scr, bm)


def _outproj(od, osw, w_out, x, mods, layer, row_fn, g_post, bm):
    rows, d = x.shape
    half = od.shape[1]
    return pl.pallas_call(
        functools.partial(_outproj_kernel, bm=bm),
        grid=(rows // bm,),
        in_specs=[
            pl.BlockSpec((bm, half), lambda i: (i, 0)),
            pl.BlockSpec((bm, half), lambda i: (i, 0)),
            _resident((None, half, d), lambda i: (layer, 0, 0)),
            _resident((None, half, d), lambda i: (layer, 1, 0)),
            pl.BlockSpec((bm, d), lambda i: (i, 0)),
            _mod_spec(layer, 2, row_fn, 1)(d),
            pl.BlockSpec((1, d), lambda i: (0, 0)),
        ],
        out_specs=pl.BlockSpec((bm, d), lambda i: (i, 0)),
        out_shape=jax.ShapeDtypeStruct((rows, d), F32),
        scratch_shapes=[pltpu.VMEM((bm, d), F32)],
        compiler_params=_params(("arbitrary",)),
        name="outproj",
    )(od, osw, w_out, w_out, x, mods, g_post)


HALO = 16
CONV_CHUNK = 512


def _ffn_up_kernel(x_ref, xp_ref, xn_ref, sh_ref, sc_ref, g_ref, wg_ref, wu_ref, cw_ref, cb_ref, o_ref,
                   h_scr, *, bm, seq):
    i = pl.program_id(0)
    j = pl.program_id(1)

    @pl.when(j == 0)
    def _():
        g, sh, sc = g_ref[...], sh_ref[...], sc_ref[...]

        def norm_rows(r, carry):
            rows = pl.multiple_of(r * NORM_CHUNK, NORM_CHUNK)
            h_scr[pl.ds(HALO + rows, NORM_CHUNK), :] = _rms_mod(x_ref[pl.ds(rows, NORM_CHUNK), :], g, sh, sc).astype(BF)
            return carry

        lax.fori_loop(0, bm // NORM_CHUNK, norm_rows, 0)
        prev_ok = ((i * bm) % seq != 0).astype(F32)
        next_ok = (((i + 1) * bm) % seq != 0).astype(F32)
        h_scr[0:HALO, :] = (_rms_mod(xp_ref[...], g, sh, sc) * prev_ok).astype(BF)
        h_scr[HALO + bm:2 * HALO + bm, :] = (_rms_mod(xn_ref[...], g, sh, sc) * next_ok).astype(BF)

    cw0, cw1, cw2, cb = cw_ref[0:1, :], cw_ref[1:2, :], cw_ref[2:3, :], cb_ref[...]
    rc = min(bm, CONV_CHUNK)
    pad = 8
    for r in range(bm // rc):
        lo = r * rc
        g = jnp.dot(h_scr[lo:lo + rc + 2 * HALO, :], wg_ref[...], preferred_element_type=F32)
        u = jnp.dot(h_scr[lo + HALO:lo + HALO + rc, :], wu_ref[...], preferred_element_type=F32)
        win = g[HALO - pad:HALO + rc + pad]
        below = pltpu.roll(win, 1, axis=0)[pad:pad + rc]
        above = pltpu.roll(win, rc + 2 * pad - 1, axis=0)[pad:pad + rc]
        gc = below * cw0 + g[HALO:HALO + rc] * cw1 + above * cw2 + cb
        a = gc * _sigmoid(gc) * u
        o_ref[lo:lo + rc, :] = a.astype(BF)


def _ffn_up(x, mods, layer, row_fn, g_pre, wg, wu, cw, cb, seq, bm, bn):
    rows, d = x.shape
    dff = wg.shape[2]
    hb = bm // HALO
    nh = rows // HALO
    mod = lambda k: _mod_spec(layer, k, row_fn, 2)(d)
    return pl.pallas_call(
        functools.partial(_ffn_up_kernel, bm=bm, seq=seq),
        grid=(rows // bm, dff // bn),
        in_specs=[
            pl.BlockSpec((bm, d), lambda i, j: (i, 0)),
            pl.BlockSpec((HALO, d), lambda i, j: (jnp.maximum(i * hb - 1, 0), 0)),
            pl.BlockSpec((HALO, d), lambda i, j: (jnp.minimum((i + 1) * hb, nh - 1), 0)),
            mod(3), mod(4),
            pl.BlockSpec((1, d), lambda i, j: (0, 0)),
            pl.BlockSpec((None, d, bn), lambda i, j: (layer, 0, j)),
            pl.BlockSpec((None, d, bn), lambda i, j: (layer, 0, j)),
            pl.BlockSpec((3, bn), lambda i, j: (0, j)),
            pl.BlockSpec((1, bn), lambda i, j: (0, j)),
        ],
        out_specs=pl.BlockSpec((bm, bn), lambda i, j: (i, j)),
        out_shape=jax.ShapeDtypeStruct((rows, dff), BF),
        scratch_shapes=[pltpu.VMEM((bm + 2 * HALO, d), BF)],
        compiler_params=_params(("arbitrary", "arbitrary")),
        name="ffn_up",
    )(x, x, x, mods, mods, g_pre, wg, wu, cw, cb)


def _ffn_down_kernel(a_ref, w_ref, x_ref, gt_ref, g_ref, o_ref, f_scr, *, bm):
    _matmul_post_norm(((a_ref, w_ref),), x_ref, gt_ref, g_ref, o_ref, f_scr, bm)


def _ffn_down(a, wd, x, mods, layer, row_fn, g_post, bm):
    rows, d = x.shape
    dff = a.shape[1]
    return pl.pallas_call(
        functools.partial(_ffn_down_kernel, bm=bm),
        grid=(rows // bm,),
        in_specs=[
            pl.BlockSpec((bm, dff), lambda i: (i, 0)),
            _resident((None, dff, d), lambda i: (layer, 0, 0)),
            pl.BlockSpec((bm, d), lambda i: (i, 0)),
            _mod_spec(layer, 5, row_fn, 1)(d),
            pl.BlockSpec((1, d), lambda i: (0, 0)),
        ],
        out_specs=pl.BlockSpec((bm, d), lambda i: (i, 0)),
        out_shape=jax.ShapeDtypeStruct((rows, d), F32),
        scratch_shapes=[pltpu.VMEM((bm, d), F32)],
        compiler_params=_params(("arbitrary",)),
        name="ffn_down",
    )(a, wd, x, mods, g_post)


def _rope_tables(seq, ctx_len):
    t = np.arange(seq)
    row, col = t // GRID_W, t % GRID_W

    def table(half):
        inv = ROPE_THETA ** (-np.arange(half, dtype=np.float64) / half)
        ar, ac = row[:, None] * inv[None, :], col[:, None] * inv[None, :]
        cos = np.concatenate([np.cos(ar), np.cos(ar), np.cos(ac), np.cos(ac)], axis=1)
        sin = np.concatenate([-np.sin(ar), np.sin(ar), -np.sin(ac), np.sin(ac)], axis=1)
        reps = SLAB // cos.shape[1]
        return np.tile(cos, (1, reps)).astype(np.float32), np.tile(sin, (1, reps)).astype(np.float32)

    cd, sd = table(DIFF_QK_DIM // 4)
    cs, ss = table(SWA_HEAD_DIM // 4)
    lat = tuple(jnp.asarray(a) for a in (cd, sd, cs, ss))
    one, zero = jnp.ones((ctx_len, SLAB), F32), jnp.zeros((ctx_len, SLAB), F32)
    return lat, (one, zero, one, zero)


def _pick(n, prefs):
    for p in prefs:
        if n % p == 0:
            return p
    raise ValueError(f"no block size in {prefs} divides {n}")


def kernel(x, c, ctx, c_ctx, w_ada, b_ada, g_pre_mix, g_post_mix, w_in, diff_lambda, diff_subln, swa_sink,
           w_out, g_pre_ffn, g_post_ffn, w_ffn_gate, w_ffn_up, ffn_conv_w, ffn_conv_b, w_ffn_down):
    b, s, d = x.shape
    cl = ctx.shape[1]
    depth = w_ada.shape[0]
    dff = w_ffn_gate.shape[2]
    assert w_in.shape[2] == IN_W and s % WINDOW == 0 and cl % WINDOW == 0

    n_rows = -(-(b + 1) // 8) * 8
    cvec = jnp.concatenate([c, c_ctx[None, :], jnp.zeros((n_rows - b - 1, d), F32)], axis=0)
    mods = _ada(cvec, w_ada, b_ada).reshape(depth, n_rows, 6, 1, d)

    lat_tabs, ctx_tabs = _rope_tables(s, cl)
    bm_in = _pick(s, (512, 256, 128))
    bm_out = _pick(s, (512, 256, 128))
    bm_up = _pick(s, (1024, 512, 256, 128))
    bm_dn = _pick(s, (256, 128))
    bm_c = _pick(cl, (256, 128))
    bn_ff = _pick(dff, (512, 256, 128))
    bq = _pick(s, (256, 128))
    bk = _pick(s, (512, 256, 128))
    bq_c = _pick(cl, (256, 128))

    xl = x.reshape(b * s, d)
    xc = ctx.reshape(b * cl, d)
    ctx_row = lambda i: b

    w_in_l, w_out_l = w_in.astype(BF), w_out.astype(BF)
    wg_l, wu_l, wd_l = w_ffn_gate.astype(BF), w_ffn_up.astype(BF), w_ffn_down.astype(BF)

    for l in range(depth):
        last = l == depth - 1
        lam_init = 0.8 - 0.6 * math.exp(-0.3 * l)
        lam_par = jnp.concatenate([diff_lambda[l], jnp.full_like(diff_lambda[l], lam_init)], axis=0)
        gpm, gqm = g_pre_mix[l][None, :], g_post_mix[l][None, :]
        gpf, gqf = g_pre_ffn[l][None, :], g_post_ffn[l][None, :]
        sub = diff_subln[l][None, :]
        cb = ffn_conv_b[l][None, :]

        p_lat = _inproj(xl, mods, l, lambda i, n=s // bm_in: i // n, gpm, w_in_l, lat_tabs, s, bm_in)
        p_ctx = _inproj(xc, mods, l, ctx_row, gpm, w_in_l, ctx_tabs, cl, bm_c)
        p_lat3, p_ctx3 = p_lat.reshape(b, s, IN_W), p_ctx.reshape(b, cl, IN_W)
        od = _diff_attention(p_lat3, p_lat3, p_ctx3, lam_par, sub, bq, bk)
        osw = _swa_attention(p_lat3, p_ctx3, swa_sink[l], band=True)
        xl = _outproj(od.reshape(b * s, -1), osw.reshape(b * s, -1), w_out_l, xl, mods, l,
                      lambda i, n=s // bm_out: i // n, gqm, bm_out)
        if not last:
            odc = _diff_attention(p_ctx3, None, p_ctx3, lam_par, sub, bq_c, bk)
            oswc = _swa_attention(p_ctx3, p_ctx3, swa_sink[l], band=False)
            xc = _outproj(odc.reshape(b * cl, -1), oswc.reshape(b * cl, -1), w_out_l, xc, mods, l, ctx_row, gqm, bm_c)

        a = _ffn_up(xl, mods, l, lambda i, n=s // bm_up: i // n, gpf, wg_l, wu_l, ffn_conv_w[l], cb, s, bm_up, bn_ff)
        xl = _ffn_down(a, wd_l, xl, mods, l, lambda i, n=s // bm_dn: i // n, gqf, bm_dn)
        if not last:
            ac = _ffn_up(xc, mods, l, ctx_row, gpf, wg_l, wu_l, ffn_conv_w[l], cb, cl, bm_c, bn_ff)
            xc = _ffn_down(ac, wd_l, xc, mods, l, ctx_row, gqf, bm_c)
    return xl.reshape(b, s, d)
```

```python
import functools
import math

import jax
import jax.numpy as jnp
import numpy as np
from jax import lax
from jax.experimental import pallas as pl
from jax.experimental.pallas import tpu as pltpu

BF = jnp.bfloat16
F32 = jnp.float32

GRID_W = 64
DIFF_HEADS = 8
DIFF_QK_DIM = 64
DIFF_V_DIM = 128
SWA_HEADS = 8
SWA_KV_HEADS = 2
SWA_GROUP = SWA_HEADS // SWA_KV_HEADS
SWA_HEAD_DIM = 128
WINDOW = 128
ROPE_THETA = 10000.0
EPS = 1e-6
NEG_INF = -1e30
LOG2E = 1.4426950408889634
DIFF_QSCALE = DIFF_QK_DIM ** -0.5 * LOG2E
SWA_QSCALE = SWA_HEAD_DIM ** -0.5 * LOG2E

SLAB = 128
DQ0, DK0, DV0, SQ0, SK0, SV0 = 0, 8, 16, 24, 32, 34
N_SLABS = 36
IN_W = N_SLABS * SLAB

VMEM_LIMIT = 56 * 1024 * 1024
NORM_CHUNK = 64
ROW_CHUNK = 256


def _params(sem, vmem=VMEM_LIMIT, flags=None):
    return pltpu.CompilerParams(dimension_semantics=sem, vmem_limit_bytes=vmem, flags=flags)


def _resident(shape, index_map):
    return pl.BlockSpec(shape, index_map, pipeline_mode=pl.Buffered(1))


def _sigmoid(v):
    return 1.0 / (1.0 + jnp.exp(-v))


def _rms_mod(xf, g, sh, sc):
    ms = jnp.mean(xf * xf, axis=-1, keepdims=True)
    return (xf * lax.rsqrt(ms + EPS) * g) * (1.0 + sc) + sh


def _rope(x, cos, sin, shift):
    lane = lax.broadcasted_iota(jnp.int32, x.shape, 1)
    fwd = pltpu.roll(x, SLAB - shift, axis=1)
    bwd = pltpu.roll(x, shift, axis=1)
    partner = jnp.where((lane & (2 * shift - 1)) < shift, fwd, bwd)
    return x * cos + partner * sin


def _ada_kernel(c_ref, w_ref, b_ref, o_ref):
    cv = c_ref[...]
    s = (cv * _sigmoid(cv)).astype(BF)
    o_ref[...] = jnp.dot(s, w_ref[...].astype(BF), preferred_element_type=F32) + b_ref[...]


def _ada(cvec, w_ada, b_ada):
    depth, d, n = w_ada.shape
    rows = cvec.shape[0]
    bn = 1024
    return pl.pallas_call(
        _ada_kernel,
        grid=(depth, n // bn),
        in_specs=[
            pl.BlockSpec((rows, d), lambda l, j: (0, 0)),
            pl.BlockSpec((None, d, bn), lambda l, j: (l, 0, j)),
            pl.BlockSpec((None, 1, bn), lambda l, j: (l, 0, j)),
        ],
        out_specs=pl.BlockSpec((None, rows, bn), lambda l, j: (l, 0, j)),
        out_shape=jax.ShapeDtypeStruct((depth, rows, n), F32),
        compiler_params=_params(("arbitrary", "arbitrary")),
        name="ada",
    )(cvec, w_ada, b_ada.reshape(depth, 1, n))


def _mod_spec(layer, k, row_fn, nargs):
    if nargs == 1:
        return lambda d: pl.BlockSpec((None, None, None, 1, d), lambda i: (layer, row_fn(i), k, 0, 0))
    return lambda d: pl.BlockSpec((None, None, None, 1, d), lambda i, j: (layer, row_fn(i), k, 0, 0))


def _inproj_kernel(x_ref, sh_ref, sc_ref, g_ref, w_ref, cd_ref, sd_ref, cs_ref, ss_ref, o_ref, h_scr, *, bm):
    g, sh, sc = g_ref[...], sh_ref[...], sc_ref[...]
    group = 4
    rc = min(bm, ROW_CHUNK)
    for lo in range(0, bm, rc):
        for r in range(lo, lo + rc, NORM_CHUNK):
            h_scr[r:r + NORM_CHUNK, :] = _rms_mod(x_ref[r:r + NORM_CHUNK, :], g, sh, sc).astype(BF)
        rows = slice(lo, lo + rc)
        for grp in range(N_SLABS // group):
            acc = jnp.dot(h_scr[rows, :], w_ref[:, grp * group * SLAB:(grp + 1) * group * SLAB],
                          preferred_element_type=F32)
            for t in range(group):
                slab = grp * group + t
                a = acc[:, t * SLAB:(t + 1) * SLAB]
                if slab < DV0:
                    a = _rope(a, cd_ref[rows, :], sd_ref[rows, :], DIFF_QK_DIM // 4)
                    if slab < DK0:
                        a = a * DIFF_QSCALE
                elif SQ0 <= slab < SV0:
                    a = _rope(a, cs_ref[rows, :], ss_ref[rows, :], SWA_HEAD_DIM // 4)
                    if slab < SK0:
                        a = a * SWA_QSCALE
                o_ref[rows, slab * SLAB:(slab + 1) * SLAB] = a.astype(BF)


def _inproj(x, mods, layer, row_fn, g_pre, w, tables, seq, bm):
    rows, d = x.shape
    nt = seq // bm
    mod = lambda k: _mod_spec(layer, k, row_fn, 1)(d)
    tab = pl.BlockSpec((bm, SLAB), lambda i: (i % nt, 0))
    return pl.pallas_call(
        functools.partial(_inproj_kernel, bm=bm),
        grid=(rows // bm,),
        in_specs=[
            pl.BlockSpec((bm, d), lambda i: (i, 0)),
            mod(0), mod(1),
            pl.BlockSpec((1, d), lambda i: (0, 0)),
            _resident((None, d, IN_W), lambda i: (layer, 0, 0)),
            tab, tab, tab, tab,
        ],
        out_specs=pl.BlockSpec((bm, IN_W), lambda i: (i, 0)),
        out_shape=jax.ShapeDtypeStruct((rows, IN_W), BF),
        scratch_shapes=[pltpu.VMEM((bm, d), BF)],
        compiler_params=_params(("arbitrary",)),
        name="inproj",
    )(x, mods, mods, g_pre, w, *tables)


ONES_ROWS = 16
DIFF_SUB_BLOCKS = 2
DIFF_UNROLL = 64
DIFF_HAZARD = 64.0


def _diff_kernel(*refs, n_lat, unroll, n_sub, lagged):
    n_out = 2 if lagged else 1
    ins, outs = refs[:-n_out], refs[-n_out:]
    if n_lat:
        lam_ref, sub_ref, q_ref, kl_ref, vl_ref, kc_ref, vc_ref = ins
    else:
        lam_ref, sub_ref, q_ref, kc_ref, vc_ref = ins
    o_ref = outs[0]
    lp = lam_ref[...]
    lam_init = lp[4:5, 0:1]
    lam = (jnp.exp(jnp.sum(lp[0:1] * lp[1:2], axis=-1, keepdims=True))
           - jnp.exp(jnp.sum(lp[2:3] * lp[3:4], axis=-1, keepdims=True)) + lam_init)

    bq = q_ref.shape[0] // n_sub
    n_chunks = n_lat + 1
    keys = lambda t: kc_ref[...] if isinstance(t, int) and t == n_lat else kl_ref[t]
    values = lambda t: vc_ref[...] if isinstance(t, int) and t == n_lat else vl_ref[t]

    def run_steps(step, carry, first, n_loop):
        trips = n_loop // unroll

        def body(i, carry):
            for u in range(unroll):
                carry = step(first + i * unroll + u, carry)
            return carry

        if trips:
            carry = lax.fori_loop(0, trips, body, carry)
        for t in range(first + trips * unroll, n_chunks):
            carry = step(t, carry)
        return carry

    def one_block(sb):
        rows = slice(sb * bq, (sb + 1) * bq)
        q = q_ref[rows, :].astype(F32)
        lane = lax.broadcasted_iota(jnp.int32, q.shape, 1)
        q_both = jnp.concatenate([jnp.where(lane < DIFF_QK_DIM, q, 0.0), jnp.where(lane >= DIFF_QK_DIM, q, 0.0)],
                                 axis=0)
        q_t = jnp.transpose(q_both).astype(BF)

        def qk(t):
            return jnp.dot(keys(t), q_t, preferred_element_type=F32)

        def pv(t, p):
            v_t = values(t)
            v_ext = jnp.concatenate([v_t, jnp.ones((ONES_ROWS, v_t.shape[1]), BF)], axis=0)
            return jnp.dot(v_ext, p, preferred_element_type=F32)

        colmax = lambda s: jnp.max(s, axis=0, keepdims=True)

        if lagged:
            s = qk(0)
            r = colmax(s)
            acc = jnp.zeros((DIFF_V_DIM + ONES_ROWS, 2 * bq), F32)

            def absorb(t, s, r_prev, r, acc, excess):
                p = jnp.exp2(s - r).astype(BF)
                cm = colmax(s)
                acc = jnp.exp2(r_prev - r) * acc + pv(t, p)
                return r, jnp.maximum(r, cm), acc, jnp.maximum(excess, cm - r)

            def step(t, carry):
                s, state = carry
                s_new = qk(t)
                return s_new, absorb(t - 1, s, *state)

            carry = (s, (r, r, acc, jnp.zeros_like(r)))
            s, state = run_steps(step, carry, 1, max(n_lat - 1, 0))
            _, _, acc, excess = absorb(n_chunks - 1, s, *state)
            outs[1][sb:sb + 1, :] = excess
        else:
            def scores(t):
                s = qk(t)
                return s, colmax(s)

            def softmax(sc, m):
                s, smax = sc
                mn = jnp.maximum(m, smax)
                return mn, jnp.exp2(m - mn), jnp.exp2(s - mn).astype(BF)

            m = jnp.full((1, 2 * bq), NEG_INF, F32)
            acc = jnp.zeros((DIFF_V_DIM + ONES_ROWS, 2 * bq), F32)
            sc = scores(0)
            if n_chunks > 1:
                sc_next = scores(1)
                m, alpha, p = softmax(sc, m)

                def step(t, carry):
                    sc, p, alpha, m, acc = carry
                    sc_new = scores(t)
                    m_new, alpha_new, p_new = softmax(sc, m)
                    acc = alpha * acc + pv(t - 2, p)
                    return sc_new, p_new, alpha_new, m_new, acc

                sc, p, alpha, m, acc = run_steps(step, (sc_next, p, alpha, m, acc), 2, max(n_lat - 2, 0))
                acc = alpha * acc + pv(n_chunks - 2, p)
            m, alpha, p = softmax(sc, m)
            acc = alpha * acc + pv(n_chunks - 1, p)

        o_t = acc[:DIFF_V_DIM] / acc[DIFF_V_DIM:DIFF_V_DIM + 1]
        o = jnp.transpose(o_t[:, :bq]) - lam * jnp.transpose(o_t[:, bq:])
        ms = jnp.mean(o * o, axis=-1, keepdims=True)
        o_ref[rows, :] = (o * lax.rsqrt(ms + EPS) * sub_ref[...] * (1.0 - lam_init)).astype(BF)

    for sb in range(n_sub):
        one_block(sb)


def _diff_attention(p_q, p_lat, p_ctx, lam_par, subln, bq, bk):
    b, sq, _ = p_q.shape
    c = p_ctx.shape[1]
    vw = DIFF_HEADS * DIFF_V_DIM
    n_lat = 0 if p_lat is None else p_lat.shape[1] // bk
    n_sub = DIFF_SUB_BLOCKS if sq % (DIFF_SUB_BLOCKS * bq) == 0 else 1
    qb = n_sub * bq
    in_specs = [
        pl.BlockSpec(lam_par.shape, lambda bi, h, qi: (0, 0)),
        pl.BlockSpec((1, DIFF_V_DIM), lambda bi, h, qi: (0, 0)),
        pl.BlockSpec((None, qb, SLAB), lambda bi, h, qi: (bi, qi, DQ0 + h)),
    ]
    args = [lam_par, subln, p_q]
    values_of = lambda p: p[:, :, DV0 * SLAB:DV0 * SLAB + vw]
    if n_lat:
        k_lat = p_lat.reshape(b, n_lat, bk, IN_W)
        v_lat = jnp.transpose(values_of(p_lat).reshape(b, n_lat, bk, DIFF_HEADS, DIFF_V_DIM), (0, 3, 1, 4, 2))
        in_specs += [pl.BlockSpec((None, n_lat, bk, SLAB), lambda bi, h, qi: (bi, 0, 0, DK0 + h)),
                     pl.BlockSpec((None, None, n_lat, DIFF_V_DIM, bk), lambda bi, h, qi: (bi, h, 0, 0, 0))]
        args += [k_lat, v_lat]
    v_ctx = jnp.transpose(values_of(p_ctx).reshape(b, c, DIFF_HEADS, DIFF_V_DIM), (0, 2, 3, 1))
    in_specs += [pl.BlockSpec((None, c, SLAB), lambda bi, h, qi: (bi, 0, DK0 + h)),
                 pl.BlockSpec((None, None, DIFF_V_DIM, c), lambda bi, h, qi: (bi, h, 0, 0))]
    args += [p_ctx, v_ctx]
    nq = sq // qb

    def call(lagged):
        out_specs = [pl.BlockSpec((None, qb, SLAB), lambda bi, h, qi: (bi, qi, h))]
        out_shape = [jax.ShapeDtypeStruct((b, sq, vw), BF)]
        if lagged:
            out_specs.append(pl.BlockSpec((None, None, None, n_sub, 2 * bq), lambda bi, h, qi: (bi, h, qi, 0, 0)))
            out_shape.append(jax.ShapeDtypeStruct((b, DIFF_HEADS, nq, n_sub, 2 * bq), F32))
        return pl.pallas_call(
            functools.partial(_diff_kernel, n_lat=n_lat, unroll=DIFF_UNROLL, n_sub=n_sub, lagged=lagged),
            grid=(b, DIFF_HEADS, nq),
            in_specs=in_specs,
            out_specs=out_specs,
            out_shape=out_shape,
            compiler_params=_params(("arbitrary", "arbitrary", "arbitrary")),
            name=("diff_attn_lagged" if lagged else "diff_attn") if n_lat else "diff_attn_ctx",
        )(*args)

    if not n_lat:
        return call(False)[0]
    od, excess = call(True)
    safe = jnp.max(excess) <= DIFF_HAZARD
    return lax.cond(safe, lambda: od, lambda: call(False)[0])


SWA_SUB_BLOCKS = 2


def _swa_kernel(*refs, band, nsb):
    blk = WINDOW
    if band:
        sink_ref, *mask_refs = refs[:1 + nsb]
        q_ref, kp_ref, kz_ref, kn_ref, vp_ref, vz_ref, vn_ref, kc_ref, vc_ref, o_ref = refs[1 + nsb:]
    else:
        sink_ref, q_ref, kc_ref, vc_ref, o_ref = refs
    rows = SWA_GROUP * blk
    rowc = lax.broadcasted_iota(jnp.int32, (rows, 1), 0)
    for j in range(nsb):
        qr = slice(j * blk, (j + 1) * blk)
        for g in range(SWA_KV_HEADS):
            gs = slice(g * SLAB, (g + 1) * SLAB)
            q4 = jnp.concatenate([q_ref[qr, (g * SWA_GROUP + h) * SLAB:(g * SWA_GROUP + h + 1) * SLAB]
                                  for h in range(SWA_GROUP)], axis=0)
            if band:
                kparts = [kp_ref[:, gs]] + [kz_ref[i * blk:(i + 1) * blk, gs] for i in range(nsb)] + [kn_ref[:, gs]]
                vparts = [vp_ref[:, gs]] + [vz_ref[i * blk:(i + 1) * blk, gs] for i in range(nsb)] + [vn_ref[:, gs]]
                kcat = jnp.concatenate(kparts[j:j + 3] + [kc_ref[:, gs]], axis=0)
                vcat = jnp.concatenate(vparts[j:j + 3] + [vc_ref[:, gs]], axis=0)
            else:
                kcat, vcat = kc_ref[:, gs], vc_ref[:, gs]
            s = lax.dot_general(q4, kcat, (((1,), (1,)), ((), ())), preferred_element_type=F32)
            if band:
                s = s + mask_refs[j][...]
            sk = jnp.full((rows, 1), sink_ref[g * SWA_GROUP + SWA_GROUP - 1], F32)
            for h in range(SWA_GROUP - 2, -1, -1):
                sk = jnp.where(rowc < (h + 1) * blk, sink_ref[g * SWA_GROUP + h], sk)
            sk = sk * LOG2E
            m = jnp.maximum(jnp.max(s, axis=-1, keepdims=True), sk)
            p = jnp.exp2(s - m)
            l = jnp.sum(p, axis=-1, keepdims=True) + jnp.exp2(sk - m)
            o = jnp.dot(p.astype(BF), vcat, preferred_element_type=F32) / l
            for h in range(SWA_GROUP):
                hs = (g * SWA_GROUP + h) * SLAB
                o_ref[qr, hs:hs + SLAB] = o[h * blk:(h + 1) * blk, :].astype(BF)


def _swa_attention(p_q, p_ctx, sink, band):
    b, sq, _ = p_q.shape
    c = p_ctx.shape[1]
    blk = WINDOW
    nb = sq // blk
    nsb = SWA_SUB_BLOCKS if nb % SWA_SUB_BLOCKS == 0 else 1
    qw = SWA_HEADS * SLAB
    kvw = SWA_KV_HEADS * SLAB
    in_specs = [
        pl.BlockSpec(memory_space=pltpu.SMEM),
        pl.BlockSpec((None, nsb * blk, qw), lambda bi, n: (bi, n, SQ0 * SLAB // qw)),
    ]
    args = [sink, p_q]
    if band:
        r = np.arange(SWA_GROUP * blk)[:, None] % blk
        col = np.arange(3 * blk + c)[None, :]
        in_band = (np.abs(r - (col - blk)) <= WINDOW) | (col >= 3 * blk)
        variants = [in_band & ~((col < blk) & bool(v & 1)) & ~((col >= 2 * blk) & (col < 3 * blk) & bool(v & 2))
                    for v in range(4)]
        mask = jnp.asarray(np.where(np.stack(variants), 0.0, NEG_INF).astype(np.float32))
        for j in range(nsb):
            in_specs.insert(1 + j, pl.BlockSpec(
                (None,) + mask.shape[1:],
                lambda bi, n, j=j: (jnp.where(n * nsb + j == 0, 1, 0) + jnp.where(n * nsb + j == nb - 1, 2, 0), 0, 0)))
            args.insert(1 + j, mask)
        for base in (SK0 * SLAB // kvw, SV0 * SLAB // kvw):
            in_specs += [
                pl.BlockSpec((None, blk, kvw), lambda bi, n, base=base: (bi, jnp.maximum(n * nsb - 1, 0), base)),
                pl.BlockSpec((None, nsb * blk, kvw), lambda bi, n, base=base: (bi, n, base)),
                pl.BlockSpec((None, blk, kvw), lambda bi, n, base=base: (bi, jnp.minimum((n + 1) * nsb, nb - 1), base)),
            ]
            args += [p_q, p_q, p_q]
    in_specs += [pl.BlockSpec((None, c, kvw), lambda bi, n: (bi, 0, SK0 * SLAB // kvw)),
                 pl.BlockSpec((None, c, kvw), lambda bi, n: (bi, 0, SV0 * SLAB // kvw))]
    args += [p_ctx, p_ctx]
    return pl.pallas_call(
        functools.partial(_swa_kernel, band=band, nsb=nsb),
        grid=(b, nb // nsb),
        in_specs=in_specs,
        out_specs=pl.BlockSpec((None, nsb * blk, qw), lambda bi, n: (bi, n, 0)),
        out_shape=jax.ShapeDtypeStruct((b, sq, qw), BF),
        compiler_params=_params(("arbitrary", "arbitrary")),
        name="swa_attn" if band else "swa_attn_ctx",
    )(*args)


def _matmul_post_norm(operands, x_ref, gt_ref, g_ref, o_ref, f_scr, bm):
    gt, g = gt_ref[...], g_ref[...]
    rc = min(bm, ROW_CHUNK)
    for lo in range(0, bm, rc):
        rows = slice(lo, lo + rc)
        f_scr[rows, :] = sum(jnp.dot(lhs[rows, :], w[...], preferred_element_type=F32) for lhs, w in operands)
        for r in range(lo, lo + rc, NORM_CHUNK):
            sub = slice(r, r + NORM_CHUNK)
            f = f_scr[sub, :]
            ms = jnp.mean(f * f, axis=-1, keepdims=True)
            o_ref[sub, :] = x_ref[sub, :] + gt * (f * lax.rsqrt(ms + EPS) * g)


def _outproj_kernel(od_ref, os_ref, wt_ref, wb_ref, x_ref, gt_ref, g_ref, o_ref, mix_scr, *, bm):
    _matmul_post_norm(((od_ref, wt_ref), (os_ref, wb_ref)), x_ref, gt_ref, g_ref, o_ref, mix_scr, bm)


def _outproj(od, osw, w_out, x, mods, layer, row_fn, g_post, bm):
    rows, d = x.shape
    half = od.shape[1]
    return pl.pallas_call(
        functools.partial(_outproj_kernel, bm=bm),
        grid=(rows // bm,),
        in_specs=[
            pl.BlockSpec((bm, half), lambda i: (i, 0)),
            pl.BlockSpec((bm, half), lambda i: (i, 0)),
            _resident((None, half, d), lambda i: (layer, 0, 0)),
            _resident((None, half, d), lambda i: (layer, 1, 0)),
            pl.BlockSpec((bm, d), lambda i: (i, 0)),
            _mod_spec(layer, 2, row_fn, 1)(d),
            pl.BlockSpec((1, d), lambda i: (0, 0)),
        ],
        out_specs=pl.BlockSpec((bm, d), lambda i: (i, 0)),
        out_shape=jax.ShapeDtypeStruct((rows, d), F32),
        scratch_shapes=[pltpu.VMEM((bm, d), F32)],
        compiler_params=_params(("arbitrary",)),
        name="outproj",
    )(od, osw, w_out, w_out, x, mods, g_post)


HALO = 16
CONV_CHUNK = 512


def _ffn_up_kernel(x_ref, xp_ref, xn_ref, sh_ref, sc_ref, g_ref, wg_ref, wu_ref, cw_ref, cb_ref, o_ref,
                   h_scr, *, bm, seq):
    i = pl.program_id(0)
    j = pl.program_id(1)

    @pl.when(j == 0)
    def _():
        g, sh, sc = g_ref[...], sh_ref[...], sc_ref[...]

        def norm_rows(r, carry):
            rows = pl.multiple_of(r * NORM_CHUNK, NORM_CHUNK)
            h_scr[pl.ds(HALO + rows, NORM_CHUNK), :] = _rms_mod(x_ref[pl.ds(rows, NORM_CHUNK), :], g, sh, sc).astype(BF)
            return carry

        lax.fori_loop(0, bm // NORM_CHUNK, norm_rows, 0)
        prev_ok = ((i * bm) % seq != 0).astype(F32)
        next_ok = (((i + 1) * bm) % seq != 0).astype(F32)
        h_scr[0:HALO, :] = (_rms_mod(xp_ref[...], g, sh, sc) * prev_ok).astype(BF)
        h_scr[HALO + bm:2 * HALO + bm, :] = (_rms_mod(xn_ref[...], g, sh, sc) * next_ok).astype(BF)

    cw0, cw1, cw2, cb = cw_ref[0:1, :], cw_ref[1:2, :], cw_ref[2:3, :], cb_ref[...]
    rc = min(bm, CONV_CHUNK)
    pad = 8
    for r in range(bm // rc):
        lo = r * rc
        g = jnp.dot(h_scr[lo:lo + rc + 2 * HALO, :], wg_ref[...], preferred_element_type=F32)
        u = jnp.dot(h_scr[lo + HALO:lo + HALO + rc, :], wu_ref[...], preferred_element_type=F32)
        win = g[HALO - pad:HALO + rc + pad]
        below = pltpu.roll(win, 1, axis=0)[pad:pad + rc]
        above = pltpu.roll(win, rc + 2 * pad - 1, axis=0)[pad:pad + rc]
        gc = below * cw0 + g[HALO:HALO + rc] * cw1 + above * cw2 + cb
        a = gc * _sigmoid(gc) * u
        o_ref[lo:lo + rc, :] = a.astype(BF)


def _ffn_up(x, mods, layer, row_fn, g_pre, wg, wu, cw, cb, seq, bm, bn):
    rows, d = x.shape
    dff = wg.shape[2]
    hb = bm // HALO
    nh = rows // HALO
    mod = lambda k: _mod_spec(layer, k, row_fn, 2)(d)
    return pl.pallas_call(
        functools.partial(_ffn_up_kernel, bm=bm, seq=seq),
        grid=(rows // bm, dff // bn),
        in_specs=[
            pl.BlockSpec((bm, d), lambda i, j: (i, 0)),
            pl.BlockSpec((HALO, d), lambda i, j: (jnp.maximum(i * hb - 1, 0), 0)),
            pl.BlockSpec((HALO, d), lambda i, j: (jnp.minimum((i + 1) * hb, nh - 1), 0)),
            mod(3), mod(4),
            pl.BlockSpec((1, d), lambda i, j: (0, 0)),
            pl.BlockSpec((None, d, bn), lambda i, j: (layer, 0, j)),
            pl.BlockSpec((None, d, bn), lambda i, j: (layer, 0, j)),
            pl.BlockSpec((3, bn), lambda i, j: (0, j)),
            pl.BlockSpec((1, bn), lambda i, j: (0, j)),
        ],
        out_specs=pl.BlockSpec((bm, bn), lambda i, j: (i, j)),
        out_shape=jax.ShapeDtypeStruct((rows, dff), BF),
        scratch_shapes=[pltpu.VMEM((bm + 2 * HALO, d), BF)],
        compiler_params=_params(("arbitrary", "arbitrary")),
        name="ffn_up",
    )(x, x, x, mods, mods, g_pre, wg, wu, cw, cb)


def _ffn_down_kernel(a_ref, w_ref, x_ref, gt_ref, g_ref, o_ref, f_scr, *, bm):
    _matmul_post_norm(((a_ref, w_ref),), x_ref, gt_ref, g_ref, o_ref, f_scr, bm)


def _ffn_down(a, wd, x, mods, layer, row_fn, g_post, bm):
    rows, d = x.shape
    dff = a.shape[1]
    return pl.pallas_call(
        functools.partial(_ffn_down_kernel, bm=bm),
        grid=(rows // bm,),
        in_specs=[
            pl.BlockSpec((bm, dff), lambda i: (i, 0)),
            _resident((None, dff, d), lambda i: (layer, 0, 0)),
            pl.BlockSpec((bm, d), lambda i: (i, 0)),
            _mod_spec(layer, 5, row_fn, 1)(d),
            pl.BlockSpec((1, d), lambda i: (0, 0)),
        ],
        out_specs=pl.BlockSpec((bm, d), lambda i: (i, 0)),
        out_shape=jax.ShapeDtypeStruct((rows, d), F32),
        scratch_shapes=[pltpu.VMEM((bm, d), F32)],
        compiler_params=_params(("arbitrary",)),
        name="ffn_down",
    )(a, wd, x, mods, g_post)


def _rope_tables(seq, ctx_len):
    t = np.arange(seq)
    row, col = t // GRID_W, t % GRID_W

    def table(half):
        inv = ROPE_THETA ** (-np.arange(half, dtype=np.float64) / half)
        ar, ac = row[:, None] * inv[None, :], col[:, None] * inv[None, :]
        cos = np.concatenate([np.cos(ar), np.cos(ar), np.cos(ac), np.cos(ac)], axis=1)
        sin = np.concatenate([-np.sin(ar), np.sin(ar), -np.sin(ac), np.sin(ac)], axis=1)
        reps = SLAB // cos.shape[1]
        return np.tile(cos, (1, reps)).astype(np.float32), np.tile(sin, (1, reps)).astype(np.float32)

    cd, sd = table(DIFF_QK_DIM // 4)
    cs, ss = table(SWA_HEAD_DIM // 4)
    lat = tuple(jnp.asarray(a) for a in (cd, sd, cs, ss))
    one, zero = jnp.ones((ctx_len, SLAB), F32), jnp.zeros((ctx_len, SLAB), F32)
    return lat, (one, zero, one, zero)


def _pick(n, prefs):
    for p in prefs:
        if n % p == 0:
            return p
    raise ValueError(f"no block size in {prefs} divides {n}")


def kernel(x, c, ctx, c_ctx, w_ada, b_ada, g_pre_mix, g_post_mix, w_in, diff_lambda, diff_subln, swa_sink,
           w_out, g_pre_ffn, g_post_ffn, w_ffn_gate, w_ffn_up, ffn_conv_w, ffn_conv_b, w_ffn_down):
    b, s, d = x.shape
    cl = ctx.shape[1]
    depth = w_ada.shape[0]
    dff = w_ffn_gate.shape[2]
    assert w_in.shape[2] == IN_W and s % WINDOW == 0 and cl % WINDOW == 0

    n_rows = -(-(b + 1) // 8) * 8
    cvec = jnp.concatenate([c, c_ctx[None, :], jnp.zeros((n_rows - b - 1, d), F32)], axis=0)
    mods = _ada(cvec, w_ada, b_ada).reshape(depth, n_rows, 6, 1, d)

    lat_tabs, ctx_tabs = _rope_tables(s, cl)
    bm_in = _pick(s, (512, 256, 128))
    bm_out = _pick(s, (512, 256, 128))
    bm_up = _pick(s, (1024, 512, 256, 128))
    bm_dn = _pick(s, (256, 128))
    bm_c = _pick(cl, (256, 128))
    bn_ff = _pick(dff, (512, 256, 128))
    bq = _pick(s, (256, 128))
    bk = _pick(s, (512, 256, 128))
    bq_c = _pick(cl, (256, 128))

    xl = x.reshape(b * s, d)
    xc = ctx.reshape(b * cl, d)
    ctx_row = lambda i: b

    w_in_l, w_out_l = w_in.astype(BF), w_out.astype(BF)
    wg_l, wu_l, wd_l = w_ffn_gate.astype(BF), w_ffn_up.astype(BF), w_ffn_down.astype(BF)

    for l in range(depth):
        last = l == depth - 1
        lam_init = 0.8 - 0.6 * math.exp(-0.3 * l)
        lam_par = jnp.concatenate([diff_lambda[l], jnp.full_like(diff_lambda[l], lam_init)], axis=0)
        gpm, gqm = g_pre_mix[l][None, :], g_post_mix[l][None, :]
        gpf, gqf = g_pre_ffn[l][None, :], g_post_ffn[l][None, :]
        sub = diff_subln[l][None, :]
        cb = ffn_conv_b[l][None, :]

        p_lat = _inproj(xl, mods, l, lambda i, n=s // bm_in: i // n, gpm, w_in_l, lat_tabs, s, bm_in)
        p_ctx = _inproj(xc, mods, l, ctx_row, gpm, w_in_l, ctx_tabs, cl, bm_c)
        p_lat3, p_ctx3 = p_lat.reshape(b, s, IN_W), p_ctx.reshape(b, cl, IN_W)
        od = _diff_attention(p_lat3, p_lat3, p_ctx3, lam_par, sub, bq, bk)
        osw = _swa_attention(p_lat3, p_ctx3, swa_sink[l], band=True)
        xl = _outproj(od.reshape(b * s, -1), osw.reshape(b * s, -1), w_out_l, xl, mods, l,
                      lambda i, n=s // bm_out: i // n, gqm, bm_out)
        if not last:
            odc = _diff_attention(p_ctx3, None, p_ctx3, lam_par, sub, bq_c, bk)
            oswc = _swa_attention(p_ctx3, p_ctx3, swa_sink[l], band=False)
            xc = _outproj(odc.reshape(b * cl, -1), oswc.reshape(b * cl, -1), w_out_l, xc, mods, l, ctx_row, gqm, bm_c)

        a = _ffn_up(xl, mods, l, lambda i, n=s // bm_up: i // n, gpf, wg_l, wu_l, ffn_conv_w[l], cb, s, bm_up, bn_ff)
        xl = _ffn_down(a, wd_l, xl, mods, l, lambda i, n=s // bm_dn: i // n, gqf, bm_dn)
        if not last:
            ac = _ffn_up(xc, mods, l, ctx_row, gpf, wg_l, wu_l, ffn_conv_w[l], cb, cl, bm_c, bn_ff)
            xc = _ffn_down(ac, wd_l, xc, mods, l, ctx_row, gqf, bm_c)
    return xl.reshape(b, s, d)
```

```python
import functools
import math

import jax
import jax.numpy as jnp
import numpy as np
from jax import lax
from jax.experimental import pallas as pl
from jax.experimental.pallas import tpu as pltpu

BF = jnp.bfloat16
F32 = jnp.float32

GRID_W = 64
DIFF_HEADS = 8
DIFF_QK_DIM = 64
DIFF_V_DIM = 128
SWA_HEADS = 8
SWA_KV_HEADS = 2
SWA_GROUP = SWA_HEADS // SWA_KV_HEADS
SWA_HEAD_DIM = 128
WINDOW = 128
ROPE_THETA = 10000.0
EPS = 1e-6
NEG_INF = -1e30
LOG2E = 1.4426950408889634
DIFF_QSCALE = DIFF_QK_DIM ** -0.5 * LOG2E
SWA_QSCALE = SWA_HEAD_DIM ** -0.5 * LOG2E

SLAB = 128
DQ0, DK0, DV0, SQ0, SK0, SV0 = 0, 8, 16, 24, 32, 34
N_SLABS = 36
IN_W = N_SLABS * SLAB

VMEM_LIMIT = 56 * 1024 * 1024
NORM_CHUNK = 64
ROW_CHUNK = 256


def _params(sem, vmem=VMEM_LIMIT, flags=None):
    return pltpu.CompilerParams(dimension_semantics=sem, vmem_limit_bytes=vmem, flags=flags)


def _resident(shape, index_map):
    return pl.BlockSpec(shape, index_map, pipeline_mode=pl.Buffered(1))


def _sigmoid(v):
    return 1.0 / (1.0 + jnp.exp(-v))


def _rms_mod(xf, g, sh, sc):
    ms = jnp.mean(xf * xf, axis=-1, keepdims=True)
    return (xf * lax.rsqrt(ms + EPS) * g) * (1.0 + sc) + sh


def _rope(x, cos, sin, shift):
    lane = lax.broadcasted_iota(jnp.int32, x.shape, 1)
    fwd = pltpu.roll(x, SLAB - shift, axis=1)
    bwd = pltpu.roll(x, shift, axis=1)
    partner = jnp.where((lane & (2 * shift - 1)) < shift, fwd, bwd)
    return x * cos + partner * sin


def _ada_kernel(c_ref, w_ref, b_ref, o_ref):
    cv = c_ref[...]
    s = (cv * _sigmoid(cv)).astype(BF)
    o_ref[...] = jnp.dot(s, w_ref[...].astype(BF), preferred_element_type=F32) + b_ref[...]


def _ada(cvec, w_ada, b_ada):
    depth, d, n = w_ada.shape
    rows = cvec.shape[0]
    bn = 1024
    return pl.pallas_call(
        _ada_kernel,
        grid=(depth, n // bn),
        in_specs=[
            pl.BlockSpec((rows, d), lambda l, j: (0, 0)),
            pl.BlockSpec((None, d, bn), lambda l, j: (l, 0, j)),
            pl.BlockSpec((None, 1, bn), lambda l, j: (l, 0, j)),
        ],
        out_specs=pl.BlockSpec((None, rows, bn), lambda l, j: (l, 0, j)),
        out_shape=jax.ShapeDtypeStruct((depth, rows, n), F32),
        compiler_params=_params(("arbitrary", "arbitrary")),
        name="ada",
    )(cvec, w_ada, b_ada.reshape(depth, 1, n))


def _mod_spec(layer, k, row_fn, nargs):
    if nargs == 1:
        return lambda d: pl.BlockSpec((None, None, None, 1, d), lambda i: (layer, row_fn(i), k, 0, 0))
    return lambda d: pl.BlockSpec((None, None, None, 1, d), lambda i, j: (layer, row_fn(i), k, 0, 0))


def _inproj_kernel(x_ref, sh_ref, sc_ref, g_ref, w_ref, cd_ref, sd_ref, cs_ref, ss_ref, o_ref, h_scr, *, bm):
    g, sh, sc = g_ref[...], sh_ref[...], sc_ref[...]
    group = 4
    rc = min(bm, ROW_CHUNK)
    for lo in range(0, bm, rc):
        for r in range(lo, lo + rc, NORM_CHUNK):
            h_scr[r:r + NORM_CHUNK, :] = _rms_mod(x_ref[r:r + NORM_CHUNK, :], g, sh, sc).astype(BF)
        rows = slice(lo, lo + rc)
        for grp in range(N_SLABS // group):
            acc = jnp.dot(h_scr[rows, :], w_ref[:, grp * group * SLAB:(grp + 1) * group * SLAB],
                          preferred_element_type=F32)
            for t in range(group):
                slab = grp * group + t
                a = acc[:, t * SLAB:(t + 1) * SLAB]
                if slab < DV0:
                    a = _rope(a, cd_ref[rows, :], sd_ref[rows, :], DIFF_QK_DIM // 4)
                    if slab < DK0:
                        a = a * DIFF_QSCALE
                elif SQ0 <= slab < SV0:
                    a = _rope(a, cs_ref[rows, :], ss_ref[rows, :], SWA_HEAD_DIM // 4)
                    if slab < SK0:
                        a = a * SWA_QSCALE
                o_ref[rows, slab * SLAB:(slab + 1) * SLAB] = a.astype(BF)


def _inproj(x, mods, layer, row_fn, g_pre, w, tables, seq, bm):
    rows, d = x.shape
    nt = seq // bm
    mod = lambda k: _mod_spec(layer, k, row_fn, 1)(d)
    tab = pl.BlockSpec((bm, SLAB), lambda i: (i % nt, 0))
    return pl.pallas_call(
        functools.partial(_inproj_kernel, bm=bm),
        grid=(rows // bm,),
        in_specs=[
            pl.BlockSpec((bm, d), lambda i: (i, 0)),
            mod(0), mod(1),
            pl.BlockSpec((1, d), lambda i: (0, 0)),
            _resident((None, d, IN_W), lambda i: (layer, 0, 0)),
            tab, tab, tab, tab,
        ],
        out_specs=pl.BlockSpec((bm, IN_W), lambda i: (i, 0)),
        out_shape=jax.ShapeDtypeStruct((rows, IN_W), BF),
        scratch_shapes=[pltpu.VMEM((bm, d), BF)],
        compiler_params=_params(("arbitrary",)),
        name="inproj",
    )(x, mods, mods, g_pre, w, *tables)


ONES_ROWS = 16
DIFF_SUB_BLOCKS = 4
DIFF_UNROLL = 64
DIFF_HAZARD = 64.0


def _diff_kernel(*refs, n_lat, unroll, n_sub, lagged):
    n_out = 2 if lagged else 1
    ins, outs = refs[:-n_out], refs[-n_out:]
    if n_lat:
        lam_ref, sub_ref, q_ref, kl_ref, vl_ref, kc_ref, vc_ref = ins
    else:
        lam_ref, sub_ref, q_ref, kc_ref, vc_ref = ins
    o_ref = outs[0]
    lp = lam_ref[...]
    lam_init = lp[4:5, 0:1]
    lam = (jnp.exp(jnp.sum(lp[0:1] * lp[1:2], axis=-1, keepdims=True))
           - jnp.exp(jnp.sum(lp[2:3] * lp[3:4], axis=-1, keepdims=True)) + lam_init)

    bq = q_ref.shape[0] // n_sub
    n_chunks = n_lat + 1
    keys = lambda t: kc_ref[...] if isinstance(t, int) and t == n_lat else kl_ref[t]
    values = lambda t: vc_ref[...] if isinstance(t, int) and t == n_lat else vl_ref[t]

    def run_steps(step, carry, first, n_loop):
        trips = n_loop // unroll

        def body(i, carry):
            for u in range(unroll):
                carry = step(first + i * unroll + u, carry)
            return carry

        if trips:
            carry = lax.fori_loop(0, trips, body, carry)
        for t in range(first + trips * unroll, n_chunks):
            carry = step(t, carry)
        return carry

    def one_block(sb):
        rows = slice(sb * bq, (sb + 1) * bq)
        q = q_ref[rows, :].astype(F32)
        lane = lax.broadcasted_iota(jnp.int32, q.shape, 1)
        q_both = jnp.concatenate([jnp.where(lane < DIFF_QK_DIM, q, 0.0), jnp.where(lane >= DIFF_QK_DIM, q, 0.0)],
                                 axis=0)
        q_t = jnp.transpose(q_both).astype(BF)

        def qk(t):
            return jnp.dot(keys(t), q_t, preferred_element_type=F32)

        def pv(t, p):
            v_t = values(t)
            v_ext = jnp.concatenate([v_t, jnp.ones((ONES_ROWS, v_t.shape[1]), BF)], axis=0)
            return jnp.dot(v_ext, p, preferred_element_type=F32)

        colmax = lambda s: jnp.max(s, axis=0, keepdims=True)

        if lagged:
            s = qk(0)
            r = colmax(s)
            acc = jnp.zeros((DIFF_V_DIM + ONES_ROWS, 2 * bq), F32)

            def absorb(t, s, r_prev, r, acc, excess):
                p = jnp.exp2(s - r).astype(BF)
                cm = colmax(s)
                acc = jnp.exp2(r_prev - r) * acc + pv(t, p)
                return r, jnp.maximum(r, cm), acc, jnp.maximum(excess, cm - r)

            def step(t, carry):
                s, state = carry
                s_new = qk(t)
                return s_new, absorb(t - 1, s, *state)

            carry = (s, (r, r, acc, jnp.zeros_like(r)))
            s, state = run_steps(step, carry, 1, max(n_lat - 1, 0))
            _, _, acc, excess = absorb(n_chunks - 1, s, *state)
            outs[1][sb:sb + 1, :] = excess
        else:
            def scores(t):
                s = qk(t)
                return s, colmax(s)

            def softmax(sc, m):
                s, smax = sc
                mn = jnp.maximum(m, smax)
                return mn, jnp.exp2(m - mn), jnp.exp2(s - mn).astype(BF)

            m = jnp.full((1, 2 * bq), NEG_INF, F32)
            acc = jnp.zeros((DIFF_V_DIM + ONES_ROWS, 2 * bq), F32)
            sc = scores(0)
            if n_chunks > 1:
                sc_next = scores(1)
                m, alpha, p = softmax(sc, m)

                def step(t, carry):
                    sc, p, alpha, m, acc = carry
                    sc_new = scores(t)
                    m_new, alpha_new, p_new = softmax(sc, m)
                    acc = alpha * acc + pv(t - 2, p)
                    return sc_new, p_new, alpha_new, m_new, acc

                sc, p, alpha, m, acc = run_steps(step, (sc_next, p, alpha, m, acc), 2, max(n_lat - 2, 0))
                acc = alpha * acc + pv(n_chunks - 2, p)
            m, alpha, p = softmax(sc, m)
            acc = alpha * acc + pv(n_chunks - 1, p)

        o_t = acc[:DIFF_V_DIM] / acc[DIFF_V_DIM:DIFF_V_DIM + 1]
        o = jnp.transpose(o_t[:, :bq]) - lam * jnp.transpose(o_t[:, bq:])
        ms = jnp.mean(o * o, axis=-1, keepdims=True)
        o_ref[rows, :] = (o * lax.rsqrt(ms + EPS) * sub_ref[...] * (1.0 - lam_init)).astype(BF)

    for sb in range(n_sub):
        one_block(sb)


def _diff_attention(p_q, p_lat, p_ctx, lam_par, subln, bq, bk):
    b, sq, _ = p_q.shape
    c = p_ctx.shape[1]
    vw = DIFF_HEADS * DIFF_V_DIM
    n_lat = 0 if p_lat is None else p_lat.shape[1] // bk
    n_sub = DIFF_SUB_BLOCKS if sq % (DIFF_SUB_BLOCKS * bq) == 0 else 1
    qb = n_sub * bq
    in_specs = [
        pl.BlockSpec(lam_par.shape, lambda bi, h, qi: (0, 0)),
        pl.BlockSpec((1, DIFF_V_DIM), lambda bi, h, qi: (0, 0)),
        pl.BlockSpec((None, qb, SLAB), lambda bi, h, qi: (bi, qi, DQ0 + h)),
    ]
    args = [lam_par, subln, p_q]
    values_of = lambda p: p[:, :, DV0 * SLAB:DV0 * SLAB + vw]
    if n_lat:
        k_lat = p_lat.reshape(b, n_lat, bk, IN_W)
        v_lat = jnp.transpose(values_of(p_lat).reshape(b, n_lat, bk, DIFF_HEADS, DIFF_V_DIM), (0, 3, 1, 4, 2))
        in_specs += [pl.BlockSpec((None, n_lat, bk, SLAB), lambda bi, h, qi: (bi, 0, 0, DK0 + h)),
                     pl.BlockSpec((None, None, n_lat, DIFF_V_DIM, bk), lambda bi, h, qi: (bi, h, 0, 0, 0))]
        args += [k_lat, v_lat]
    v_ctx = jnp.transpose(values_of(p_ctx).reshape(b, c, DIFF_HEADS, DIFF_V_DIM), (0, 2, 3, 1))
    in_specs += [pl.BlockSpec((None, c, SLAB), lambda bi, h, qi: (bi, 0, DK0 + h)),
                 pl.BlockSpec((None, None, DIFF_V_DIM, c), lambda bi, h, qi: (bi, h, 0, 0))]
    args += [p_ctx, v_ctx]
    nq = sq // qb

    def call(lagged):
        out_specs = [pl.BlockSpec((None, qb, SLAB), lambda bi, h, qi: (bi, qi, h))]
        out_shape = [jax.ShapeDtypeStruct((b, sq, vw), BF)]
        if lagged:
            out_specs.append(pl.BlockSpec((None, None, None, n_sub, 2 * bq), lambda bi, h, qi: (bi, h, qi, 0, 0)))
            out_shape.append(jax.ShapeDtypeStruct((b, DIFF_HEADS, nq, n_sub, 2 * bq), F32))
        return pl.pallas_call(
            functools.partial(_diff_kernel, n_lat=n_lat, unroll=DIFF_UNROLL, n_sub=n_sub, lagged=lagged),
            grid=(b, DIFF_HEADS, nq),
            in_specs=in_specs,
            out_specs=out_specs,
            out_shape=out_shape,
            compiler_params=_params(("arbitrary", "arbitrary", "arbitrary")),
            name=("diff_attn_lagged" if lagged else "diff_attn") if n_lat else "diff_attn_ctx",
        )(*args)

    if not n_lat:
        return call(False)[0]
    od, excess = call(True)
    safe = jnp.max(excess) <= DIFF_HAZARD
    return lax.cond(safe, lambda: od, lambda: call(False)[0])


SWA_SUB_BLOCKS = 2


def _swa_kernel(*refs, band, nsb):
    blk = WINDOW
    if band:
        sink_ref, *mask_refs = refs[:1 + nsb]
        q_ref, kp_ref, kz_ref, kn_ref, vp_ref, vz_ref, vn_ref, kc_ref, vc_ref, o_ref = refs[1 + nsb:]
    else:
        sink_ref, q_ref, kc_ref, vc_ref, o_ref = refs
    rows = SWA_GROUP * blk
    rowc = lax.broadcasted_iota(jnp.int32, (rows, 1), 0)
    for j in range(nsb):
        qr = slice(j * blk, (j + 1) * blk)
        for g in range(SWA_KV_HEADS):
            gs = slice(g * SLAB, (g + 1) * SLAB)
            q4 = jnp.concatenate([q_ref[qr, (g * SWA_GROUP + h) * SLAB:(g * SWA_GROUP + h + 1) * SLAB]
                                  for h in range(SWA_GROUP)], axis=0)
            if band:
                kparts = [kp_ref[:, gs]] + [kz_ref[i * blk:(i + 1) * blk, gs] for i in range(nsb)] + [kn_ref[:, gs]]
                vparts = [vp_ref[:, gs]] + [vz_ref[i * blk:(i + 1) * blk, gs] for i in range(nsb)] + [vn_ref[:, gs]]
                kcat = jnp.concatenate(kparts[j:j + 3] + [kc_ref[:, gs]], axis=0)
                vcat = jnp.concatenate(vparts[j:j + 3] + [vc_ref[:, gs]], axis=0)
            else:
                kcat, vcat = kc_ref[:, gs], vc_ref[:, gs]
            s = lax.dot_general(q4, kcat, (((1,), (1,)), ((), ())), preferred_element_type=F32)
            if band:
                s = s + mask_refs[j][...]
            sk = jnp.full((rows, 1), sink_ref[g * SWA_GROUP + SWA_GROUP - 1], F32)
            for h in range(SWA_GROUP - 2, -1, -1):
                sk = jnp.where(rowc < (h + 1) * blk, sink_ref[g * SWA_GROUP + h], sk)
            sk = sk * LOG2E
            m = jnp.maximum(jnp.max(s, axis=-1, keepdims=True), sk)
            p = jnp.exp2(s - m)
            l = jnp.sum(p, axis=-1, keepdims=True) + jnp.exp2(sk - m)
            o = jnp.dot(p.astype(BF), vcat, preferred_element_type=F32) / l
            for h in range(SWA_GROUP):
                hs = (g * SWA_GROUP + h) * SLAB
                o_ref[qr, hs:hs + SLAB] = o[h * blk:(h + 1) * blk, :].astype(BF)


def _swa_attention(p_q, p_ctx, sink, band):
    b, sq, _ = p_q.shape
    c = p_ctx.shape[1]
    blk = WINDOW
    nb = sq // blk
    nsb = SWA_SUB_BLOCKS if nb % SWA_SUB_BLOCKS == 0 else 1
    qw = SWA_HEADS * SLAB
    kvw = SWA_KV_HEADS * SLAB
    in_specs = [
        pl.BlockSpec(memory_space=pltpu.SMEM),
        pl.BlockSpec((None, nsb * blk, qw), lambda bi, n: (bi, n, SQ0 * SLAB // qw)),
    ]
    args = [sink, p_q]
    if band:
        r = np.arange(SWA_GROUP * blk)[:, None] % blk
        col = np.arange(3 * blk + c)[None, :]
        in_band = (np.abs(r - (col - blk)) <= WINDOW) | (col >= 3 * blk)
        variants = [in_band & ~((col < blk) & bool(v & 1)) & ~((col >= 2 * blk) & (col < 3 * blk) & bool(v & 2))
                    for v in range(4)]
        mask = jnp.asarray(np.where(np.stack(variants), 0.0, NEG_INF).astype(np.float32))
        for j in range(nsb):
            in_specs.insert(1 + j, pl.BlockSpec(
                (None,) + mask.shape[1:],
                lambda bi, n, j=j: (jnp.where(n * nsb + j == 0, 1, 0) + jnp.where(n * nsb + j == nb - 1, 2, 0), 0, 0)))
            args.insert(1 + j, mask)
        for base in (SK0 * SLAB // kvw, SV0 * SLAB // kvw):
            in_specs += [
                pl.BlockSpec((None, blk, kvw), lambda bi, n, base=base: (bi, jnp.maximum(n * nsb - 1, 0), base)),
                pl.BlockSpec((None, nsb * blk, kvw), lambda bi, n, base=base: (bi, n, base)),
                pl.BlockSpec((None, blk, kvw), lambda bi, n, base=base: (bi, jnp.minimum((n + 1) * nsb, nb - 1), base)),
            ]
            args += [p_q, p_q, p_q]
    in_specs += [pl.BlockSpec((None, c, kvw), lambda bi, n: (bi, 0, SK0 * SLAB // kvw)),
                 pl.BlockSpec((None, c, kvw), lambda bi, n: (bi, 0, SV0 * SLAB // kvw))]
    args += [p_ctx, p_ctx]
    return pl.pallas_call(
        functools.partial(_swa_kernel, band=band, nsb=nsb),
        grid=(b, nb // nsb),
        in_specs=in_specs,
        out_specs=pl.BlockSpec((None, nsb * blk, qw), lambda bi, n: (bi, n, 0)),
        out_shape=jax.ShapeDtypeStruct((b, sq, qw), BF),
        compiler_params=_params(("arbitrary", "arbitrary")),
        name="swa_attn" if band else "swa_attn_ctx",
    )(*args)


def _matmul_post_norm(operands, x_ref, gt_ref, g_ref, o_ref, f_scr, bm):
    gt, g = gt_ref[...], g_ref[...]
    rc = min(bm, ROW_CHUNK)
    for lo in range(0, bm, rc):
        rows = slice(lo, lo + rc)
        f_scr[rows, :] = sum(jnp.dot(lhs[rows, :], w[...], preferred_element_type=F32) for lhs, w in operands)
        for r in range(lo, lo + rc, NORM_CHUNK):
            sub = slice(r, r + NORM_CHUNK)
            f = f_scr[sub, :]
            ms = jnp.mean(f * f, axis=-1, keepdims=True)
            o_ref[sub, :] = x_ref[sub, :] + gt * (f * lax.rsqrt(ms + EPS) * g)


def _outproj_kernel(od_ref, os_ref, wt_ref, wb_ref, x_ref, gt_ref, g_ref, o_ref, mix_scr, *, bm):
    _matmul_post_norm(((od_ref, wt_ref), (os_ref, wb_ref)), x_ref, gt_ref, g_ref, o_ref, mix_scr, bm)


def _outproj(od, osw, w_out, x, mods, layer, row_fn, g_post, bm):
    rows, d = x.shape
    half = od.shape[1]
    return pl.pallas_call(
        functools.partial(_outproj_kernel, bm=bm),
        grid=(rows // bm,),
        in_specs=[
            pl.BlockSpec((bm, half), lambda i: (i, 0)),
            pl.BlockSpec((bm, half), lambda i: (i, 0)),
            _resident((None, half, d), lambda i: (layer, 0, 0)),
            _resident((None, half, d), lambda i: (layer, 1, 0)),
            pl.BlockSpec((bm, d), lambda i: (i, 0)),
            _mod_spec(layer, 2, row_fn, 1)(d),
            pl.BlockSpec((1, d), lambda i: (0, 0)),
        ],
        out_specs=pl.BlockSpec((bm, d), lambda i: (i, 0)),
        out_shape=jax.ShapeDtypeStruct((rows, d), F32),
        scratch_shapes=[pltpu.VMEM((bm, d), F32)],
        compiler_params=_params(("arbitrary",)),
        name="outproj",
    )(od, osw, w_out, w_out, x, mods, g_post)


HALO = 16
CONV_CHUNK = 512


def _ffn_up_kernel(x_ref, xp_ref, xn_ref, sh_ref, sc_ref, g_ref, wg_ref, wu_ref, cw_ref, cb_ref, o_ref,
                   h_scr, *, bm, seq):
    i = pl.program_id(0)
    j = pl.program_id(1)

    @pl.when(j == 0)
    def _():
        g, sh, sc = g_ref[...], sh_ref[...], sc_ref[...]

        def norm_rows(r, carry):
            rows = pl.multiple_of(r * NORM_CHUNK, NORM_CHUNK)
            h_scr[pl.ds(HALO + rows, NORM_CHUNK), :] = _rms_mod(x_ref[pl.ds(rows, NORM_CHUNK), :], g, sh, sc).astype(BF)
            return carry

        lax.fori_loop(0, bm // NORM_CHUNK, norm_rows, 0)
        prev_ok = ((i * bm) % seq != 0).astype(F32)
        next_ok = (((i + 1) * bm) % seq != 0).astype(F32)
        h_scr[0:HALO, :] = (_rms_mod(xp_ref[...], g, sh, sc) * prev_ok).astype(BF)
        h_scr[HALO + bm:2 * HALO + bm, :] = (_rms_mod(xn_ref[...], g, sh, sc) * next_ok).astype(BF)

    cw0, cw1, cw2, cb = cw_ref[0:1, :], cw_ref[1:2, :], cw_ref[2:3, :], cb_ref[...]
    rc = min(bm, CONV_CHUNK)
    pad = 8
    for r in range(bm // rc):
        lo = r * rc
        g = jnp.dot(h_scr[lo:lo + rc + 2 * HALO, :], wg_ref[...], preferred_element_type=F32)
        u = jnp.dot(h_scr[lo + HALO:lo + HALO + rc, :], wu_ref[...], preferred_element_type=F32)
        win = g[HALO - pad:HALO + rc + pad]
        below = pltpu.roll(win, 1, axis=0)[pad:pad + rc]
        above = pltpu.roll(win, rc + 2 * pad - 1, axis=0)[pad:pad + rc]
        gc = below * cw0 + g[HALO:HALO + rc] * cw1 + above * cw2 + cb
        a = gc * _sigmoid(gc) * u
        o_ref[lo:lo + rc, :] = a.astype(BF)


def _ffn_up(x, mods, layer, row_fn, g_pre, wg, wu, cw, cb, seq, bm, bn):
    rows, d = x.shape
    dff = wg.shape[2]
    hb = bm // HALO
    nh = rows // HALO
    mod = lambda k: _mod_spec(layer, k, row_fn, 2)(d)
    return pl.pallas_call(
        functools.partial(_ffn_up_kernel, bm=bm, seq=seq),
        grid=(rows // bm, dff // bn),
        in_specs=[
            pl.BlockSpec((bm, d), lambda i, j: (i, 0)),
            pl.BlockSpec((HALO, d), lambda i, j: (jnp.maximum(i * hb - 1, 0), 0)),
            pl.BlockSpec((HALO, d), lambda i, j: (jnp.minimum((i + 1) * hb, nh - 1), 0)),
            mod(3), mod(4),
            pl.BlockSpec((1, d), lambda i, j: (0, 0)),
            pl.BlockSpec((None, d, bn), lambda i, j: (layer, 0, j)),
            pl.BlockSpec((None, d, bn), lambda i, j: (layer, 0, j)),
            pl.BlockSpec((3, bn), lambda i, j: (0, j)),
            pl.BlockSpec((1, bn), lambda i, j: (0, j)),
        ],
        out_specs=pl.BlockSpec((bm, bn), lambda i, j: (i, j)),
        out_shape=jax.ShapeDtypeStruct((rows, dff), BF),
        scratch_shapes=[pltpu.VMEM((bm + 2 * HALO, d), BF)],
        compiler_params=_params(("arbitrary", "arbitrary")),
        name="ffn_up",
    )(x, x, x, mods, mods, g_pre, wg, wu, cw, cb)


def _ffn_down_kernel(a_ref, w_ref, x_ref, gt_ref, g_ref, o_ref, f_scr, *, bm):
    _matmul_post_norm(((a_ref, w_ref),), x_ref, gt_ref, g_ref, o_ref, f_scr, bm)


def _ffn_down(a, wd, x, mods, layer, row_fn, g_post, bm):
    rows, d = x.shape
    dff = a.shape[1]
    return pl.pallas_call(
        functools.partial(_ffn_down_kernel, bm=bm),
        grid=(rows // bm,),
        in_specs=[
            pl.BlockSpec((bm, dff), lambda i: (i, 0)),
            _resident((None, dff, d), lambda i: (layer, 0, 0)),
            pl.BlockSpec((bm, d), lambda i: (i, 0)),
            _mod_spec(layer, 5, row_fn, 1)(d),
            pl.BlockSpec((1, d), lambda i: (0, 0)),
        ],
        out_specs=pl.BlockSpec((bm, d), lambda i: (i, 0)),
        out_shape=jax.ShapeDtypeStruct((rows, d), F32),
        scratch_shapes=[pltpu.VMEM((bm, d), F32)],
        compiler_params=_params(("arbitrary",)),
        name="ffn_down",
    )(a, wd, x, mods, g_post)


def _rope_tables(seq, ctx_len):
    t = np.arange(seq)
    row, col = t // GRID_W, t % GRID_W

    def table(half):
        inv = ROPE_THETA ** (-np.arange(half, dtype=np.float64) / half)
        ar, ac = row[:, None] * inv[None, :], col[:, None] * inv[None, :]
        cos = np.concatenate([np.cos(ar), np.cos(ar), np.cos(ac), np.cos(ac)], axis=1)
        sin = np.concatenate([-np.sin(ar), np.sin(ar), -np.sin(ac), np.sin(ac)], axis=1)
        reps = SLAB // cos.shape[1]
        return np.tile(cos, (1, reps)).astype(np.float32), np.tile(sin, (1, reps)).astype(np.float32)

    cd, sd = table(DIFF_QK_DIM // 4)
    cs, ss = table(SWA_HEAD_DIM // 4)
    lat = tuple(jnp.asarray(a) for a in (cd, sd, cs, ss))
    one, zero = jnp.ones((ctx_len, SLAB), F32), jnp.zeros((ctx_len, SLAB), F32)
    return lat, (one, zero, one, zero)


def _pick(n, prefs):
    for p in prefs:
        if n % p == 0:
            return p
    raise ValueError(f"no block size in {prefs} divides {n}")


def kernel(x, c, ctx, c_ctx, w_ada, b_ada, g_pre_mix, g_post_mix, w_in, diff_lambda, diff_subln, swa_sink,
           w_out, g_pre_ffn, g_post_ffn, w_ffn_gate, w_ffn_up, ffn_conv_w, ffn_conv_b, w_ffn_down):
    b, s, d = x.shape
    cl = ctx.shape[1]
    depth = w_ada.shape[0]
    dff = w_ffn_gate.shape[2]
    assert w_in.shape[2] == IN_W and s % WINDOW == 0 and cl % WINDOW == 0

    n_rows = -(-(b + 1) // 8) * 8
    cvec = jnp.concatenate([c, c_ctx[None, :], jnp.zeros((n_rows - b - 1, d), F32)], axis=0)
    mods = _ada(cvec, w_ada, b_ada).reshape(depth, n_rows, 6, 1, d)

    lat_tabs, ctx_tabs = _rope_tables(s, cl)
    bm_in = _pick(s, (512, 256, 128))
    bm_out = _pick(s, (512, 256, 128))
    bm_up = _pick(s, (1024, 512, 256, 128))
    bm_dn = _pick(s, (256, 128))
    bm_c = _pick(cl, (256, 128))
    bn_ff = _pick(dff, (512, 256, 128))
    bq = _pick(s, (256, 128))
    bk = _pick(s, (512, 256, 128))
    bq_c = _pick(cl, (256, 128))

    xl = x.reshape(b * s, d)
    xc = ctx.reshape(b * cl, d)
    ctx_row = lambda i: b

    w_in_l, w_out_l = w_in.astype(BF), w_out.astype(BF)
    wg_l, wu_l, wd_l = w_ffn_gate.astype(BF), w_ffn_up.astype(BF), w_ffn_down.astype(BF)

    for l in range(depth):
        last = l == depth - 1
        lam_init = 0.8 - 0.6 * math.exp(-0.3 * l)
        lam_par = jnp.concatenate([diff_lambda[l], jnp.full_like(diff_lambda[l], lam_init)], axis=0)
        gpm, gqm = g_pre_mix[l][None, :], g_post_mix[l][None, :]
        gpf, gqf = g_pre_ffn[l][None, :], g_post_ffn[l][None, :]
        sub = diff_subln[l][None, :]
        cb = ffn_conv_b[l][None, :]

        p_lat = _inproj(xl, mods, l, lambda i, n=s // bm_in: i // n, gpm, w_in_l, lat_tabs, s, bm_in)
        p_ctx = _inproj(xc, mods, l, ctx_row, gpm, w_in_l, ctx_tabs, cl, bm_c)
        p_lat3, p_ctx3 = p_lat.reshape(b, s, IN_W), p_ctx.reshape(b, cl, IN_W)
        od = _diff_attention(p_lat3, p_lat3, p_ctx3, lam_par, sub, bq, bk)
        osw = _swa_attention(p_lat3, p_ctx3, swa_sink[l], band=True)
        xl = _outproj(od.reshape(b * s, -1), osw.reshape(b * s, -1), w_out_l, xl, mods, l,
                      lambda i, n=s // bm_out: i // n, gqm, bm_out)
        if not last:
            odc = _diff_attention(p_ctx3, None, p_ctx3, lam_par, sub, bq_c, bk)
            oswc = _swa_attention(p_ctx3, p_ctx3, swa_sink[l], band=False)
            xc = _outproj(odc.reshape(b * cl, -1), oswc.reshape(b * cl, -1), w_out_l, xc, mods, l, ctx_row, gqm, bm_c)

        a = _ffn_up(xl, mods, l, lambda i, n=s // bm_up: i // n, gpf, wg_l, wu_l, ffn_conv_w[l], cb, s, bm_up, bn_ff)
        xl = _ffn_down(a, wd_l, xl, mods, l, lambda i, n=s // bm_dn: i // n, gqf, bm_dn)
        if not last:
            ac = _ffn_up(xc, mods, l, ctx_row, gpf, wg_l, wu_l, ffn_conv_w[l], cb, cl, bm_c, bn_ff)
            xc = _ffn_down(ac, wd_l, xc, mods, l, ctx_row, gqf, bm_c)
    return xl.reshape(b, s, d)
```

```python
import functools
import math

import jax
import jax.numpy as jnp
import numpy as np
from jax import lax
from jax.experimental import pallas as pl
from jax.experimental.pallas import tpu as pltpu

BF = jnp.bfloat16
F32 = jnp.float32

GRID_W = 64
DIFF_HEADS = 8
DIFF_QK_DIM = 64
DIFF_V_DIM = 128
SWA_HEADS = 8
SWA_KV_HEADS = 2
SWA_GROUP = SWA_HEADS // SWA_KV_HEADS
SWA_HEAD_DIM = 128
WINDOW = 128
ROPE_THETA = 10000.0
EPS = 1e-6
NEG_INF = -1e30
LOG2E = 1.4426950408889634
DIFF_QSCALE = DIFF_QK_DIM ** -0.5 * LOG2E
SWA_QSCALE = SWA_HEAD_DIM ** -0.5 * LOG2E

SLAB = 128
DQ0, DK0, DV0, SQ0, SK0, SV0 = 0, 8, 16, 24, 32, 34
N_SLABS = 36
IN_W = N_SLABS * SLAB

VMEM_LIMIT = 56 * 1024 * 1024
NORM_CHUNK = 64
ROW_CHUNK = 256


def _params(sem):
    return pltpu.CompilerParams(dimension_semantics=sem, vmem_limit_bytes=VMEM_LIMIT)


def _resident(shape, index_map):
    return pl.BlockSpec(shape, index_map, pipeline_mode=pl.Buffered(1))


def _sigmoid(v):
    return 1.0 / (1.0 + jnp.exp(-v))


def _rms_mod(xf, g, sh, sc):
    ms = jnp.mean(xf * xf, axis=-1, keepdims=True)
    return (xf * lax.rsqrt(ms + EPS) * g) * (1.0 + sc) + sh


def _rope(x, cos, sin, shift):
    lane = lax.broadcasted_iota(jnp.int32, x.shape, 1)
    fwd = pltpu.roll(x, SLAB - shift, axis=1)
    bwd = pltpu.roll(x, shift, axis=1)
    partner = jnp.where((lane & (2 * shift - 1)) < shift, fwd, bwd)
    return x * cos + partner * sin


def _ada_kernel(c_ref, w_ref, b_ref, o_ref):
    cv = c_ref[...]
    s = (cv * _sigmoid(cv)).astype(BF)
    o_ref[...] = jnp.dot(s, w_ref[...].astype(BF), preferred_element_type=F32) + b_ref[...]


def _ada(cvec, w_ada, b_ada):
    depth, d, n = w_ada.shape
    rows = cvec.shape[0]
    bn = _pick(n, (1024, 512, 256, 128))
    return pl.pallas_call(
        _ada_kernel,
        grid=(depth, n // bn),
        in_specs=[
            pl.BlockSpec((rows, d), lambda l, j: (0, 0)),
            pl.BlockSpec((None, d, bn), lambda l, j: (l, 0, j)),
            pl.BlockSpec((None, 1, bn), lambda l, j: (l, 0, j)),
        ],
        out_specs=pl.BlockSpec((None, rows, bn), lambda l, j: (l, 0, j)),
        out_shape=jax.ShapeDtypeStruct((depth, rows, n), F32),
        compiler_params=_params(("arbitrary", "arbitrary")),
        name="ada",
    )(cvec, w_ada, b_ada.reshape(depth, 1, n))


def _mod_spec(layer, k, row_fn, nargs):
    if nargs == 1:
        return lambda d: pl.BlockSpec((None, None, None, 1, d), lambda i: (layer, row_fn(i), k, 0, 0))
    return lambda d: pl.BlockSpec((None, None, None, 1, d), lambda i, j: (layer, row_fn(i), k, 0, 0))


def _inproj_kernel(x_ref, sh_ref, sc_ref, g_ref, w_ref, cd_ref, sd_ref, cs_ref, ss_ref, o_ref, h_scr, *, bm):
    g, sh, sc = g_ref[...], sh_ref[...], sc_ref[...]
    group = 4
    rc = min(bm, ROW_CHUNK)
    for lo in range(0, bm, rc):
        for r in range(lo, lo + rc, NORM_CHUNK):
            h_scr[r:r + NORM_CHUNK, :] = _rms_mod(x_ref[r:r + NORM_CHUNK, :], g, sh, sc).astype(BF)
        rows = slice(lo, lo + rc)
        for grp in range(N_SLABS // group):
            acc = jnp.dot(h_scr[rows, :], w_ref[:, grp * group * SLAB:(grp + 1) * group * SLAB],
                          preferred_element_type=F32)
            for t in range(group):
                slab = grp * group + t
                a = acc[:, t * SLAB:(t + 1) * SLAB]
                if slab < DV0:
                    a = _rope(a, cd_ref[rows, :], sd_ref[rows, :], DIFF_QK_DIM // 4)
                    if slab < DK0:
                        a = a * DIFF_QSCALE
                elif SQ0 <= slab < SV0:
                    a = _rope(a, cs_ref[rows, :], ss_ref[rows, :], SWA_HEAD_DIM // 4)
                    if slab < SK0:
                        a = a * SWA_QSCALE
                o_ref[rows, slab * SLAB:(slab + 1) * SLAB] = a.astype(BF)


def _inproj(x, mods, layer, row_fn, g_pre, w, tables, seq, bm):
    rows, d = x.shape
    nt = seq // bm
    mod = lambda k: _mod_spec(layer, k, row_fn, 1)(d)
    tab = pl.BlockSpec((bm, SLAB), lambda i: (i % nt, 0))
    return pl.pallas_call(
        functools.partial(_inproj_kernel, bm=bm),
        grid=(rows // bm,),
        in_specs=[
            pl.BlockSpec((bm, d), lambda i: (i, 0)),
            mod(0), mod(1),
            pl.BlockSpec((1, d), lambda i: (0, 0)),
            _resident((None, d, IN_W), lambda i: (layer, 0, 0)),
            tab, tab, tab, tab,
        ],
        out_specs=pl.BlockSpec((bm, IN_W), lambda i: (i, 0)),
        out_shape=jax.ShapeDtypeStruct((rows, IN_W), BF),
        scratch_shapes=[pltpu.VMEM((bm, d), BF)],
        compiler_params=_params(("arbitrary",)),
        name="inproj",
    )(x, mods, mods, g_pre, w, *tables)


ONES_ROWS = 16
DIFF_SUB_BLOCKS = 4
DIFF_UNROLL = 64
DIFF_HAZARD = 64.0


def _diff_kernel(*refs, n_lat, unroll, n_sub, lagged):
    n_out = 2 if lagged else 1
    ins, outs = refs[:-n_out], refs[-n_out:]
    if n_lat:
        lam_ref, sub_ref, q_ref, kl_ref, vl_ref, kc_ref, vc_ref = ins
    else:
        lam_ref, sub_ref, q_ref, kc_ref, vc_ref = ins
    o_ref = outs[0]
    lp = lam_ref[...]
    lam_init = lp[4:5, 0:1]
    lam = (jnp.exp(jnp.sum(lp[0:1] * lp[1:2], axis=-1, keepdims=True))
           - jnp.exp(jnp.sum(lp[2:3] * lp[3:4], axis=-1, keepdims=True)) + lam_init)

    bq = q_ref.shape[0] // n_sub
    n_chunks = n_lat + 1
    keys = lambda t: kc_ref[...] if isinstance(t, int) and t == n_lat else kl_ref[t]
    values = lambda t: vc_ref[...] if isinstance(t, int) and t == n_lat else vl_ref[t]

    def run_steps(step, carry, first, n_loop):
        trips = n_loop // unroll

        def body(i, carry):
            for u in range(unroll):
                carry = step(first + i * unroll + u, carry)
            return carry

        if trips:
            carry = lax.fori_loop(0, trips, body, carry)
        for t in range(first + trips * unroll, n_chunks):
            carry = step(t, carry)
        return carry

    def one_block(sb):
        rows = slice(sb * bq, (sb + 1) * bq)
        q = q_ref[rows, :].astype(F32)
        lane = lax.broadcasted_iota(jnp.int32, q.shape, 1)
        q_both = jnp.concatenate([jnp.where(lane < DIFF_QK_DIM, q, 0.0), jnp.where(lane >= DIFF_QK_DIM, q, 0.0)],
                                 axis=0)
        q_t = jnp.transpose(q_both).astype(BF)

        def qk(t):
            return jnp.dot(keys(t), q_t, preferred_element_type=F32)

        def pv(t, p):
            v_t = values(t)
            v_ext = jnp.concatenate([v_t, jnp.ones((ONES_ROWS, v_t.shape[1]), BF)], axis=0)
            return jnp.dot(v_ext, p, preferred_element_type=F32)

        colmax = lambda s: jnp.max(s, axis=0, keepdims=True)

        if lagged:
            s = qk(0)
            r = colmax(s)
            acc = jnp.zeros((DIFF_V_DIM + ONES_ROWS, 2 * bq), F32)

            def absorb(t, s, r_prev, r, acc, excess):
                p = jnp.exp2(s - r).astype(BF)
                cm = colmax(s)
                acc = jnp.exp2(r_prev - r) * acc + pv(t, p)
                return r, jnp.maximum(r, cm), acc, jnp.maximum(excess, cm - r)

            def step(t, carry):
                s, state = carry
                s_new = qk(t)
                return s_new, absorb(t - 1, s, *state)

            carry = (s, (r, r, acc, jnp.zeros_like(r)))
            s, state = run_steps(step, carry, 1, max(n_lat - 1, 0))
            _, _, acc, excess = absorb(n_chunks - 1, s, *state)
            outs[1][sb:sb + 1, :] = excess
        else:
            def scores(t):
                s = qk(t)
                return s, colmax(s)

            def softmax(sc, m):
                s, smax = sc
                mn = jnp.maximum(m, smax)
                return mn, jnp.exp2(m - mn), jnp.exp2(s - mn).astype(BF)

            m = jnp.full((1, 2 * bq), NEG_INF, F32)
            acc = jnp.zeros((DIFF_V_DIM + ONES_ROWS, 2 * bq), F32)
            sc = scores(0)
            if n_chunks > 1:
                sc_next = scores(1)
                m, alpha, p = softmax(sc, m)

                def step(t, carry):
                    sc, p, alpha, m, acc = carry
                    sc_new = scores(t)
                    m_new, alpha_new, p_new = softmax(sc, m)
                    acc = alpha * acc + pv(t - 2, p)
                    return sc_new, p_new, alpha_new, m_new, acc

                sc, p, alpha, m, acc = run_steps(step, (sc_next, p, alpha, m, acc), 2, max(n_lat - 2, 0))
                acc = alpha * acc + pv(n_chunks - 2, p)
            m, alpha, p = softmax(sc, m)
            acc = alpha * acc + pv(n_chunks - 1, p)

        o_t = acc[:DIFF_V_DIM] / acc[DIFF_V_DIM:DIFF_V_DIM + 1]
        o = jnp.transpose(o_t[:, :bq]) - lam * jnp.transpose(o_t[:, bq:])
        ms = jnp.mean(o * o, axis=-1, keepdims=True)
        o_ref[rows, :] = (o * lax.rsqrt(ms + EPS) * sub_ref[...] * (1.0 - lam_init)).astype(BF)

    for sb in range(n_sub):
        one_block(sb)


def _diff_attention(p_q, p_lat, p_ctx, lam_par, subln, bq, bk):
    b, sq, _ = p_q.shape
    c = p_ctx.shape[1]
    vw = DIFF_HEADS * DIFF_V_DIM
    n_lat = 0 if p_lat is None else p_lat.shape[1] // bk
    n_sub = DIFF_SUB_BLOCKS if sq % (DIFF_SUB_BLOCKS * bq) == 0 else 1
    qb = n_sub * bq
    in_specs = [
        pl.BlockSpec(lam_par.shape, lambda bi, h, qi: (0, 0)),
        pl.BlockSpec((1, DIFF_V_DIM), lambda bi, h, qi: (0, 0)),
        pl.BlockSpec((None, qb, SLAB), lambda bi, h, qi: (bi, qi, DQ0 + h)),
    ]
    args = [lam_par, subln, p_q]
    values_of = lambda p: p[:, :, DV0 * SLAB:DV0 * SLAB + vw]
    if n_lat:
        k_lat = p_lat.reshape(b, n_lat, bk, IN_W)
        v_lat = jnp.transpose(values_of(p_lat).reshape(b, n_lat, bk, DIFF_HEADS, DIFF_V_DIM), (0, 3, 1, 4, 2))
        in_specs += [pl.BlockSpec((None, n_lat, bk, SLAB), lambda bi, h, qi: (bi, 0, 0, DK0 + h)),
                     pl.BlockSpec((None, None, n_lat, DIFF_V_DIM, bk), lambda bi, h, qi: (bi, h, 0, 0, 0))]
        args += [k_lat, v_lat]
    v_ctx = jnp.transpose(values_of(p_ctx).reshape(b, c, DIFF_HEADS, DIFF_V_DIM), (0, 2, 3, 1))
    in_specs += [pl.BlockSpec((None, c, SLAB), lambda bi, h, qi: (bi, 0, DK0 + h)),
                 pl.BlockSpec((None, None, DIFF_V_DIM, c), lambda bi, h, qi: (bi, h, 0, 0))]
    args += [p_ctx, v_ctx]
    nq = sq // qb

    def call(lagged):
        out_specs = [pl.BlockSpec((None, qb, SLAB), lambda bi, h, qi: (bi, qi, h))]
        out_shape = [jax.ShapeDtypeStruct((b, sq, vw), BF)]
        if lagged:
            out_specs.append(pl.BlockSpec((None, None, None, n_sub, 2 * bq), lambda bi, h, qi: (bi, h, qi, 0, 0)))
            out_shape.append(jax.ShapeDtypeStruct((b, DIFF_HEADS, nq, n_sub, 2 * bq), F32))
        return pl.pallas_call(
            functools.partial(_diff_kernel, n_lat=n_lat, unroll=DIFF_UNROLL, n_sub=n_sub, lagged=lagged),
            grid=(b, DIFF_HEADS, nq),
            in_specs=in_specs,
            out_specs=out_specs,
            out_shape=out_shape,
            compiler_params=_params(("arbitrary", "arbitrary", "arbitrary")),
            name=("diff_attn_lagged" if lagged else "diff_attn") if n_lat else "diff_attn_ctx",
        )(*args)

    if not n_lat:
        return call(False)[0]
    od, excess = call(True)
    safe = jnp.max(excess) <= DIFF_HAZARD
    return lax.cond(safe, lambda: od, lambda: call(False)[0])


SWA_SUB_BLOCKS = 2


def _swa_kernel(*refs, band, nsb):
    blk = WINDOW
    if band:
        sink_ref, *mask_refs = refs[:1 + nsb]
        q_ref, kp_ref, kz_ref, kn_ref, vp_ref, vz_ref, vn_ref, kc_ref, vc_ref, o_ref = refs[1 + nsb:]
    else:
        sink_ref, q_ref, kc_ref, vc_ref, o_ref = refs
    rows = SWA_GROUP * blk
    rowc = lax.broadcasted_iota(jnp.int32, (rows, 1), 0)
    for j in range(nsb):
        qr = slice(j * blk, (j + 1) * blk)
        for g in range(SWA_KV_HEADS):
            gs = slice(g * SLAB, (g + 1) * SLAB)
            q4 = jnp.concatenate([q_ref[qr, (g * SWA_GROUP + h) * SLAB:(g * SWA_GROUP + h + 1) * SLAB]
                                  for h in range(SWA_GROUP)], axis=0)
            if band:
                kparts = [kp_ref[:, gs]] + [kz_ref[i * blk:(i + 1) * blk, gs] for i in range(nsb)] + [kn_ref[:, gs]]
                vparts = [vp_ref[:, gs]] + [vz_ref[i * blk:(i + 1) * blk, gs] for i in range(nsb)] + [vn_ref[:, gs]]
                kcat = jnp.concatenate(kparts[j:j + 3] + [kc_ref[:, gs]], axis=0)
                vcat = jnp.concatenate(vparts[j:j + 3] + [vc_ref[:, gs]], axis=0)
            else:
                kcat, vcat = kc_ref[:, gs], vc_ref[:, gs]
            s = lax.dot_general(q4, kcat, (((1,), (1,)), ((), ())), preferred_element_type=F32)
            if band:
                s = s + mask_refs[j][...]
            sk = jnp.full((rows, 1), sink_ref[g * SWA_GROUP + SWA_GROUP - 1], F32)
            for h in range(SWA_GROUP - 2, -1, -1):
                sk = jnp.where(rowc < (h + 1) * blk, sink_ref[g * SWA_GROUP + h], sk)
            sk = sk * LOG2E
            m = jnp.maximum(jnp.max(s, axis=-1, keepdims=True), sk)
            p = jnp.exp2(s - m)
            l = jnp.sum(p, axis=-1, keepdims=True) + jnp.exp2(sk - m)
            o = jnp.dot(p.astype(BF), vcat, preferred_element_type=F32) / l
            for h in range(SWA_GROUP):
                hs = (g * SWA_GROUP + h) * SLAB
                o_ref[qr, hs:hs + SLAB] = o[h * blk:(h + 1) * blk, :].astype(BF)


def _swa_attention(p_q, p_ctx, sink, band):
    b, sq, _ = p_q.shape
    c = p_ctx.shape[1]
    blk = WINDOW
    nb = sq // blk
    nsb = SWA_SUB_BLOCKS if nb % SWA_SUB_BLOCKS == 0 else 1
    qw = SWA_HEADS * SLAB
    kvw = SWA_KV_HEADS * SLAB
    in_specs = [
        pl.BlockSpec(memory_space=pltpu.SMEM),
        pl.BlockSpec((None, nsb * blk, qw), lambda bi, n: (bi, n, SQ0 * SLAB // qw)),
    ]
    args = [sink, p_q]
    if band:
        r = np.arange(SWA_GROUP * blk)[:, None] % blk
        col = np.arange(3 * blk + c)[None, :]
        in_band = (np.abs(r - (col - blk)) <= WINDOW) | (col >= 3 * blk)
        variants = [in_band & ~((col < blk) & bool(v & 1)) & ~((col >= 2 * blk) & (col < 3 * blk) & bool(v & 2))
                    for v in range(4)]
        mask = jnp.asarray(np.where(np.stack(variants), 0.0, NEG_INF).astype(np.float32))
        for j in range(nsb):
            in_specs.insert(1 + j, pl.BlockSpec(
                (None,) + mask.shape[1:],
                lambda bi, n, j=j: (jnp.where(n * nsb + j == 0, 1, 0) + jnp.where(n * nsb + j == nb - 1, 2, 0), 0, 0)))
            args.insert(1 + j, mask)
        for base in (SK0 * SLAB // kvw, SV0 * SLAB // kvw):
            in_specs += [
                pl.BlockSpec((None, blk, kvw), lambda bi, n, base=base: (bi, jnp.maximum(n * nsb - 1, 0), base)),
                pl.BlockSpec((None, nsb * blk, kvw), lambda bi, n, base=base: (bi, n, base)),
                pl.BlockSpec((None, blk, kvw), lambda bi, n, base=base: (bi, jnp.minimum((n + 1) * nsb, nb - 1), base)),
            ]
            args += [p_q, p_q, p_q]
    in_specs += [pl.BlockSpec((None, c, kvw), lambda bi, n: (bi, 0, SK0 * SLAB // kvw)),
                 pl.BlockSpec((None, c, kvw), lambda bi, n: (bi, 0, SV0 * SLAB // kvw))]
    args += [p_ctx, p_ctx]
    return pl.pallas_call(
        functools.partial(_swa_kernel, band=band, nsb=nsb),
        grid=(b, nb // nsb),
        in_specs=in_specs,
        out_specs=pl.BlockSpec((None, nsb * blk, qw), lambda bi, n: (bi, n, 0)),
        out_shape=jax.ShapeDtypeStruct((b, sq, qw), BF),
        compiler_params=_params(("arbitrary", "arbitrary")),
        name="swa_attn" if band else "swa_attn_ctx",
    )(*args)


def _matmul_post_norm(operands, x_ref, gt_ref, g_ref, o_ref, f_scr, bm):
    gt, g = gt_ref[...], g_ref[...]
    rc = min(bm, ROW_CHUNK)
    for lo in range(0, bm, rc):
        rows = slice(lo, lo + rc)
        f_scr[rows, :] = sum(jnp.dot(lhs[rows, :], w[...], preferred_element_type=F32) for lhs, w in operands)
        for r in range(lo, lo + rc, NORM_CHUNK):
            sub = slice(r, r + NORM_CHUNK)
            f = f_scr[sub, :]
            ms = jnp.mean(f * f, axis=-1, keepdims=True)
            o_ref[sub, :] = x_ref[sub, :] + gt * (f * lax.rsqrt(ms + EPS) * g)


def _outproj_kernel(od_ref, os_ref, wt_ref, wb_ref, x_ref, gt_ref, g_ref, o_ref, mix_scr, *, bm):
    _matmul_post_norm(((od_ref, wt_ref), (os_ref, wb_ref)), x_ref, gt_ref, g_ref, o_ref, mix_scr, bm)


def _outproj(od, osw, w_out, x, mods, layer, row_fn, g_post, bm):
    rows, d = x.shape
    half = od.shape[1]
    return pl.pallas_call(
        functools.partial(_outproj_kernel, bm=bm),
        grid=(rows // bm,),
        in_specs=[
            pl.BlockSpec((bm, half), lambda i: (i, 0)),
            pl.BlockSpec((bm, half), lambda i: (i, 0)),
            _resident((None, half, d), lambda i: (layer, 0, 0)),
            _resident((None, half, d), lambda i: (layer, 1, 0)),
            pl.BlockSpec((bm, d), lambda i: (i, 0)),
            _mod_spec(layer, 2, row_fn, 1)(d),
            pl.BlockSpec((1, d), lambda i: (0, 0)),
        ],
        out_specs=pl.BlockSpec((bm, d), lambda i: (i, 0)),
        out_shape=jax.ShapeDtypeStruct((rows, d), F32),
        scratch_shapes=[pltpu.VMEM((bm, d), F32)],
        compiler_params=_params(("arbitrary",)),
        name="outproj",
    )(od, osw, w_out, w_out, x, mods, g_post)


HALO = 16
CONV_CHUNK = 512


def _ffn_up_kernel(x_ref, xp_ref, xn_ref, sh_ref, sc_ref, g_ref, wg_ref, wu_ref, cw_ref, cb_ref, o_ref,
                   h_scr, *, bm, seq):
    i = pl.program_id(0)
    j = pl.program_id(1)

    @pl.when(j == 0)
    def _():
        g, sh, sc = g_ref[...], sh_ref[...], sc_ref[...]

        def norm_rows(r, carry):
            rows = pl.multiple_of(r * NORM_CHUNK, NORM_CHUNK)
            h_scr[pl.ds(HALO + rows, NORM_CHUNK), :] = _rms_mod(x_ref[pl.ds(rows, NORM_CHUNK), :], g, sh, sc).astype(BF)
            return carry

        lax.fori_loop(0, bm // NORM_CHUNK, norm_rows, 0)
        prev_ok = ((i * bm) % seq != 0).astype(F32)
        next_ok = (((i + 1) * bm) % seq != 0).astype(F32)
        h_scr[0:HALO, :] = (_rms_mod(xp_ref[...], g, sh, sc) * prev_ok).astype(BF)
        h_scr[HALO + bm:2 * HALO + bm, :] = (_rms_mod(xn_ref[...], g, sh, sc) * next_ok).astype(BF)

    cw0, cw1, cw2, cb = cw_ref[0:1, :], cw_ref[1:2, :], cw_ref[2:3, :], cb_ref[...]
    rc = min(bm, CONV_CHUNK)
    pad = 8
    for r in range(bm // rc):
        lo = r * rc
        g = jnp.dot(h_scr[lo:lo + rc + 2 * HALO, :], wg_ref[...], preferred_element_type=F32)
        u = jnp.dot(h_scr[lo + HALO:lo + HALO + rc, :], wu_ref[...], preferred_element_type=F32)
        win = g[HALO - pad:HALO + rc + pad]
        below = pltpu.roll(win, 1, axis=0)[pad:pad + rc]
        above = pltpu.roll(win, rc + 2 * pad - 1, axis=0)[pad:pad + rc]
        gc = below * cw0 + g[HALO:HALO + rc] * cw1 + above * cw2 + cb
        a = gc * _sigmoid(gc) * u
        o_ref[lo:lo + rc, :] = a.astype(BF)


def _ffn_up(x, mods, layer, row_fn, g_pre, wg, wu, cw, cb, seq, bm, bn):
    rows, d = x.shape
    dff = wg.shape[2]
    hb = bm // HALO
    nh = rows // HALO
    mod = lambda k: _mod_spec(layer, k, row_fn, 2)(d)
    return pl.pallas_call(
        functools.partial(_ffn_up_kernel, bm=bm, seq=seq),
        grid=(rows // bm, dff // bn),
        in_specs=[
            pl.BlockSpec((bm, d), lambda i, j: (i, 0)),
            pl.BlockSpec((HALO, d), lambda i, j: (jnp.maximum(i * hb - 1, 0), 0)),
            pl.BlockSpec((HALO, d), lambda i, j: (jnp.minimum((i + 1) * hb, nh - 1), 0)),
            mod(3), mod(4),
            pl.BlockSpec((1, d), lambda i, j: (0, 0)),
            pl.BlockSpec((None, d, bn), lambda i, j: (layer, 0, j)),
            pl.BlockSpec((None, d, bn), lambda i, j: (layer, 0, j)),
            pl.BlockSpec((3, bn), lambda i, j: (0, j)),
            pl.BlockSpec((1, bn), lambda i, j: (0, j)),
        ],
        out_specs=pl.BlockSpec((bm, bn), lambda i, j: (i, j)),
        out_shape=jax.ShapeDtypeStruct((rows, dff), BF),
        scratch_shapes=[pltpu.VMEM((bm + 2 * HALO, d), BF)],
        compiler_params=_params(("arbitrary", "arbitrary")),
        name="ffn_up",
    )(x, x, x, mods, mods, g_pre, wg, wu, cw, cb)


def _ffn_down_kernel(a_ref, w_ref, x_ref, gt_ref, g_ref, o_ref, f_scr, *, bm):
    _matmul_post_norm(((a_ref, w_ref),), x_ref, gt_ref, g_ref, o_ref, f_scr, bm)


def _ffn_down(a, wd, x, mods, layer, row_fn, g_post, bm):
    rows, d = x.shape
    dff = a.shape[1]
    return pl.pallas_call(
        functools.partial(_ffn_down_kernel, bm=bm),
        grid=(rows // bm,),
        in_specs=[
            pl.BlockSpec((bm, dff), lambda i: (i, 0)),
            _resident((None, dff, d), lambda i: (layer, 0, 0)),
            pl.BlockSpec((bm, d), lambda i: (i, 0)),
            _mod_spec(layer, 5, row_fn, 1)(d),
            pl.BlockSpec((1, d), lambda i: (0, 0)),
        ],
        out_specs=pl.BlockSpec((bm, d), lambda i: (i, 0)),
        out_shape=jax.ShapeDtypeStruct((rows, d), F32),
        scratch_shapes=[pltpu.VMEM((bm, d), F32)],
        compiler_params=_params(("arbitrary",)),
        name="ffn_down",
    )(a, wd, x, mods, g_post)


def _rope_tables(seq, ctx_len):
    t = np.arange(seq)
    row, col = t // GRID_W, t % GRID_W

    def table(half):
        inv = ROPE_THETA ** (-np.arange(half, dtype=np.float64) / half)
        ar, ac = row[:, None] * inv[None, :], col[:, None] * inv[None, :]
        cos = np.concatenate([np.cos(ar), np.cos(ar), np.cos(ac), np.cos(ac)], axis=1)
        sin = np.concatenate([-np.sin(ar), np.sin(ar), -np.sin(ac), np.sin(ac)], axis=1)
        reps = SLAB // cos.shape[1]
        return np.tile(cos, (1, reps)).astype(np.float32), np.tile(sin, (1, reps)).astype(np.float32)

    cd, sd = table(DIFF_QK_DIM // 4)
    cs, ss = table(SWA_HEAD_DIM // 4)
    lat = tuple(jnp.asarray(a) for a in (cd, sd, cs, ss))
    one, zero = jnp.ones((ctx_len, SLAB), F32), jnp.zeros((ctx_len, SLAB), F32)
    return lat, (one, zero, one, zero)


def _pick(n, prefs):
    for p in prefs:
        if n % p == 0:
            return p
    raise ValueError(f"no block size in {prefs} divides {n}")


def kernel(x, c, ctx, c_ctx, w_ada, b_ada, g_pre_mix, g_post_mix, w_in, diff_lambda, diff_subln, swa_sink,
           w_out, g_pre_ffn, g_post_ffn, w_ffn_gate, w_ffn_up, ffn_conv_w, ffn_conv_b, w_ffn_down):
    b, s, d = x.shape
    cl = ctx.shape[1]
    depth = w_ada.shape[0]
    dff = w_ffn_gate.shape[2]
    assert w_in.shape[2] == IN_W and s % WINDOW == 0 and cl % WINDOW == 0

    n_rows = -(-(b + 1) // 8) * 8
    cvec = jnp.concatenate([c, c_ctx[None, :], jnp.zeros((n_rows - b - 1, d), F32)], axis=0)
    mods = _ada(cvec, w_ada, b_ada).reshape(depth, n_rows, 6, 1, d)

    lat_tabs, ctx_tabs = _rope_tables(s, cl)
    bm_in = _pick(s, (512, 256, 128))
    bm_out = _pick(s, (512, 256, 128))
    bm_up = _pick(s, (1024, 512, 256, 128))
    bm_dn = _pick(s, (256, 128))
    bm_c = _pick(cl, (256, 128))
    bn_ff = _pick(dff, (512, 256, 128))
    bq = _pick(s, (256, 128))
    bk = _pick(s, (512, 256, 128))
    bq_c = _pick(cl, (256, 128))

    xl = x.reshape(b * s, d)
    xc = ctx.reshape(b * cl, d)
    ctx_row = lambda i: b

    w_in_l, w_out_l = w_in.astype(BF), w_out.astype(BF)
    wg_l, wu_l, wd_l = w_ffn_gate.astype(BF), w_ffn_up.astype(BF), w_ffn_down.astype(BF)

    for l in range(depth):
        last = l == depth - 1
        lam_init = 0.8 - 0.6 * math.exp(-0.3 * l)
        lam_par = jnp.concatenate([diff_lambda[l], jnp.full_like(diff_lambda[l], lam_init)], axis=0)
        gpm, gqm = g_pre_mix[l][None, :], g_post_mix[l][None, :]
        gpf, gqf = g_pre_ffn[l][None, :], g_post_ffn[l][None, :]
        sub = diff_subln[l][None, :]
        cb = ffn_conv_b[l][None, :]

        p_lat = _inproj(xl, mods, l, lambda i, n=s // bm_in: i // n, gpm, w_in_l, lat_tabs, s, bm_in)
        p_ctx = _inproj(xc, mods, l, ctx_row, gpm, w_in_l, ctx_tabs, cl, bm_c)
        p_lat3, p_ctx3 = p_lat.reshape(b, s, IN_W), p_ctx.reshape(b, cl, IN_W)
        od = _diff_attention(p_lat3, p_lat3, p_ctx3, lam_par, sub, bq, bk)
        osw = _swa_attention(p_lat3, p_ctx3, swa_sink[l], band=True)
        xl = _outproj(od.reshape(b * s, -1), osw.reshape(b * s, -1), w_out_l, xl, mods, l,
                      lambda i, n=s // bm_out: i // n, gqm, bm_out)
        if not last:
            odc = _diff_attention(p_ctx3, None, p_ctx3, lam_par, sub, bq_c, bk)
            oswc = _swa_attention(p_ctx3, p_ctx3, swa_sink[l], band=False)
            xc = _outproj(odc.reshape(b * cl, -1), oswc.reshape(b * cl, -1), w_out_l, xc, mods, l, ctx_row, gqm, bm_c)

        a = _ffn_up(xl, mods, l, lambda i, n=s // bm_up: i // n, gpf, wg_l, wu_l, ffn_conv_w[l], cb, s, bm_up, bn_ff)
        xl = _ffn_down(a, wd_l, xl, mods, l, lambda i, n=s // bm_dn: i // n, gqf, bm_dn)
        if not last:
            ac = _ffn_up(xc, mods, l, ctx_row, gpf, wg_l, wu_l, ffn_conv_w[l], cb, cl, bm_c, bn_ff)
            xc = _ffn_down(ac, wd_l, xc, mods, l, ctx_row, gqf, bm_c)
    return xl.reshape(b, s, d)
```

```python
import functools
import math

import jax
import jax.numpy as jnp
import numpy as np
from jax import lax
from jax.experimental import pallas as pl
from jax.experimental.pallas import tpu as pltpu

BF = jnp.bfloat16
F32 = jnp.float32

GRID_W = 64
DIFF_HEADS = 8
DIFF_QK_DIM = 64
DIFF_V_DIM = 128
SWA_HEADS = 8
SWA_KV_HEADS = 2
SWA_GROUP = SWA_HEADS // SWA_KV_HEADS
SWA_HEAD_DIM = 128
WINDOW = 128
ROPE_THETA = 10000.0
EPS = 1e-6
NEG_INF = -1e30
LOG2E = 1.4426950408889634
DIFF_QSCALE = DIFF_QK_DIM ** -0.5 * LOG2E
SWA_QSCALE = SWA_HEAD_DIM ** -0.5 * LOG2E

SLAB = 128
DQ0, DK0, DV0, SQ0, SK0, SV0 = 0, 8, 16, 24, 32, 34
N_SLABS = 36
IN_W = N_SLABS * SLAB

VMEM_LIMIT = 56 * 1024 * 1024
NORM_CHUNK = 64
ROW_CHUNK = 256


def _params(sem):
    return pltpu.CompilerParams(dimension_semantics=sem, vmem_limit_bytes=VMEM_LIMIT)


def _resident(shape, index_map):
    return pl.BlockSpec(shape, index_map, pipeline_mode=pl.Buffered(1))


def _sigmoid(v):
    return 1.0 / (1.0 + jnp.exp(-v))


def _rms_mod(xf, g, sh, sc):
    ms = jnp.mean(xf * xf, axis=-1, keepdims=True)
    return (xf * lax.rsqrt(ms + EPS) * g) * (1.0 + sc) + sh


def _rope(x, cos, sin, shift):
    lane = lax.broadcasted_iota(jnp.int32, x.shape, 1)
    fwd = pltpu.roll(x, SLAB - shift, axis=1)
    bwd = pltpu.roll(x, shift, axis=1)
    partner = jnp.where((lane & (2 * shift - 1)) < shift, fwd, bwd)
    return x * cos + partner * sin


def _ada_kernel(c_ref, w_ref, b_ref, o_ref):
    cv = c_ref[...]
    s = (cv * _sigmoid(cv)).astype(BF)
    o_ref[...] = jnp.dot(s, w_ref[...].astype(BF), preferred_element_type=F32) + b_ref[...]


def _ada(cvec, w_ada, b_ada):
    depth, d, n = w_ada.shape
    rows = cvec.shape[0]
    bn = _pick(n, (1024, 512, 256, 128))
    return pl.pallas_call(
        _ada_kernel,
        grid=(depth, n // bn),
        in_specs=[
            pl.BlockSpec((rows, d), lambda l, j: (0, 0)),
            pl.BlockSpec((None, d, bn), lambda l, j: (l, 0, j)),
            pl.BlockSpec((None, 1, bn), lambda l, j: (l, 0, j)),
        ],
        out_specs=pl.BlockSpec((None, rows, bn), lambda l, j: (l, 0, j)),
        out_shape=jax.ShapeDtypeStruct((depth, rows, n), F32),
        compiler_params=_params(("arbitrary", "arbitrary")),
        name="ada",
    )(cvec, w_ada, b_ada.reshape(depth, 1, n))


def _mod_spec(layer, k, row_fn, nargs):
    if nargs == 1:
        return lambda d: pl.BlockSpec((None, None, None, 1, d), lambda i: (layer, row_fn(i), k, 0, 0))
    return lambda d: pl.BlockSpec((None, None, None, 1, d), lambda i, j: (layer, row_fn(i), k, 0, 0))


def _inproj_kernel(x_ref, sh_ref, sc_ref, g_ref, w_ref, cd_ref, sd_ref, cs_ref, ss_ref, o_ref, vt_ref, h_scr, *, bm):
    g, sh, sc = g_ref[...], sh_ref[...], sc_ref[...]
    group = 4
    rc = min(bm, ROW_CHUNK)
    for lo in range(0, bm, rc):
        for r in range(lo, lo + rc, NORM_CHUNK):
            h_scr[r:r + NORM_CHUNK, :] = _rms_mod(x_ref[r:r + NORM_CHUNK, :], g, sh, sc).astype(BF)
        rows = slice(lo, lo + rc)
        for grp in range(N_SLABS // group):
            acc = jnp.dot(h_scr[rows, :], w_ref[:, grp * group * SLAB:(grp + 1) * group * SLAB],
                          preferred_element_type=F32)
            for t in range(group):
                slab = grp * group + t
                a = acc[:, t * SLAB:(t + 1) * SLAB]
                if slab < DV0:
                    a = _rope(a, cd_ref[rows, :], sd_ref[rows, :], DIFF_QK_DIM // 4)
                    if slab < DK0:
                        a = a * DIFF_QSCALE
                elif SQ0 <= slab < SV0:
                    a = _rope(a, cs_ref[rows, :], ss_ref[rows, :], SWA_HEAD_DIM // 4)
                    if slab < SK0:
                        a = a * SWA_QSCALE
                elif DV0 <= slab < SQ0:
                    vt_ref[slab - DV0, 0, :, lo:lo + rc] = jnp.transpose(a).astype(BF)
                o_ref[rows, slab * SLAB:(slab + 1) * SLAB] = a.astype(BF)


def _inproj(x, mods, layer, row_fn, g_pre, w, tables, seq, bm):
    rows, d = x.shape
    nt = seq // bm
    mod = lambda k: _mod_spec(layer, k, row_fn, 1)(d)
    tab = pl.BlockSpec((bm, SLAB), lambda i: (i % nt, 0))
    return pl.pallas_call(
        functools.partial(_inproj_kernel, bm=bm),
        grid=(rows // bm,),
        in_specs=[
            pl.BlockSpec((bm, d), lambda i: (i, 0)),
            mod(0), mod(1),
            pl.BlockSpec((1, d), lambda i: (0, 0)),
            _resident((None, d, IN_W), lambda i: (layer, 0, 0)),
            tab, tab, tab, tab,
        ],
        out_specs=[pl.BlockSpec((bm, IN_W), lambda i: (i, 0)),
                   pl.BlockSpec((DIFF_HEADS, 1, DIFF_V_DIM, bm), lambda i: (0, i, 0, 0))],
        out_shape=[jax.ShapeDtypeStruct((rows, IN_W), BF),
                   jax.ShapeDtypeStruct((DIFF_HEADS, rows // bm, DIFF_V_DIM, bm), BF)],
        scratch_shapes=[pltpu.VMEM((bm, d), BF)],
        compiler_params=_params(("arbitrary",)),
        name="inproj",
    )(x, mods, mods, g_pre, w, *tables)


ONES_ROWS = 16
DIFF_SUB_BLOCKS = 4
DIFF_UNROLL = 64
DIFF_HAZARD = 64.0


def _diff_kernel(*refs, n_lat, unroll, n_sub, lagged):
    n_out = 2 if lagged else 1
    ins, outs = refs[:-n_out], refs[-n_out:]
    if n_lat:
        lam_ref, sub_ref, q_ref, kl_ref, vl_ref, kc_ref, vc_ref = ins
    else:
        lam_ref, sub_ref, q_ref, kc_ref, vc_ref = ins
    o_ref = outs[0]
    lp = lam_ref[...]
    lam_init = lp[4:5, 0:1]
    lam = (jnp.exp(jnp.sum(lp[0:1] * lp[1:2], axis=-1, keepdims=True))
           - jnp.exp(jnp.sum(lp[2:3] * lp[3:4], axis=-1, keepdims=True)) + lam_init)

    bq = q_ref.shape[0] // n_sub
    n_chunks = n_lat + 1
    keys = lambda t: kc_ref[...] if isinstance(t, int) and t == n_lat else kl_ref[t]
    values = lambda t: vc_ref[...] if isinstance(t, int) and t == n_lat else vl_ref[t]

    def run_steps(step, carry, first, n_loop):
        trips = n_loop // unroll

        def body(i, carry):
            for u in range(unroll):
                carry = step(first + i * unroll + u, carry)
            return carry

        if trips:
            carry = lax.fori_loop(0, trips, body, carry)
        for t in range(first + trips * unroll, n_chunks):
            carry = step(t, carry)
        return carry

    def one_block(sb):
        rows = slice(sb * bq, (sb + 1) * bq)
        q = q_ref[rows, :].astype(F32)
        lane = lax.broadcasted_iota(jnp.int32, q.shape, 1)
        q_both = jnp.concatenate([jnp.where(lane < DIFF_QK_DIM, q, 0.0), jnp.where(lane >= DIFF_QK_DIM, q, 0.0)],
                                 axis=0)
        q_t = jnp.transpose(q_both).astype(BF)

        def qk(t):
            return jnp.dot(keys(t), q_t, preferred_element_type=F32)

        def pv(t, p):
            v_t = values(t)
            v_ext = jnp.concatenate([v_t, jnp.ones((ONES_ROWS, v_t.shape[1]), BF)], axis=0)
            return jnp.dot(v_ext, p, preferred_element_type=F32)

        colmax = lambda s: jnp.max(s, axis=0, keepdims=True)

        if lagged:
            s = qk(0)
            r = colmax(s)
            acc = jnp.zeros((DIFF_V_DIM + ONES_ROWS, 2 * bq), F32)

            def absorb(t, s, r_prev, r, acc, excess):
                p = jnp.exp2(s - r).astype(BF)
                cm = colmax(s)
                acc = jnp.exp2(r_prev - r) * acc + pv(t, p)
                return r, jnp.maximum(r, cm), acc, jnp.maximum(excess, cm - r)

            def step(t, carry):
                s, state = carry
                s_new = qk(t)
                return s_new, absorb(t - 1, s, *state)

            carry = (s, (r, r, acc, jnp.zeros_like(r)))
            s, state = run_steps(step, carry, 1, max(n_lat - 1, 0))
            _, _, acc, excess = absorb(n_chunks - 1, s, *state)
            outs[1][sb:sb + 1, :] = excess
        else:
            def scores(t):
                s = qk(t)
                return s, colmax(s)

            def softmax(sc, m):
                s, smax = sc
                mn = jnp.maximum(m, smax)
                return mn, jnp.exp2(m - mn), jnp.exp2(s - mn).astype(BF)

            m = jnp.full((1, 2 * bq), NEG_INF, F32)
            acc = jnp.zeros((DIFF_V_DIM + ONES_ROWS, 2 * bq), F32)
            sc = scores(0)
            if n_chunks > 1:
                sc_next = scores(1)
                m, alpha, p = softmax(sc, m)

                def step(t, carry):
                    sc, p, alpha, m, acc = carry
                    sc_new = scores(t)
                    m_new, alpha_new, p_new = softmax(sc, m)
                    acc = alpha * acc + pv(t - 2, p)
                    return sc_new, p_new, alpha_new, m_new, acc

                sc, p, alpha, m, acc = run_steps(step, (sc_next, p, alpha, m, acc), 2, max(n_lat - 2, 0))
                acc = alpha * acc + pv(n_chunks - 2, p)
            m, alpha, p = softmax(sc, m)
            acc = alpha * acc + pv(n_chunks - 1, p)

        o_t = acc[:DIFF_V_DIM] / acc[DIFF_V_DIM:DIFF_V_DIM + 1]
        o = jnp.transpose(o_t[:, :bq]) - lam * jnp.transpose(o_t[:, bq:])
        ms = jnp.mean(o * o, axis=-1, keepdims=True)
        o_ref[rows, :] = (o * lax.rsqrt(ms + EPS) * sub_ref[...] * (1.0 - lam_init)).astype(BF)

    for sb in range(n_sub):
        one_block(sb)


def _diff_attention(p_q, p_lat, p_ctx, vt_lat, vt_ctx, lam_par, subln, bq, bk):
    b, sq, _ = p_q.shape
    c = p_ctx.shape[1]
    vw = DIFF_HEADS * DIFF_V_DIM
    n_lat = 0 if p_lat is None else p_lat.shape[1] // bk
    n_sub = DIFF_SUB_BLOCKS if sq % (DIFF_SUB_BLOCKS * bq) == 0 else 1
    qb = n_sub * bq
    in_specs = [
        pl.BlockSpec(lam_par.shape, lambda bi, h, qi: (0, 0)),
        pl.BlockSpec((1, DIFF_V_DIM), lambda bi, h, qi: (0, 0)),
        pl.BlockSpec((None, qb, SLAB), lambda bi, h, qi: (bi, qi, DQ0 + h)),
    ]
    args = [lam_par, subln, p_q]
    if n_lat:
        k_lat = p_lat.reshape(b, n_lat, bk, IN_W)
        in_specs += [pl.BlockSpec((None, n_lat, bk, SLAB), lambda bi, h, qi: (bi, 0, 0, DK0 + h)),
                     pl.BlockSpec((None, n_lat, DIFF_V_DIM, bk), lambda bi, h, qi: (h, bi, 0, 0))]
        args += [k_lat, vt_lat]
    in_specs += [pl.BlockSpec((None, c, SLAB), lambda bi, h, qi: (bi, 0, DK0 + h)),
                 pl.BlockSpec((None, None, DIFF_V_DIM, c), lambda bi, h, qi: (h, bi, 0, 0))]
    args += [p_ctx, vt_ctx]
    nq = sq // qb

    def call(lagged):
        out_specs = [pl.BlockSpec((None, qb, SLAB), lambda bi, h, qi: (bi, qi, h))]
        out_shape = [jax.ShapeDtypeStruct((b, sq, vw), BF)]
        if lagged:
            out_specs.append(pl.BlockSpec((None, None, None, n_sub, 2 * bq), lambda bi, h, qi: (bi, h, qi, 0, 0)))
            out_shape.append(jax.ShapeDtypeStruct((b, DIFF_HEADS, nq, n_sub, 2 * bq), F32))
        return pl.pallas_call(
            functools.partial(_diff_kernel, n_lat=n_lat, unroll=DIFF_UNROLL, n_sub=n_sub, lagged=lagged),
            grid=(b, DIFF_HEADS, nq),
            in_specs=in_specs,
            out_specs=out_specs,
            out_shape=out_shape,
            compiler_params=_params(("arbitrary", "arbitrary", "arbitrary")),
            name=("diff_attn_lagged" if lagged else "diff_attn") if n_lat else "diff_attn_ctx",
        )(*args)

    if not n_lat:
        return call(False)[0]
    od, excess = call(True)
    safe = jnp.max(excess) <= DIFF_HAZARD
    return lax.cond(safe, lambda: od, lambda: call(False)[0])


SWA_SUB_BLOCKS = 2


def _swa_kernel(*refs, band, nsb):
    blk = WINDOW
    if band:
        sink_ref, *mask_refs = refs[:1 + nsb]
        q_ref, kp_ref, kz_ref, kn_ref, vp_ref, vz_ref, vn_ref, kc_ref, vc_ref, o_ref = refs[1 + nsb:]
    else:
        sink_ref, q_ref, kc_ref, vc_ref, o_ref = refs
    rows = SWA_GROUP * blk
    rowc = lax.broadcasted_iota(jnp.int32, (rows, 1), 0)
    for j in range(nsb):
        qr = slice(j * blk, (j + 1) * blk)
        for g in range(SWA_KV_HEADS):
            gs = slice(g * SLAB, (g + 1) * SLAB)
            q4 = jnp.concatenate([q_ref[qr, (g * SWA_GROUP + h) * SLAB:(g * SWA_GROUP + h + 1) * SLAB]
                                  for h in range(SWA_GROUP)], axis=0)
            if band:
                kparts = [kp_ref[:, gs]] + [kz_ref[i * blk:(i + 1) * blk, gs] for i in range(nsb)] + [kn_ref[:, gs]]
                vparts = [vp_ref[:, gs]] + [vz_ref[i * blk:(i + 1) * blk, gs] for i in range(nsb)] + [vn_ref[:, gs]]
                kcat = jnp.concatenate(kparts[j:j + 3] + [kc_ref[:, gs]], axis=0)
                vcat = jnp.concatenate(vparts[j:j + 3] + [vc_ref[:, gs]], axis=0)
            else:
                kcat, vcat = kc_ref[:, gs], vc_ref[:, gs]
            s = lax.dot_general(q4, kcat, (((1,), (1,)), ((), ())), preferred_element_type=F32)
            if band:
                s = s + mask_refs[j][...]
            sk = jnp.full((rows, 1), sink_ref[g * SWA_GROUP + SWA_GROUP - 1], F32)
            for h in range(SWA_GROUP - 2, -1, -1):
                sk = jnp.where(rowc < (h + 1) * blk, sink_ref[g * SWA_GROUP + h], sk)
            sk = sk * LOG2E
            m = jnp.maximum(jnp.max(s, axis=-1, keepdims=True), sk)
            p = jnp.exp2(s - m)
            l = jnp.sum(p, axis=-1, keepdims=True) + jnp.exp2(sk - m)
            o = jnp.dot(p.astype(BF), vcat, preferred_element_type=F32) / l
            for h in range(SWA_GROUP):
                hs = (g * SWA_GROUP + h) * SLAB
                o_ref[qr, hs:hs + SLAB] = o[h * blk:(h + 1) * blk, :].astype(BF)


def _swa_attention(p_q, p_ctx, sink, band):
    b, sq, _ = p_q.shape
    c = p_ctx.shape[1]
    blk = WINDOW
    nb = sq // blk
    nsb = SWA_SUB_BLOCKS if nb % SWA_SUB_BLOCKS == 0 else 1
    qw = SWA_HEADS * SLAB
    kvw = SWA_KV_HEADS * SLAB
    in_specs = [
        pl.BlockSpec(memory_space=pltpu.SMEM),
        pl.BlockSpec((None, nsb * blk, qw), lambda bi, n: (bi, n, SQ0 * SLAB // qw)),
    ]
    args = [sink, p_q]
    if band:
        r = np.arange(SWA_GROUP * blk)[:, None] % blk
        col = np.arange(3 * blk + c)[None, :]
        in_band = (np.abs(r - (col - blk)) <= WINDOW) | (col >= 3 * blk)
        variants = [in_band & ~((col < blk) & bool(v & 1)) & ~((col >= 2 * blk) & (col < 3 * blk) & bool(v & 2))
                    for v in range(4)]
        mask = jnp.asarray(np.where(np.stack(variants), 0.0, NEG_INF).astype(np.float32))
        for j in range(nsb):
            in_specs.insert(1 + j, pl.BlockSpec(
                (None,) + mask.shape[1:],
                lambda bi, n, j=j: (jnp.where(n * nsb + j == 0, 1, 0) + jnp.where(n * nsb + j == nb - 1, 2, 0), 0, 0)))
            args.insert(1 + j, mask)
        for base in (SK0 * SLAB // kvw, SV0 * SLAB // kvw):
            in_specs += [
                pl.BlockSpec((None, blk, kvw), lambda bi, n, base=base: (bi, jnp.maximum(n * nsb - 1, 0), base)),
                pl.BlockSpec((None, nsb * blk, kvw), lambda bi, n, base=base: (bi, n, base)),
                pl.BlockSpec((None, blk, kvw), lambda bi, n, base=base: (bi, jnp.minimum((n + 1) * nsb, nb - 1), base)),
            ]
            args += [p_q, p_q, p_q]
    in_specs += [pl.BlockSpec((None, c, kvw), lambda bi, n: (bi, 0, SK0 * SLAB // kvw)),
                 pl.BlockSpec((None, c, kvw), lambda bi, n: (bi, 0, SV0 * SLAB // kvw))]
    args += [p_ctx, p_ctx]
    return pl.pallas_call(
        functools.partial(_swa_kernel, band=band, nsb=nsb),
        grid=(b, nb // nsb),
        in_specs=in_specs,
        out_specs=pl.BlockSpec((None, nsb * blk, qw), lambda bi, n: (bi, n, 0)),
        out_shape=jax.ShapeDtypeStruct((b, sq, qw), BF),
        compiler_params=_params(("arbitrary", "arbitrary")),
        name="swa_attn" if band else "swa_attn_ctx",
    )(*args)


def _matmul_post_norm(operands, x_ref, gt_ref, g_ref, o_ref, f_scr, bm):
    gt, g = gt_ref[...], g_ref[...]
    rc = min(bm, ROW_CHUNK)
    for lo in range(0, bm, rc):
        rows = slice(lo, lo + rc)
        f_scr[rows, :] = sum(jnp.dot(lhs[rows, :], w[...], preferred_element_type=F32) for lhs, w in operands)
        for r in range(lo, lo + rc, NORM_CHUNK):
            sub = slice(r, r + NORM_CHUNK)
            f = f_scr[sub, :]
            ms = jnp.mean(f * f, axis=-1, keepdims=True)
            o_ref[sub, :] = x_ref[sub, :] + gt * (f * lax.rsqrt(ms + EPS) * g)


def _outproj_kernel(od_ref, os_ref, wt_ref, wb_ref, x_ref, gt_ref, g_ref, o_ref, mix_scr, *, bm):
    _matmul_post_norm(((od_ref, wt_ref), (os_ref, wb_ref)), x_ref, gt_ref, g_ref, o_ref, mix_scr, bm)


def _outproj(od, osw, w_out, x, mods, layer, row_fn, g_post, bm):
    rows, d = x.shape
    half = od.shape[1]
    return pl.pallas_call(
        functools.partial(_outproj_kernel, bm=bm),
        grid=(rows // bm,),
        in_specs=[
            pl.BlockSpec((bm, half), lambda i: (i, 0)),
            pl.BlockSpec((bm, half), lambda i: (i, 0)),
            _resident((None, half, d), lambda i: (layer, 0, 0)),
            _resident((None, half, d), lambda i: (layer, 1, 0)),
            pl.BlockSpec((bm, d), lambda i: (i, 0)),
            _mod_spec(layer, 2, row_fn, 1)(d),
            pl.BlockSpec((1, d), lambda i: (0, 0)),
        ],
        out_specs=pl.BlockSpec((bm, d), lambda i: (i, 0)),
        out_shape=jax.ShapeDtypeStruct((rows, d), F32),
        scratch_shapes=[pltpu.VMEM((bm, d), F32)],
        compiler_params=_params(("arbitrary",)),
        name="outproj",
    )(od, osw, w_out, w_out, x, mods, g_post)


HALO = 16
CONV_CHUNK = 512


def _ffn_up_kernel(x_ref, xp_ref, xn_ref, sh_ref, sc_ref, g_ref, wg_ref, wu_ref, cw_ref, cb_ref, o_ref,
                   h_scr, *, bm, seq):
    i = pl.program_id(0)
    j = pl.program_id(1)

    @pl.when(j == 0)
    def _():
        g, sh, sc = g_ref[...], sh_ref[...], sc_ref[...]

        def norm_rows(r, carry):
            rows = pl.multiple_of(r * NORM_CHUNK, NORM_CHUNK)
            h_scr[pl.ds(HALO + rows, NORM_CHUNK), :] = _rms_mod(x_ref[pl.ds(rows, NORM_CHUNK), :], g, sh, sc).astype(BF)
            return carry

        lax.fori_loop(0, bm // NORM_CHUNK, norm_rows, 0)
        prev_ok = ((i * bm) % seq != 0).astype(F32)
        next_ok = (((i + 1) * bm) % seq != 0).astype(F32)
        h_scr[0:HALO, :] = (_rms_mod(xp_ref[...], g, sh, sc) * prev_ok).astype(BF)
        h_scr[HALO + bm:2 * HALO + bm, :] = (_rms_mod(xn_ref[...], g, sh, sc) * next_ok).astype(BF)

    cw0, cw1, cw2, cb = cw_ref[0:1, :], cw_ref[1:2, :], cw_ref[2:3, :], cb_ref[...]
    rc = min(bm, CONV_CHUNK)
    pad = 8
    for r in range(bm // rc):
        lo = r * rc
        g = jnp.dot(h_scr[lo:lo + rc + 2 * HALO, :], wg_ref[...], preferred_element_type=F32)
        u = jnp.dot(h_scr[lo + HALO:lo + HALO + rc, :], wu_ref[...], preferred_element_type=F32)
        win = g[HALO - pad:HALO + rc + pad]
        below = pltpu.roll(win, 1, axis=0)[pad:pad + rc]
        above = pltpu.roll(win, rc + 2 * pad - 1, axis=0)[pad:pad + rc]
        gc = below * cw0 + g[HALO:HALO + rc] * cw1 + above * cw2 + cb
        a = gc * _sigmoid(gc) * u
        o_ref[lo:lo + rc, :] = a.astype(BF)


def _ffn_up(x, mods, layer, row_fn, g_pre, wg, wu, cw, cb, seq, bm, bn):
    rows, d = x.shape
    dff = wg.shape[2]
    hb = bm // HALO
    nh = rows // HALO
    mod = lambda k: _mod_spec(layer, k, row_fn, 2)(d)
    return pl.pallas_call(
        functools.partial(_ffn_up_kernel, bm=bm, seq=seq),
        grid=(rows // bm, dff // bn),
        in_specs=[
            pl.BlockSpec((bm, d), lambda i, j: (i, 0)),
            pl.BlockSpec((HALO, d), lambda i, j: (jnp.maximum(i * hb - 1, 0), 0)),
            pl.BlockSpec((HALO, d), lambda i, j: (jnp.minimum((i + 1) * hb, nh - 1), 0)),
            mod(3), mod(4),
            pl.BlockSpec((1, d), lambda i, j: (0, 0)),
            pl.BlockSpec((None, d, bn), lambda i, j: (layer, 0, j)),
            pl.BlockSpec((None, d, bn), lambda i, j: (layer, 0, j)),
            pl.BlockSpec((3, bn), lambda i, j: (0, j)),
            pl.BlockSpec((1, bn), lambda i, j: (0, j)),
        ],
        out_specs=pl.BlockSpec((bm, bn), lambda i, j: (i, j)),
        out_shape=jax.ShapeDtypeStruct((rows, dff), BF),
        scratch_shapes=[pltpu.VMEM((bm + 2 * HALO, d), BF)],
        compiler_params=_params(("arbitrary", "arbitrary")),
        name="ffn_up",
    )(x, x, x, mods, mods, g_pre, wg, wu, cw, cb)


def _ffn_down_kernel(a_ref, w_ref, x_ref, gt_ref, g_ref, o_ref, f_scr, *, bm):
    _matmul_post_norm(((a_ref, w_ref),), x_ref, gt_ref, g_ref, o_ref, f_scr, bm)


def _ffn_down(a, wd, x, mods, layer, row_fn, g_post, bm):
    rows, d = x.shape
    dff = a.shape[1]
    return pl.pallas_call(
        functools.partial(_ffn_down_kernel, bm=bm),
        grid=(rows // bm,),
        in_specs=[
            pl.BlockSpec((bm, dff), lambda i: (i, 0)),
            _resident((None, dff, d), lambda i: (layer, 0, 0)),
            pl.BlockSpec((bm, d), lambda i: (i, 0)),
            _mod_spec(layer, 5, row_fn, 1)(d),
            pl.BlockSpec((1, d), lambda i: (0, 0)),
        ],
        out_specs=pl.BlockSpec((bm, d), lambda i: (i, 0)),
        out_shape=jax.ShapeDtypeStruct((rows, d), F32),
        scratch_shapes=[pltpu.VMEM((bm, d), F32)],
        compiler_params=_params(("arbitrary",)),
        name="ffn_down",
    )(a, wd, x, mods, g_post)


def _rope_tables(seq, ctx_len):
    t = np.arange(seq)
    row, col = t // GRID_W, t % GRID_W

    def table(half):
        inv = ROPE_THETA ** (-np.arange(half, dtype=np.float64) / half)
        ar, ac = row[:, None] * inv[None, :], col[:, None] * inv[None, :]
        cos = np.concatenate([np.cos(ar), np.cos(ar), np.cos(ac), np.cos(ac)], axis=1)
        sin = np.concatenate([-np.sin(ar), np.sin(ar), -np.sin(ac), np.sin(ac)], axis=1)
        reps = SLAB // cos.shape[1]
        return np.tile(cos, (1, reps)).astype(np.float32), np.tile(sin, (1, reps)).astype(np.float32)

    cd, sd = table(DIFF_QK_DIM // 4)
    cs, ss = table(SWA_HEAD_DIM // 4)
    lat = tuple(jnp.asarray(a) for a in (cd, sd, cs, ss))
    one, zero = jnp.ones((ctx_len, SLAB), F32), jnp.zeros((ctx_len, SLAB), F32)
    return lat, (one, zero, one, zero)


def _pick(n, prefs):
    for p in prefs:
        if n % p == 0:
            return p
    raise ValueError(f"no block size in {prefs} divides {n}")


def kernel(x, c, ctx, c_ctx, w_ada, b_ada, g_pre_mix, g_post_mix, w_in, diff_lambda, diff_subln, swa_sink,
           w_out, g_pre_ffn, g_post_ffn, w_ffn_gate, w_ffn_up, ffn_conv_w, ffn_conv_b, w_ffn_down):
    b, s, d = x.shape
    cl = ctx.shape[1]
    depth = w_ada.shape[0]
    dff = w_ffn_gate.shape[2]
    assert w_in.shape[2] == IN_W and s % WINDOW == 0 and cl % WINDOW == 0

    n_rows = -(-(b + 1) // 8) * 8
    cvec = jnp.concatenate([c, c_ctx[None, :], jnp.zeros((n_rows - b - 1, d), F32)], axis=0)
    mods = _ada(cvec, w_ada, b_ada).reshape(depth, n_rows, 6, 1, d)

    lat_tabs, ctx_tabs = _rope_tables(s, cl)
    bm_in = _pick(s, (512, 256, 128))
    bm_out = _pick(s, (512, 256, 128))
    bm_up = _pick(s, (1024, 512, 256, 128))
    bm_dn = _pick(s, (256, 128))
    bm_c = _pick(cl, (256, 128))
    bn_ff = _pick(dff, (512, 256, 128))
    bq = _pick(s, (256, 128))
    bk = bm_in
    assert bm_c == cl
    bq_c = _pick(cl, (256, 128))

    xl = x.reshape(b * s, d)
    xc = ctx.reshape(b * cl, d)
    ctx_row = lambda i: b

    w_in_l, w_out_l = w_in.astype(BF), w_out.astype(BF)
    wg_l, wu_l, wd_l = w_ffn_gate.astype(BF), w_ffn_up.astype(BF), w_ffn_down.astype(BF)

    for l in range(depth):
        last = l == depth - 1
        lam_init = 0.8 - 0.6 * math.exp(-0.3 * l)
        lam_par = jnp.concatenate([diff_lambda[l], jnp.full_like(diff_lambda[l], lam_init)], axis=0)
        gpm, gqm = g_pre_mix[l][None, :], g_post_mix[l][None, :]
        gpf, gqf = g_pre_ffn[l][None, :], g_post_ffn[l][None, :]
        sub = diff_subln[l][None, :]
        cb = ffn_conv_b[l][None, :]

        p_lat, vt_lat = _inproj(xl, mods, l, lambda i, n=s // bm_in: i // n, gpm, w_in_l, lat_tabs, s, bm_in)
        p_ctx, vt_ctx = _inproj(xc, mods, l, ctx_row, gpm, w_in_l, ctx_tabs, cl, bm_c)
        p_lat3, p_ctx3 = p_lat.reshape(b, s, IN_W), p_ctx.reshape(b, cl, IN_W)
        od = _diff_attention(p_lat3, p_lat3, p_ctx3, vt_lat, vt_ctx, lam_par, sub, bq, bk)
        osw = _swa_attention(p_lat3, p_ctx3, swa_sink[l], band=True)
        xl = _outproj(od.reshape(b * s, -1), osw.reshape(b * s, -1), w_out_l, xl, mods, l,
                      lambda i, n=s // bm_out: i // n, gqm, bm_out)
        if not last:
            odc = _diff_attention(p_ctx3, None, p_ctx3, None, vt_ctx, lam_par, sub, bq_c, bk)
            oswc = _swa_attention(p_ctx3, p_ctx3, swa_sink[l], band=False)
            xc = _outproj(odc.reshape(b * cl, -1), oswc.reshape(b * cl, -1), w_out_l, xc, mods, l, ctx_row, gqm, bm_c)

        a = _ffn_up(xl, mods, l, lambda i, n=s // bm_up: i // n, gpf, wg_l, wu_l, ffn_conv_w[l], cb, s, bm_up, bn_ff)
        xl = _ffn_down(a, wd_l, xl, mods, l, lambda i, n=s // bm_dn: i // n, gqf, bm_dn)
        if not last:
            ac = _ffn_up(xc, mods, l, ctx_row, gpf, wg_l, wu_l, ffn_conv_w[l], cb, cl, bm_c, bn_ff)
            xc = _ffn_down(ac, wd_l, xc, mods, l, ctx_row, gqf, bm_c)
    return xl.reshape(b, s, d)
```

```python
import functools
import math

import jax
import jax.numpy as jnp
import numpy as np
from jax import lax
from jax.experimental import pallas as pl
from jax.experimental.pallas import tpu as pltpu

BF = jnp.bfloat16
F32 = jnp.float32

GRID_W = 64
DIFF_HEADS = 8
DIFF_QK_DIM = 64
DIFF_V_DIM = 128
SWA_HEADS = 8
SWA_KV_HEADS = 2
SWA_GROUP = SWA_HEADS // SWA_KV_HEADS
SWA_HEAD_DIM = 128
WINDOW = 128
ROPE_THETA = 10000.0
EPS = 1e-6
NEG_INF = -1e30
LOG2E = 1.4426950408889634
DIFF_QSCALE = DIFF_QK_DIM ** -0.5 * LOG2E
SWA_QSCALE = SWA_HEAD_DIM ** -0.5 * LOG2E

SLAB = 128
DQ0, DK0, DV0, SQ0, SK0, SV0 = 0, 8, 16, 24, 32, 34
N_SLABS = 36
IN_W = N_SLABS * SLAB

VMEM_LIMIT = 56 * 1024 * 1024
NORM_CHUNK = 64
ROW_CHUNK = 256


def _params(sem):
    return pltpu.CompilerParams(dimension_semantics=sem, vmem_limit_bytes=VMEM_LIMIT)


def _resident(shape, index_map):
    return pl.BlockSpec(shape, index_map, pipeline_mode=pl.Buffered(1))


def _sigmoid(v):
    return 1.0 / (1.0 + jnp.exp(-v))


def _rms_mod(xf, g, sh, sc):
    ms = jnp.mean(xf * xf, axis=-1, keepdims=True)
    return (xf * lax.rsqrt(ms + EPS) * g) * (1.0 + sc) + sh


def _rope(x, cos, sin, shift):
    lane = lax.broadcasted_iota(jnp.int32, x.shape, 1)
    fwd = pltpu.roll(x, SLAB - shift, axis=1)
    bwd = pltpu.roll(x, shift, axis=1)
    partner = jnp.where((lane & (2 * shift - 1)) < shift, fwd, bwd)
    return x * cos + partner * sin


def _ada_kernel(c_ref, w_ref, b_ref, o_ref):
    cv = c_ref[...]
    s = (cv * _sigmoid(cv)).astype(BF)
    o_ref[...] = jnp.dot(s, w_ref[...].astype(BF), preferred_element_type=F32) + b_ref[...]


def _ada(cvec, w_ada, b_ada):
    depth, d, n = w_ada.shape
    rows = cvec.shape[0]
    bn = _pick(n, (1024, 512, 256, 128))
    return pl.pallas_call(
        _ada_kernel,
        grid=(depth, n // bn),
        in_specs=[
            pl.BlockSpec((rows, d), lambda l, j: (0, 0)),
            pl.BlockSpec((None, d, bn), lambda l, j: (l, 0, j)),
            pl.BlockSpec((None, 1, bn), lambda l, j: (l, 0, j)),
        ],
        out_specs=pl.BlockSpec((None, rows, bn), lambda l, j: (l, 0, j)),
        out_shape=jax.ShapeDtypeStruct((depth, rows, n), F32),
        compiler_params=_params(("arbitrary", "arbitrary")),
        name="ada",
    )(cvec, w_ada, b_ada.reshape(depth, 1, n))


def _mod_spec(layer, k, row_fn, nargs):
    if nargs == 1:
        return lambda d: pl.BlockSpec((None, None, None, 1, d), lambda i: (layer, row_fn(i), k, 0, 0))
    return lambda d: pl.BlockSpec((None, None, None, 1, d), lambda i, j: (layer, row_fn(i), k, 0, 0))


def _inproj_kernel(x_ref, sh_ref, sc_ref, g_ref, w_ref, cd_ref, sd_ref, cs_ref, ss_ref, o_ref, vt_ref, h_scr, *, bm):
    g, sh, sc = g_ref[...], sh_ref[...], sc_ref[...]
    group = 4
    rc = min(bm, ROW_CHUNK)
    for lo in range(0, bm, rc):
        for r in range(lo, lo + rc, NORM_CHUNK):
            h_scr[r:r + NORM_CHUNK, :] = _rms_mod(x_ref[r:r + NORM_CHUNK, :], g, sh, sc).astype(BF)
        rows = slice(lo, lo + rc)
        for grp in range(N_SLABS // group):
            acc = jnp.dot(h_scr[rows, :], w_ref[:, grp * group * SLAB:(grp + 1) * group * SLAB],
                          preferred_element_type=F32)
            for t in range(group):
                slab = grp * group + t
                a = acc[:, t * SLAB:(t + 1) * SLAB]
                if slab < DV0:
                    a = _rope(a, cd_ref[rows, :], sd_ref[rows, :], DIFF_QK_DIM // 4)
                    if slab < DK0:
                        a = a * DIFF_QSCALE
                elif SQ0 <= slab < SV0:
                    a = _rope(a, cs_ref[rows, :], ss_ref[rows, :], SWA_HEAD_DIM // 4)
                    if slab < SK0:
                        a = a * SWA_QSCALE
                elif DV0 <= slab < SQ0:
                    vt_ref[slab - DV0, 0, :, lo:lo + rc] = jnp.transpose(a).astype(BF)
                o_ref[rows, slab * SLAB:(slab + 1) * SLAB] = a.astype(BF)


def _inproj(x, mods, layer, row_fn, g_pre, w, tables, seq, bm):
    rows, d = x.shape
    nt = seq // bm
    mod = lambda k: _mod_spec(layer, k, row_fn, 1)(d)
    tab = pl.BlockSpec((bm, SLAB), lambda i: (i % nt, 0))
    return pl.pallas_call(
        functools.partial(_inproj_kernel, bm=bm),
        grid=(rows // bm,),
        in_specs=[
            pl.BlockSpec((bm, d), lambda i: (i, 0)),
            mod(0), mod(1),
            pl.BlockSpec((1, d), lambda i: (0, 0)),
            _resident((None, d, IN_W), lambda i: (layer, 0, 0)),
            tab, tab, tab, tab,
        ],
        out_specs=[pl.BlockSpec((bm, IN_W), lambda i: (i, 0)),
                   pl.BlockSpec((DIFF_HEADS, 1, DIFF_V_DIM, bm), lambda i: (0, i, 0, 0))],
        out_shape=[jax.ShapeDtypeStruct((rows, IN_W), BF),
                   jax.ShapeDtypeStruct((DIFF_HEADS, rows // bm, DIFF_V_DIM, bm), BF)],
        scratch_shapes=[pltpu.VMEM((bm, d), BF)],
        compiler_params=_params(("arbitrary",)),
        name="inproj",
    )(x, mods, mods, g_pre, w, *tables)


ONES_ROWS = 16
DIFF_SUB_BLOCKS = 4
DIFF_UNROLL = 64
DIFF_HAZARD = 64.0


def _diff_kernel(*refs, n_lat, unroll, n_sub, lagged):
    n_out = 2 if lagged else 1
    ins, outs = refs[:-n_out], refs[-n_out:]
    if n_lat:
        lam_ref, sub_ref, q_ref, kl_ref, vl_ref, kc_ref, vc_ref = ins
    else:
        lam_ref, sub_ref, q_ref, kc_ref, vc_ref = ins
    o_ref = outs[0]
    lp = lam_ref[...]
    lam_init = lp[4:5, 0:1]
    lam = (jnp.exp(jnp.sum(lp[0:1] * lp[1:2], axis=-1, keepdims=True))
           - jnp.exp(jnp.sum(lp[2:3] * lp[3:4], axis=-1, keepdims=True)) + lam_init)

    bq = q_ref.shape[0] // n_sub
    n_chunks = n_lat + 1
    keys = lambda t: kc_ref[...] if isinstance(t, int) and t == n_lat else kl_ref[t]
    values = lambda t: vc_ref[...] if isinstance(t, int) and t == n_lat else vl_ref[t]

    def run_steps(step, carry, first, n_loop):
        trips = n_loop // unroll

        def body(i, carry):
            for u in range(unroll):
                carry = step(first + i * unroll + u, carry)
            return carry

        if trips:
            carry = lax.fori_loop(0, trips, body, carry)
        for t in range(first + trips * unroll, n_chunks):
            carry = step(t, carry)
        return carry

    def one_block(sb):
        rows = slice(sb * bq, (sb + 1) * bq)
        q = q_ref[rows, :].astype(F32)
        lane = lax.broadcasted_iota(jnp.int32, q.shape, 1)
        q_both = jnp.concatenate([jnp.where(lane < DIFF_QK_DIM, q, 0.0), jnp.where(lane >= DIFF_QK_DIM, q, 0.0)],
                                 axis=0)
        q_t = jnp.transpose(q_both).astype(BF)

        def qk(t):
            return jnp.dot(keys(t), q_t, preferred_element_type=F32)

        def pv(t, p):
            v_t = values(t)
            v_ext = jnp.concatenate([v_t, jnp.ones((ONES_ROWS, v_t.shape[1]), BF)], axis=0)
            return jnp.dot(v_ext, p, preferred_element_type=F32)

        colmax = lambda s: jnp.max(s, axis=0, keepdims=True)

        if lagged:
            s = qk(0)
            r = colmax(s)
            acc = jnp.zeros((DIFF_V_DIM + ONES_ROWS, 2 * bq), F32)

            def absorb(t, s, r_prev, r, acc, excess):
                p = jnp.exp2(s - r).astype(BF)
                cm = colmax(s)
                acc = jnp.exp2(r_prev - r) * acc + pv(t, p)
                return r, jnp.maximum(r, cm), acc, jnp.maximum(excess, cm - r)

            def step(t, carry):
                s, state = carry
                s_new = qk(t)
                return s_new, absorb(t - 1, s, *state)

            carry = (s, (r, r, acc, jnp.zeros_like(r)))
            s, state = run_steps(step, carry, 1, max(n_lat - 1, 0))
            _, _, acc, excess = absorb(n_chunks - 1, s, *state)
            outs[1][sb:sb + 1, :] = excess
        else:
            def scores(t):
                s = qk(t)
                return s, colmax(s)

            def softmax(sc, m):
                s, smax = sc
                mn = jnp.maximum(m, smax)
                return mn, jnp.exp2(m - mn), jnp.exp2(s - mn).astype(BF)

            m = jnp.full((1, 2 * bq), NEG_INF, F32)
            acc = jnp.zeros((DIFF_V_DIM + ONES_ROWS, 2 * bq), F32)
            sc = scores(0)
            if n_chunks > 1:
                sc_next = scores(1)
                m, alpha, p = softmax(sc, m)

                def step(t, carry):
                    sc, p, alpha, m, acc = carry
                    sc_new = scores(t)
                    m_new, alpha_new, p_new = softmax(sc, m)
                    acc = alpha * acc + pv(t - 2, p)
                    return sc_new, p_new, alpha_new, m_new, acc

                sc, p, alpha, m, acc = run_steps(step, (sc_next, p, alpha, m, acc), 2, max(n_lat - 2, 0))
                acc = alpha * acc + pv(n_chunks - 2, p)
            m, alpha, p = softmax(sc, m)
            acc = alpha * acc + pv(n_chunks - 1, p)

        o_t = acc[:DIFF_V_DIM] / acc[DIFF_V_DIM:DIFF_V_DIM + 1]
        o = jnp.transpose(o_t[:, :bq]) - lam * jnp.transpose(o_t[:, bq:])
        ms = jnp.mean(o * o, axis=-1, keepdims=True)
        o_ref[rows, :] = (o * lax.rsqrt(ms + EPS) * sub_ref[...] * (1.0 - lam_init)).astype(BF)

    for sb in range(n_sub):
        one_block(sb)


def _diff_attention(p_q, p_lat, p_ctx, vt_lat, vt_ctx, lam_par, subln, bq, bk):
    b, sq, _ = p_q.shape
    c = p_ctx.shape[1]
    vw = DIFF_HEADS * DIFF_V_DIM
    n_lat = 0 if p_lat is None else p_lat.shape[1] // bk
    n_sub = DIFF_SUB_BLOCKS if sq % (DIFF_SUB_BLOCKS * bq) == 0 else 1
    qb = n_sub * bq
    in_specs = [
        pl.BlockSpec(lam_par.shape, lambda bi, h, qi: (0, 0)),
        pl.BlockSpec((1, DIFF_V_DIM), lambda bi, h, qi: (0, 0)),
        pl.BlockSpec((None, qb, SLAB), lambda bi, h, qi: (bi, qi, DQ0 + h)),
    ]
    args = [lam_par, subln, p_q]
    if n_lat:
        k_lat = p_lat.reshape(b, n_lat, bk, IN_W)
        in_specs += [pl.BlockSpec((None, n_lat, bk, SLAB), lambda bi, h, qi: (bi, 0, 0, DK0 + h)),
                     pl.BlockSpec((None, n_lat, DIFF_V_DIM, bk), lambda bi, h, qi: (h, bi, 0, 0))]
        args += [k_lat, vt_lat]
    in_specs += [pl.BlockSpec((None, c, SLAB), lambda bi, h, qi: (bi, 0, DK0 + h)),
                 pl.BlockSpec((None, None, DIFF_V_DIM, c), lambda bi, h, qi: (h, bi, 0, 0))]
    args += [p_ctx, vt_ctx]
    nq = sq // qb

    def call(lagged):
        out_specs = [pl.BlockSpec((None, qb, SLAB), lambda bi, h, qi: (bi, qi, h))]
        out_shape = [jax.ShapeDtypeStruct((b, sq, vw), BF)]
        if lagged:
            out_specs.append(pl.BlockSpec((None, None, None, n_sub, 2 * bq), lambda bi, h, qi: (bi, h, qi, 0, 0)))
            out_shape.append(jax.ShapeDtypeStruct((b, DIFF_HEADS, nq, n_sub, 2 * bq), F32))
        return pl.pallas_call(
            functools.partial(_diff_kernel, n_lat=n_lat, unroll=DIFF_UNROLL, n_sub=n_sub, lagged=lagged),
            grid=(b, DIFF_HEADS, nq),
            in_specs=in_specs,
            out_specs=out_specs,
            out_shape=out_shape,
            compiler_params=_params(("arbitrary", "arbitrary", "arbitrary")),
            name=("diff_attn_lagged" if lagged else "diff_attn") if n_lat else "diff_attn_ctx",
        )(*args)

    if not n_lat:
        return call(False)[0]
    od, excess = call(True)
    safe = jnp.max(excess) <= DIFF_HAZARD
    return lax.cond(safe, lambda: od, lambda: call(False)[0])


SWA_SUB_BLOCKS = 4


def _swa_kernel(*refs, band, nsb):
    blk = WINDOW
    if band:
        sink_ref, *mask_refs = refs[:1 + nsb]
        q_ref, kp_ref, kz_ref, kn_ref, vp_ref, vz_ref, vn_ref, kc_ref, vc_ref, o_ref = refs[1 + nsb:]
    else:
        sink_ref, q_ref, kc_ref, vc_ref, o_ref = refs
    rows = SWA_GROUP * blk
    rowc = lax.broadcasted_iota(jnp.int32, (rows, 1), 0)
    for j in range(nsb):
        qr = slice(j * blk, (j + 1) * blk)
        for g in range(SWA_KV_HEADS):
            gs = slice(g * SLAB, (g + 1) * SLAB)
            q4 = jnp.concatenate([q_ref[qr, (g * SWA_GROUP + h) * SLAB:(g * SWA_GROUP + h + 1) * SLAB]
                                  for h in range(SWA_GROUP)], axis=0)
            if band:
                kparts = [kp_ref[:, gs]] + [kz_ref[i * blk:(i + 1) * blk, gs] for i in range(nsb)] + [kn_ref[:, gs]]
                vparts = [vp_ref[:, gs]] + [vz_ref[i * blk:(i + 1) * blk, gs] for i in range(nsb)] + [vn_ref[:, gs]]
                kcat = jnp.concatenate(kparts[j:j + 3] + [kc_ref[:, gs]], axis=0)
                vcat = jnp.concatenate(vparts[j:j + 3] + [vc_ref[:, gs]], axis=0)
            else:
                kcat, vcat = kc_ref[:, gs], vc_ref[:, gs]
            s = lax.dot_general(q4, kcat, (((1,), (1,)), ((), ())), preferred_element_type=F32)
            if band:
                s = s + mask_refs[j][...]
            sk = jnp.full((rows, 1), sink_ref[g * SWA_GROUP + SWA_GROUP - 1], F32)
            for h in range(SWA_GROUP - 2, -1, -1):
                sk = jnp.where(rowc < (h + 1) * blk, sink_ref[g * SWA_GROUP + h], sk)
            sk = sk * LOG2E
            m = jnp.maximum(jnp.max(s, axis=-1, keepdims=True), sk)
            p = jnp.exp2(s - m)
            l = jnp.sum(p, axis=-1, keepdims=True) + jnp.exp2(sk - m)
            o = jnp.dot(p.astype(BF), vcat, preferred_element_type=F32) / l
            for h in range(SWA_GROUP):
                hs = (g * SWA_GROUP + h) * SLAB
                o_ref[qr, hs:hs + SLAB] = o[h * blk:(h + 1) * blk, :].astype(BF)


def _swa_attention(p_q, p_ctx, sink, band):
    b, sq, _ = p_q.shape
    c = p_ctx.shape[1]
    blk = WINDOW
    nb = sq // blk
    nsb = SWA_SUB_BLOCKS if nb % SWA_SUB_BLOCKS == 0 else 1
    qw = SWA_HEADS * SLAB
    kvw = SWA_KV_HEADS * SLAB
    in_specs = [
        pl.BlockSpec(memory_space=pltpu.SMEM),
        pl.BlockSpec((None, nsb * blk, qw), lambda bi, n: (bi, n, SQ0 * SLAB // qw)),
    ]
    args = [sink, p_q]
    if band:
        r = np.arange(SWA_GROUP * blk)[:, None] % blk
        col = np.arange(3 * blk + c)[None, :]
        in_band = (np.abs(r - (col - blk)) <= WINDOW) | (col >= 3 * blk)
        variants = [in_band & ~((col < blk) & bool(v & 1)) & ~((col >= 2 * blk) & (col < 3 * blk) & bool(v & 2))
                    for v in range(4)]
        mask = jnp.asarray(np.where(np.stack(variants), 0.0, NEG_INF).astype(np.float32))
        for j in range(nsb):
            in_specs.insert(1 + j, pl.BlockSpec(
                (None,) + mask.shape[1:],
                lambda bi, n, j=j: (jnp.where(n * nsb + j == 0, 1, 0) + jnp.where(n * nsb + j == nb - 1, 2, 0), 0, 0)))
            args.insert(1 + j, mask)
        for base in (SK0 * SLAB // kvw, SV0 * SLAB // kvw):
            in_specs += [
                pl.BlockSpec((None, blk, kvw), lambda bi, n, base=base: (bi, jnp.maximum(n * nsb - 1, 0), base)),
                pl.BlockSpec((None, nsb * blk, kvw), lambda bi, n, base=base: (bi, n, base)),
                pl.BlockSpec((None, blk, kvw), lambda bi, n, base=base: (bi, jnp.minimum((n + 1) * nsb, nb - 1), base)),
            ]
            args += [p_q, p_q, p_q]
    in_specs += [pl.BlockSpec((None, c, kvw), lambda bi, n: (bi, 0, SK0 * SLAB // kvw)),
                 pl.BlockSpec((None, c, kvw), lambda bi, n: (bi, 0, SV0 * SLAB // kvw))]
    args += [p_ctx, p_ctx]
    return pl.pallas_call(
        functools.partial(_swa_kernel, band=band, nsb=nsb),
        grid=(b, nb // nsb),
        in_specs=in_specs,
        out_specs=pl.BlockSpec((None, nsb * blk, qw), lambda bi, n: (bi, n, 0)),
        out_shape=jax.ShapeDtypeStruct((b, sq, qw), BF),
        compiler_params=_params(("arbitrary", "arbitrary")),
        name="swa_attn" if band else "swa_attn_ctx",
    )(*args)


def _matmul_post_norm(operands, x_ref, gt_ref, g_ref, o_ref, f_scr, bm):
    gt, g = gt_ref[...], g_ref[...]
    rc = min(bm, ROW_CHUNK)
    for lo in range(0, bm, rc):
        rows = slice(lo, lo + rc)
        f_scr[rows, :] = sum(jnp.dot(lhs[rows, :], w[...], preferred_element_type=F32) for lhs, w in operands)
        for r in range(lo, lo + rc, NORM_CHUNK):
            sub = slice(r, r + NORM_CHUNK)
            f = f_scr[sub, :]
            ms = jnp.mean(f * f, axis=-1, keepdims=True)
            o_ref[sub, :] = x_ref[sub, :] + gt * (f * lax.rsqrt(ms + EPS) * g)


def _outproj_kernel(od_ref, os_ref, wt_ref, wb_ref, x_ref, gt_ref, g_ref, o_ref, mix_scr, *, bm):
    _matmul_post_norm(((od_ref, wt_ref), (os_ref, wb_ref)), x_ref, gt_ref, g_ref, o_ref, mix_scr, bm)


def _outproj(od, osw, w_out, x, mods, layer, row_fn, g_post, bm):
    rows, d = x.shape
    half = od.shape[1]
    return pl.pallas_call(
        functools.partial(_outproj_kernel, bm=bm),
        grid=(rows // bm,),
        in_specs=[
            pl.BlockSpec((bm, half), lambda i: (i, 0)),
            pl.BlockSpec((bm, half), lambda i: (i, 0)),
            _resident((None, half, d), lambda i: (layer, 0, 0)),
            _resident((None, half, d), lambda i: (layer, 1, 0)),
            pl.BlockSpec((bm, d), lambda i: (i, 0)),
            _mod_spec(layer, 2, row_fn, 1)(d),
            pl.BlockSpec((1, d), lambda i: (0, 0)),
        ],
        out_specs=pl.BlockSpec((bm, d), lambda i: (i, 0)),
        out_shape=jax.ShapeDtypeStruct((rows, d), F32),
        scratch_shapes=[pltpu.VMEM((bm, d), F32)],
        compiler_params=_params(("arbitrary",)),
        name="outproj",
    )(od, osw, w_out, w_out, x, mods, g_post)


HALO = 16
CONV_CHUNK = 512


def _ffn_up_kernel(x_ref, xp_ref, xn_ref, sh_ref, sc_ref, g_ref, wg_ref, wu_ref, cw_ref, cb_ref, o_ref,
                   h_scr, *, bm, seq):
    i = pl.program_id(0)
    j = pl.program_id(1)

    @pl.when(j == 0)
    def _():
        g, sh, sc = g_ref[...], sh_ref[...], sc_ref[...]

        def norm_rows(r, carry):
            rows = pl.multiple_of(r * NORM_CHUNK, NORM_CHUNK)
            h_scr[pl.ds(HALO + rows, NORM_CHUNK), :] = _rms_mod(x_ref[pl.ds(rows, NORM_CHUNK), :], g, sh, sc).astype(BF)
            return carry

        lax.fori_loop(0, bm // NORM_CHUNK, norm_rows, 0)
        prev_ok = ((i * bm) % seq != 0).astype(F32)
        next_ok = (((i + 1) * bm) % seq != 0).astype(F32)
        h_scr[0:HALO, :] = (_rms_mod(xp_ref[...], g, sh, sc) * prev_ok).astype(BF)
        h_scr[HALO + bm:2 * HALO + bm, :] = (_rms_mod(xn_ref[...], g, sh, sc) * next_ok).astype(BF)

    cw0, cw1, cw2, cb = cw_ref[0:1, :], cw_ref[1:2, :], cw_ref[2:3, :], cb_ref[...]
    rc = min(bm, CONV_CHUNK)
    pad = 8
    for r in range(bm // rc):
        lo = r * rc
        g = jnp.dot(h_scr[lo:lo + rc + 2 * HALO, :], wg_ref[...], preferred_element_type=F32)
        u = jnp.dot(h_scr[lo + HALO:lo + HALO + rc, :], wu_ref[...], preferred_element_type=F32)
        win = g[HALO - pad:HALO + rc + pad]
        below = pltpu.roll(win, 1, axis=0)[pad:pad + rc]
        above = pltpu.roll(win, rc + 2 * pad - 1, axis=0)[pad:pad + rc]
        gc = below * cw0 + g[HALO:HALO + rc] * cw1 + above * cw2 + cb
        a = gc * _sigmoid(gc) * u
        o_ref[lo:lo + rc, :] = a.astype(BF)


def _ffn_up(x, mods, layer, row_fn, g_pre, wg, wu, cw, cb, seq, bm, bn):
    rows, d = x.shape
    dff = wg.shape[2]
    hb = bm // HALO
    nh = rows // HALO
    mod = lambda k: _mod_spec(layer, k, row_fn, 2)(d)
    return pl.pallas_call(
        functools.partial(_ffn_up_kernel, bm=bm, seq=seq),
        grid=(rows // bm, dff // bn),
        in_specs=[
            pl.BlockSpec((bm, d), lambda i, j: (i, 0)),
            pl.BlockSpec((HALO, d), lambda i, j: (jnp.maximum(i * hb - 1, 0), 0)),
            pl.BlockSpec((HALO, d), lambda i, j: (jnp.minimum((i + 1) * hb, nh - 1), 0)),
            mod(3), mod(4),
            pl.BlockSpec((1, d), lambda i, j: (0, 0)),
            pl.BlockSpec((None, d, bn), lambda i, j: (layer, 0, j)),
            pl.BlockSpec((None, d, bn), lambda i, j: (layer, 0, j)),
            pl.BlockSpec((3, bn), lambda i, j: (0, j)),
            pl.BlockSpec((1, bn), lambda i, j: (0, j)),
        ],
        out_specs=pl.BlockSpec((bm, bn), lambda i, j: (i, j)),
        out_shape=jax.ShapeDtypeStruct((rows, dff), BF),
        scratch_shapes=[pltpu.VMEM((bm + 2 * HALO, d), BF)],
        compiler_params=_params(("arbitrary", "arbitrary")),
        name="ffn_up",
    )(x, x, x, mods, mods, g_pre, wg, wu, cw, cb)


def _ffn_down_kernel(a_ref, w_ref, x_ref, gt_ref, g_ref, o_ref, f_scr, *, bm):
    _matmul_post_norm(((a_ref, w_ref),), x_ref, gt_ref, g_ref, o_ref, f_scr, bm)


def _ffn_down(a, wd, x, mods, layer, row_fn, g_post, bm):
    rows, d = x.shape
    dff = a.shape[1]
    return pl.pallas_call(
        functools.partial(_ffn_down_kernel, bm=bm),
        grid=(rows // bm,),
        in_specs=[
            pl.BlockSpec((bm, dff), lambda i: (i, 0)),
            _resident((None, dff, d), lambda i: (layer, 0, 0)),
            pl.BlockSpec((bm, d), lambda i: (i, 0)),
            _mod_spec(layer, 5, row_fn, 1)(d),
            pl.BlockSpec((1, d), lambda i: (0, 0)),
        ],
        out_specs=pl.BlockSpec((bm, d), lambda i: (i, 0)),
        out_shape=jax.ShapeDtypeStruct((rows, d), F32),
        scratch_shapes=[pltpu.VMEM((bm, d), F32)],
        compiler_params=_params(("arbitrary",)),
        name="ffn_down",
    )(a, wd, x, mods, g_post)


def _rope_tables(seq, ctx_len):
    t = np.arange(seq)
    row, col = t // GRID_W, t % GRID_W

    def table(half):
        inv = ROPE_THETA ** (-np.arange(half, dtype=np.float64) / half)
        ar, ac = row[:, None] * inv[None, :], col[:, None] * inv[None, :]
        cos = np.concatenate([np.cos(ar), np.cos(ar), np.cos(ac), np.cos(ac)], axis=1)
        sin = np.concatenate([-np.sin(ar), np.sin(ar), -np.sin(ac), np.sin(ac)], axis=1)
        reps = SLAB // cos.shape[1]
        return np.tile(cos, (1, reps)).astype(np.float32), np.tile(sin, (1, reps)).astype(np.float32)

    cd, sd = table(DIFF_QK_DIM // 4)
    cs, ss = table(SWA_HEAD_DIM // 4)
    lat = tuple(jnp.asarray(a) for a in (cd, sd, cs, ss))
    one, zero = jnp.ones((ctx_len, SLAB), F32), jnp.zeros((ctx_len, SLAB), F32)
    return lat, (one, zero, one, zero)


def _pick(n, prefs):
    for p in prefs:
        if n % p == 0:
            return p
    raise ValueError(f"no block size in {prefs} divides {n}")


def kernel(x, c, ctx, c_ctx, w_ada, b_ada, g_pre_mix, g_post_mix, w_in, diff_lambda, diff_subln, swa_sink,
           w_out, g_pre_ffn, g_post_ffn, w_ffn_gate, w_ffn_up, ffn_conv_w, ffn_conv_b, w_ffn_down):
    b, s, d = x.shape
    cl = ctx.shape[1]
    depth = w_ada.shape[0]
    dff = w_ffn_gate.shape[2]
    assert w_in.shape[2] == IN_W and s % WINDOW == 0 and cl % WINDOW == 0

    n_rows = -(-(b + 1) // 8) * 8
    cvec = jnp.concatenate([c, c_ctx[None, :], jnp.zeros((n_rows - b - 1, d), F32)], axis=0)
    mods = _ada(cvec, w_ada, b_ada).reshape(depth, n_rows, 6, 1, d)

    lat_tabs, ctx_tabs = _rope_tables(s, cl)
    bm_in = _pick(s, (512, 256, 128))
    bm_out = _pick(s, (512, 256, 128))
    bm_up = _pick(s, (1024, 512, 256, 128))
    bm_dn = _pick(s, (256, 128))
    bm_c = _pick(cl, (256, 128))
    bn_ff = _pick(dff, (512, 256, 128))
    bq = _pick(s, (256, 128))
    bk = bm_in
    assert bm_c == cl
    bq_c = _pick(cl, (256, 128))

    xl = x.reshape(b * s, d)
    xc = ctx.reshape(b * cl, d)
    ctx_row = lambda i: b

    w_in_l, w_out_l = w_in.astype(BF), w_out.astype(BF)
    wg_l, wu_l, wd_l = w_ffn_gate.astype(BF), w_ffn_up.astype(BF), w_ffn_down.astype(BF)

    for l in range(depth):
        last = l == depth - 1
        lam_init = 0.8 - 0.6 * math.exp(-0.3 * l)
        lam_par = jnp.concatenate([diff_lambda[l], jnp.full_like(diff_lambda[l], lam_init)], axis=0)
        gpm, gqm = g_pre_mix[l][None, :], g_post_mix[l][None, :]
        gpf, gqf = g_pre_ffn[l][None, :], g_post_ffn[l][None, :]
        sub = diff_subln[l][None, :]
        cb = ffn_conv_b[l][None, :]

        p_lat, vt_lat = _inproj(xl, mods, l, lambda i, n=s // bm_in: i // n, gpm, w_in_l, lat_tabs, s, bm_in)
        p_ctx, vt_ctx = _inproj(xc, mods, l, ctx_row, gpm, w_in_l, ctx_tabs, cl, bm_c)
        p_lat3, p_ctx3 = p_lat.reshape(b, s, IN_W), p_ctx.reshape(b, cl, IN_W)
        od = _diff_attention(p_lat3, p_lat3, p_ctx3, vt_lat, vt_ctx, lam_par, sub, bq, bk)
        osw = _swa_attention(p_lat3, p_ctx3, swa_sink[l], band=True)
        xl = _outproj(od.reshape(b * s, -1), osw.reshape(b * s, -1), w_out_l, xl, mods, l,
                      lambda i, n=s // bm_out: i // n, gqm, bm_out)
        if not last:
            odc = _diff_attention(p_ctx3, None, p_ctx3, None, vt_ctx, lam_par, sub, bq_c, bk)
            oswc = _swa_attention(p_ctx3, p_ctx3, swa_sink[l], band=False)
            xc = _outproj(odc.reshape(b * cl, -1), oswc.reshape(b * cl, -1), w_out_l, xc, mods, l, ctx_row, gqm, bm_c)

        a = _ffn_up(xl, mods, l, lambda i, n=s // bm_up: i // n, gpf, wg_l, wu_l, ffn_conv_w[l], cb, s, bm_up, bn_ff)
        xl = _ffn_down(a, wd_l, xl, mods, l, lambda i, n=s // bm_dn: i // n, gqf, bm_dn)
        if not last:
            ac = _ffn_up(xc, mods, l, ctx_row, gpf, wg_l, wu_l, ffn_conv_w[l], cb, cl, bm_c, bn_ff)
            xc = _ffn_down(ac, wd_l, xc, mods, l, ctx_row, gqf, bm_c)
    return xl.reshape(b, s, d)
```
